```python
import math
import jax, jax.numpy as jnp
from jax import lax
import numpy as np

D_MODEL = 1024
BATCH = 32
SEQ = 2048
DEPTH = 1
DEC_BATCH = 32
DEC_SEQ = 16
PAST_LEN = 1024

CHUNK = 64
RET_HEADS = 4
RET_DK = 128
RET_DV = 128
ROPE_BASE = 10000.0
ATT_HEADS = 4
ATT_HEAD_DIM = 128
IDX_HEADS = 8
IDX_DIM = 64
TOPK_MAX = 256
Q_BLOCK = 128
NUM_BUCKETS = 32
MAX_DISTANCE = 128
N_GROUPS = 4
EXPERTS_PER_GROUP = 8
N_EXPERTS = N_GROUPS * EXPERTS_PER_GROUP
TOP_E = 2
D_EXPERT = 256
ROW_BLOCK = 128
EPS = 1e-6
NEG_INF = -1e30
RET_WIDTH = RET_HEADS * RET_DV
ATT_WIDTH = ATT_HEADS * ATT_HEAD_DIM
MIX_WIDTH = RET_WIDTH + ATT_WIDTH
PROJ_SIZES = (RET_HEADS * RET_DK, RET_HEADS * RET_DK, RET_WIDTH, RET_WIDTH, ATT_WIDTH, ATT_WIDTH, ATT_WIDTH, IDX_HEADS * IDX_DIM, IDX_DIM, IDX_HEADS)
N_PROJ = 2 * RET_HEADS * RET_DK + 2 * RET_WIDTH + 3 * ATT_WIDTH + IDX_HEADS * IDX_DIM + IDX_DIM + IDX_HEADS

kernel_name = 'hybrid_retention_dsa_hmoe_stream'


def rmsnorm(x, g):
    xf = x.astype(jnp.float32)
    y = xf * lax.rsqrt(jnp.mean(jnp.square(xf), -1, keepdims=True) + EPS)
    return y.astype(x.dtype) * g


def rotary(x, pos):
    half = x.shape[-1] // 2
    inv = ROPE_BASE ** (-jnp.arange(half, dtype=jnp.float32) / half)
    ang = pos.astype(jnp.float32)[:, None] * inv[None, :]
    cos = jnp.cos(ang)[None, :, None, :]
    sin = jnp.sin(ang)[None, :, None, :]
    x1, x2 = x[..., :half], x[..., half:]
    return jnp.concatenate([x1 * cos - x2 * sin, x1 * sin + x2 * cos], -1)


def head_norm(o, g):
    mu = jnp.mean(o, -1, keepdims=True)
    var = jnp.mean(jnp.square(o - mu), -1, keepdims=True)
    o = (o - mu) * lax.rsqrt(var + EPS)
    return o.reshape(o.shape[0], o.shape[1], -1) * g.astype(jnp.float32)


def retention(q, k, v, s0, cl):
    b, t, nh, dk = q.shape
    dv = v.shape[-1]
    nc = t // cl
    log_g = jnp.log1p(-jnp.exp2(-5.0 - jnp.arange(nh, dtype=jnp.float32)))
    qc = q.reshape(b, nc, cl, nh, dk)
    kc = k.reshape(b, nc, cl, nh, dk)
    vc = v.reshape(b, nc, cl, nh, dv)
    i = jnp.arange(cl, dtype=jnp.float32)
    diff = i[:, None] - i[None, :]
    decay = jnp.where(diff[None] >= 0, jnp.exp(jnp.maximum(diff, 0.0)[None] * log_g[:, None, None]), 0.0)
    scores = jnp.einsum('bcihd,bcjhd->bchij', qc, kc) * decay
    intra = jnp.einsum('bchij,bcjhv->bcihv', scores, vc)
    k_decay = jnp.exp((cl - 1.0 - i)[:, None] * log_g[None, :])
    q_decay = jnp.exp((i + 1.0)[:, None] * log_g[None, :])
    u = jnp.einsum('bcjhd,jh,bcjhv->cbhdv', kc, k_decay, vc)
    chunk_decay = jnp.exp(cl * log_g)[None, :, None, None]

    def step(s, u_c):
        return chunk_decay * s + u_c, s

    s_final, s_prev = lax.scan(step, s0, u)
    inter = jnp.einsum('bcihd,ih,cbhdv->bcihv', qc, q_decay, s_prev)
    return (intra + inter).reshape(b, t, nh, dv), s_final


def t5_bucket(rel):
    nb = NUM_BUCKETS // 2
    max_exact = nb // 2
    base = jnp.where(rel > 0, nb, 0)
    n = jnp.abs(rel)
    nf = jnp.maximum(n, 1).astype(jnp.float32)
    large = max_exact + (jnp.log(nf / max_exact) / math.log(MAX_DISTANCE / max_exact) * (nb - max_exact)).astype(jnp.int32)
    large = jnp.minimum(large, nb - 1)
    return base + jnp.where(n < max_exact, n, large)


def dsa_attend(q, qi, wi, qpos, k, v, ki, rel_bias, n_sel):
    L = k.shape[0]
    idx = jnp.einsum('thd,sd->ths', qi, ki).astype(jnp.float32) * IDX_DIM ** -0.5
    score = jnp.einsum('th,ths->ts', wi.astype(jnp.float32) * IDX_HEADS ** -0.5, jax.nn.relu(idx))
    admissible = (jnp.arange(L)[None, :] // CHUNK) <= (qpos[:, None] // CHUNK)
    score = jnp.where(admissible, score, NEG_INF)
    top_val, sel = lax.top_k(score, n_sel)
    valid = top_val > 0.5 * NEG_INF
    k_sel = k[sel]
    v_sel = v[sel]
    logits = jnp.einsum('thd,tnhd->thn', q, k_sel).astype(jnp.float32) * ATT_HEAD_DIM ** -0.5
    bias = rel_bias[t5_bucket(sel - qpos[:, None])]
    logits = logits + jnp.transpose(bias, (0, 2, 1)).astype(jnp.float32)
    logits = jnp.where(valid[:, None, :], logits, NEG_INF)
    p = jax.nn.softmax(logits, axis=-1)
    return jnp.einsum('thn,tnhd->thd', p.astype(v.dtype), v_sel)


def sparse_attention(q, qi, wi, k, v, ki, qpos, rel_bias):
    t = q.shape[1]
    L = k.shape[1]
    n_sel = min(TOPK_MAX, L // 4)
    blk = min(Q_BLOCK, t)
    nb = t // blk
    qpos_b = qpos.reshape(nb, blk)

    def per_seq(a):
        q1, qi1, wi1, k1, v1, ki1 = a

        def per_block(bq):
            qq, qqi, ww, pp = bq
            return dsa_attend(qq, qqi, ww, pp, k1, v1, ki1, rel_bias, n_sel)

        out = lax.map(per_block, (q1.reshape(nb, blk, ATT_HEADS, ATT_HEAD_DIM), qi1.reshape(nb, blk, IDX_HEADS, IDX_DIM), wi1.reshape(nb, blk, IDX_HEADS), qpos_b))
        return out.reshape(t, ATT_HEADS, ATT_HEAD_DIM)

    return lax.map(per_seq, (q, qi, wi, k, v, ki))


def routed_experts(h, expert, gate, w_gate, w_up, w_down):
    n, d = h.shape
    a = n * TOP_E
    flat_e = expert.reshape(a)
    flat_tok = jnp.repeat(jnp.arange(n, dtype=jnp.int32), TOP_E)
    flat_w = gate.reshape(a)
    order = jnp.argsort(flat_e)
    se = flat_e[order]
    counts = jnp.bincount(flat_e, length=N_EXPERTS)
    padded = (counts + ROW_BLOCK - 1) // ROW_BLOCK * ROW_BLOCK
    ends = jnp.cumsum(padded)
    dest = (ends - padded)[se] + jnp.arange(a) - (jnp.cumsum(counts) - counts)[se]
    n_blocks = (a + N_EXPERTS * (ROW_BLOCK - 1) + ROW_BLOCK - 1) // ROW_BLOCK
    rows = n_blocks * ROW_BLOCK
    buf_tok = jnp.zeros((rows,), jnp.int32).at[dest].set(flat_tok[order])
    buf_w = jnp.zeros((rows,), jnp.float32).at[dest].set(flat_w[order])
    blk_e = jnp.minimum(jnp.searchsorted(ends, jnp.arange(n_blocks) * ROW_BLOCK, side='right'), N_EXPERTS - 1)

    def expert_block(args):
        tok, e = args
        xb = h[tok]
        hid = jax.nn.silu(xb @ w_gate[e]) * (xb @ w_up[e])
        return hid @ w_down[e]

    out = lax.map(expert_block, (buf_tok.reshape(n_blocks, ROW_BLOCK), blk_e)).reshape(rows, d)
    out = out.astype(jnp.float32) * buf_w[:, None]
    return jnp.zeros((n, d), jnp.float32).at[buf_tok].add(out).astype(h.dtype)


def hier_moe(h, w_group, b_group, w_er, b_er, w_gate, w_up, w_down):
    b, t, d = h.shape
    hf = h.reshape(b * t, d)
    n = hf.shape[0]
    g_logits = (hf @ w_group).astype(jnp.float32) + b_group.astype(jnp.float32)
    p_top, g_top = lax.top_k(jax.nn.softmax(g_logits, -1), 1)
    e_logits = jnp.einsum('nd,dge->nge', hf, w_er).astype(jnp.float32) + b_er.astype(jnp.float32)
    e_logits = e_logits[jnp.arange(n), g_top[:, 0]]
    e_val, e_top = lax.top_k(e_logits, TOP_E)
    gate = jax.nn.softmax(e_val, -1) * p_top
    expert = g_top * EXPERTS_PER_GROUP + e_top
    return routed_experts(hf, expert, gate, w_gate, w_up, w_down).reshape(b, t, d)


def encoder_layer(x, past_len, s_ret, past_k, past_v, past_ki, params):
    (rel_bias, g_mix, w_in, g_ret, w_out, g_ffn, w_group, b_group, w_er, b_er, w_gate, w_up, w_down) = params
    b, t, _ = x.shape
    f32 = jnp.float32
    h = rmsnorm(x, g_mix)
    pos = past_len + jnp.arange(t)
    split_points = np.cumsum(PROJ_SIZES)[:-1]
    rq, rk, rv, rg, aq, ak, av, iq, ik, iw = jnp.split(h @ w_in, split_points, axis=-1)
    rq = rotary(rq.reshape(b, t, RET_HEADS, RET_DK).astype(f32), pos)
    rk = rotary(rk.reshape(b, t, RET_HEADS, RET_DK).astype(f32), pos) * RET_DK ** -0.5
    rv = rv.reshape(b, t, RET_HEADS, RET_DV).astype(f32)
    o_ret, s_new = retention(rq, rk, rv, s_ret.astype(f32), min(CHUNK, t))
    o_ret = (jax.nn.silu(rg.astype(f32)) * head_norm(o_ret, g_ret)).astype(x.dtype)
    aq = aq.reshape(b, t, ATT_HEADS, ATT_HEAD_DIM)
    ak = ak.reshape(b, t, ATT_HEADS, ATT_HEAD_DIM)
    av = av.reshape(b, t, ATT_HEADS, ATT_HEAD_DIM)
    iq = iq.reshape(b, t, IDX_HEADS, IDX_DIM)
    keys_k = jnp.concatenate([past_k.astype(ak.dtype), ak], 1)
    keys_v = jnp.concatenate([past_v.astype(av.dtype), av], 1)
    keys_i = jnp.concatenate([past_ki.astype(ik.dtype), ik], 1)
    o_att = sparse_attention(aq, iq, iw, keys_k, keys_v, keys_i, pos, rel_bias)
    mix = jnp.concatenate([o_ret, o_att.reshape(b, t, ATT_WIDTH).astype(x.dtype)], -1)
    x = x + mix @ w_out
    x = x + hier_moe(rmsnorm(x, g_ffn), w_group, b_group, w_er, b_er, w_gate, w_up, w_down)
    return x, (ak, av, ik, s_new.astype(x.dtype))


def setup_inputs(seed: int = 0) -> dict:
    key = jax.random.key(seed)
    ks = jax.random.split(key, 20)
    f32 = jnp.float32

    def nrm(k, shape, scale):
        return jax.random.normal(k, shape, f32) * scale

    return {
        'x_prompt': nrm(ks[0], (BATCH, SEQ, D_MODEL), 1.0),
        'x_sample': nrm(ks[1], (DEC_BATCH, DEC_SEQ, D_MODEL), 1.0),
        'cache_attn_k': nrm(ks[2], (DEPTH, DEC_BATCH, PAST_LEN, ATT_HEADS, ATT_HEAD_DIM), 1.0),
        'cache_attn_v': nrm(ks[3], (DEPTH, DEC_BATCH, PAST_LEN, ATT_HEADS, ATT_HEAD_DIM), 1.0),
        'cache_idx_k': nrm(ks[4], (DEPTH, DEC_BATCH, PAST_LEN, IDX_DIM), 1.0),
        'state_ret': nrm(ks[5], (DEPTH, DEC_BATCH, RET_HEADS, RET_DK, RET_DV), 1.0),
        'rel_bias': nrm(ks[6], (NUM_BUCKETS, ATT_HEADS), 0.1),
        'g_mix': 1.0 + nrm(ks[7], (DEPTH, D_MODEL), 0.01),
        'w_in': nrm(ks[8], (DEPTH, D_MODEL, N_PROJ), D_MODEL ** -0.5),
        'g_ret': 1.0 + nrm(ks[9], (DEPTH, RET_WIDTH), 0.01),
        'w_out': nrm(ks[10], (DEPTH, MIX_WIDTH, D_MODEL), MIX_WIDTH ** -0.5),
        'g_ffn': 1.0 + nrm(ks[11], (DEPTH, D_MODEL), 0.01),
        'w_group': nrm(ks[12], (DEPTH, D_MODEL, N_GROUPS), D_MODEL ** -0.5),
        'b_group': nrm(ks[13], (DEPTH, N_GROUPS), 0.01),
        'w_expert_router': nrm(ks[14], (DEPTH, D_MODEL, N_GROUPS, EXPERTS_PER_GROUP), D_MODEL ** -0.5),
        'b_expert_router': nrm(ks[15], (DEPTH, N_GROUPS, EXPERTS_PER_GROUP), 0.01),
        'w_gate': nrm(ks[16], (DEPTH, N_EXPERTS, D_MODEL, D_EXPERT), D_MODEL ** -0.5),
        'w_up': nrm(ks[17], (DEPTH, N_EXPERTS, D_MODEL, D_EXPERT), D_MODEL ** -0.5),
        'w_down': nrm(ks[18], (DEPTH, N_EXPERTS, D_EXPERT, D_MODEL), D_EXPERT ** -0.5),
        'g_final': 1.0 + nrm(ks[19], (D_MODEL,), 0.01),
    }


def reference(x_prompt, x_sample, cache_attn_k, cache_attn_v, cache_idx_k, state_ret, rel_bias, g_mix, w_in, g_ret, w_out, g_ffn, w_group, b_group, w_expert_router, b_expert_router, w_gate, w_up, w_down, g_final):
    past_len = cache_attn_k.shape[2]
    nb = x_prompt.shape[0]
    hp, hs = x_prompt, x_sample
    st_p, st_s = [], []
    for l in range(DEPTH):
        params = (rel_bias, g_mix[l], w_in[l], g_ret[l], w_out[l], g_ffn[l], w_group[l], b_group[l], w_expert_router[l], b_expert_router[l], w_gate[l], w_up[l], w_down[l])
        empty_kv = jnp.zeros((nb, 0, ATT_HEADS, ATT_HEAD_DIM), hp.dtype)
        empty_ki = jnp.zeros((nb, 0, IDX_DIM), hp.dtype)
        s0 = jnp.zeros((nb, RET_HEADS, RET_DK, RET_DV), hp.dtype)
        hp, sp = encoder_layer(hp, 0, s0, empty_kv, empty_kv, empty_ki, params)
        hs, ss = encoder_layer(hs, past_len, state_ret[l], cache_attn_k[l], cache_attn_v[l], cache_idx_k[l], params)
        st_p.append(sp)
        st_s.append(ss)
    y_prompt = rmsnorm(hp, g_final)
    y_sample = rmsnorm(hs, g_final)
    new_k_p = jnp.stack([s[0] for s in st_p])
    new_v_p = jnp.stack([s[1] for s in st_p])
    new_ki_p = jnp.stack([s[2] for s in st_p])
    ret_p = jnp.stack([s[3] for s in st_p])
    new_k_s = jnp.stack([s[0] for s in st_s])
    new_v_s = jnp.stack([s[1] for s in st_s])
    new_ki_s = jnp.stack([s[2] for s in st_s])
    ret_s = jnp.stack([s[3] for s in st_s])
    return (y_prompt, y_sample, new_k_p, new_v_p, new_ki_p, ret_p, new_k_s, new_v_s, new_ki_s, ret_s)
```

```python
import functools
import math

import jax
import jax.numpy as jnp
import numpy as np
from jax import lax
from jax.experimental import pallas as pl
from jax.experimental.pallas import tpu as pltpu

F32 = jnp.float32
BF16 = jnp.bfloat16
I32 = jnp.int32

CHUNK = 64
RET_HEADS = 4
RET_DK = 128
RET_DV = 128
ROPE_BASE = 10000.0
ATT_HEADS = 4
ATT_HEAD_DIM = 128
IDX_HEADS = 8
IDX_DIM = 64
TOPK_MAX = 256
Q_BLOCK = 128
NUM_BUCKETS = 32
MAX_DISTANCE = 128
N_GROUPS = 4
EXPERTS_PER_GROUP = 8
N_EXPERTS = N_GROUPS * EXPERTS_PER_GROUP
TOP_E = 2
EPS = 1e-6
NEG_INF = -1e30

LANES = 128
VMEM_LIMIT = 56 * 1024 * 1024
RET_CHUNK = 256
INT_MIN = -(2 ** 31)


def _f32_key_const(v):
    b = int(np.array(v, np.float32).view(np.int32))
    return b ^ ((b >> 31) & 0x7FFFFFFF)


HALF_NEG_KEY = _f32_key_const(0.5 * NEG_INF)


def _cparams(sem):
    return pltpu.CompilerParams(dimension_semantics=sem, vmem_limit_bytes=VMEM_LIMIT)


def _inproj_kernel(x_ref, g_ref, wm_ref, wt_ref, rq, rk, rv, rg, aq, ak, av, akb, avb, iq, ik, iw):
    x = x_ref[...]
    h = (x * lax.rsqrt(jnp.mean(jnp.square(x), -1, keepdims=True) + EPS)) * g_ref[...]
    hb = h.astype(BF16)

    def proj(i):
        return jnp.dot(hb, wm_ref[:, i * 512:(i + 1) * 512], preferred_element_type=F32)

    rq[...] = proj(0)
    rk[...] = proj(1)
    rv[...] = proj(2).astype(BF16)
    rg[...] = proj(3)
    aq[...] = proj(4).astype(BF16)
    k = proj(5)
    ak[...] = k
    akb[...] = k.astype(BF16)
    v = proj(6)
    av[...] = v
    avb[...] = v.astype(BF16)
    iq[...] = proj(7).astype(BF16)
    t = jnp.dot(hb, wt_ref[...], preferred_element_type=F32)
    ik[...] = t[:, :IDX_DIM]
    iw[...] = t[:, IDX_DIM:IDX_DIM + IDX_HEADS]


def _inproj(x2, g_mix, w_in):
    n, d = x2.shape
    tm = 256 if n % 256 == 0 else n
    wm = w_in[:, :4096].astype(BF16)
    wt = jnp.pad(w_in[:, 4096:], ((0, 0), (0, LANES - (w_in.shape[1] - 4096)))).astype(BF16)
    row = lambda w, dt: jax.ShapeDtypeStruct((n, w), dt)
    rspec = lambda w: pl.BlockSpec((tm, w), lambda i: (i, 0))
    widths = [(512, F32), (512, F32), (512, BF16), (512, F32), (512, BF16), (512, F32), (512, F32),
              (512, BF16), (512, BF16), (512, BF16), (IDX_DIM, F32), (IDX_HEADS, F32)]
    return pl.pallas_call(
        _inproj_kernel,
        grid=(n // tm,),
        in_specs=[rspec(d), pl.BlockSpec((1, d), lambda i: (0, 0)),
                  pl.BlockSpec((d, 4096), lambda i: (0, 0)), pl.BlockSpec((d, LANES), lambda i: (0, 0))],
        out_specs=[rspec(w) for w, _ in widths],
        out_shape=[row(w, dt) for w, dt in widths],
        compiler_params=_cparams(("parallel",)),
        name="inproj",
    )(x2, g_mix.reshape(1, d), wm, wt)


def _retention_kernel(cd_ref, rq_ref, rk_ref, rv_ref, rg_ref, cos_ref, sin_ref, dmat_ref, qd_ref, kd_ref,
                      gret_ref, s0_ref, o_ref, s_ref):
    c = pl.program_id(1)

    @pl.when(c == 0)
    def _():
        s_ref[...] = s0_ref[...]

    cosf = cos_ref[...]
    sinf = sin_ref[...]
    half = RET_DK // 2

    def rot(x):
        return x * cosf + pltpu.roll(x, half, 1) * sinf

    for h in range(RET_HEADS):
        sl = slice(h * RET_DK, (h + 1) * RET_DK)
        q = rot(rq_ref[:, sl])
        k = rot(rk_ref[:, sl]) * (RET_DK ** -0.5)
        v = rv_ref[:, sl]
        s = s_ref[0, h]
        sc = lax.dot_general(q.astype(BF16), k.astype(BF16), (((1,), (1,)), ((), ())),
                             preferred_element_type=F32) * dmat_ref[h]
        o = jnp.dot(sc.astype(BF16), v, preferred_element_type=F32)
        o = o + jnp.dot((q * qd_ref[:, sl]).astype(BF16), s.astype(BF16), preferred_element_type=F32)
        kdt = jnp.transpose(k * kd_ref[:, sl]).astype(BF16)
        s_ref[0, h] = cd_ref[h] * s + jnp.dot(kdt, v, preferred_element_type=F32)
        mu = jnp.mean(o, -1, keepdims=True)
        var = jnp.mean(jnp.square(o - mu), -1, keepdims=True)
        on = (o - mu) * lax.rsqrt(var + EPS) * gret_ref[:, sl]
        o_ref[:, sl] = (jax.nn.silu(rg_ref[:, sl]) * on).astype(o_ref.dtype)


def _retention(rq, rk, rv, rg, s0, g_ret, b, t, past_len):
    cl = min(RET_CHUNK, t)
    nc = t // cl
    half = RET_DK // 2
    pos = (past_len + jnp.arange(t)).astype(F32)
    inv = ROPE_BASE ** (-jnp.arange(half, dtype=F32) / half)
    ang = pos[:, None] * inv[None, :]
    cosf = jnp.concatenate([jnp.cos(ang), jnp.cos(ang)], -1)
    sinf = jnp.concatenate([-jnp.sin(ang), jnp.sin(ang)], -1)
    log_g = jnp.log1p(-jnp.exp2(-5.0 - jnp.arange(RET_HEADS, dtype=F32)))
    i = jnp.arange(cl, dtype=F32)
    diff = i[:, None] - i[None, :]
    dmat = jnp.where(diff[None] >= 0, jnp.exp(jnp.maximum(diff, 0.0)[None] * log_g[:, None, None]), 0.0)
    kd = jnp.repeat(jnp.exp((cl - 1.0 - i)[:, None] * log_g[None, :]), RET_DK, axis=1)
    qd = jnp.repeat(jnp.exp((i + 1.0)[:, None] * log_g[None, :]), RET_DK, axis=1)
    cd = jnp.exp(cl * log_g)
    w = RET_HEADS * RET_DK
    rspec = pl.BlockSpec((cl, w), lambda bi, ci: (bi * nc + ci, 0))
    cspec = lambda shape: pl.BlockSpec(shape, lambda bi, ci: (0,) * len(shape))
    sspec = pl.BlockSpec((1, RET_HEADS, RET_DK, RET_DV), lambda bi, ci: (bi, 0, 0, 0))
    return pl.pallas_call(
        _retention_kernel,
        grid=(b, nc),
        in_specs=[pl.BlockSpec(memory_space=pltpu.SMEM), rspec, rspec, rspec, rspec,
                  pl.BlockSpec((cl, RET_DK), lambda bi, ci: (ci, 0)),
                  pl.BlockSpec((cl, RET_DK), lambda bi, ci: (ci, 0)),
                  cspec((RET_HEADS, cl, cl)), cspec((cl, w)), cspec((cl, w)), cspec((1, w)), sspec],
        out_specs=[rspec, sspec],
        out_shape=[jax.ShapeDtypeStruct((b * t, w), BF16),
                   jax.ShapeDtypeStruct((b, RET_HEADS, RET_DK, RET_DV), F32)],
        compiler_params=_cparams(("parallel", "arbitrary")),
        name="retention",
    )(cd, rq, rk, rv, rg, cosf, sinf, dmat, qd, kd, g_ret.reshape(1, w), s0)


def _t5_bucket(rel):
    nb = NUM_BUCKETS // 2
    max_exact = nb // 2
    base = jnp.where(rel > 0, nb, 0)
    n = jnp.abs(rel)
    nf = jnp.maximum(n, 1).astype(F32)
    large = max_exact + (jnp.log(nf / max_exact) / math.log(MAX_DISTANCE / max_exact) * (nb - max_exact)).astype(I32)
    large = jnp.minimum(large, nb - 1)
    return base + jnp.where(n < max_exact, n, large)


def _band_kernel(rb_ref, bucket_ref, band_ref):
    bucket = bucket_ref[...]
    for h in range(ATT_HEADS):
        acc = jnp.zeros(bucket.shape, F32)
        for j in range(NUM_BUCKETS):
            acc = jnp.where(bucket == j, rb_ref[j, h], acc)
        band_ref[h] = acc


def _bias_band(rel_bias):
    t = jnp.arange(Q_BLOCK, dtype=I32)[:, None]
    c = jnp.arange(2 * LANES, dtype=I32)[None, :]
    bucket = _t5_bucket(c - LANES - t)
    return pl.pallas_call(
        _band_kernel,
        in_specs=[pl.BlockSpec(memory_space=pltpu.SMEM), pl.BlockSpec(memory_space=pltpu.VMEM)],
        out_specs=pl.BlockSpec(memory_space=pltpu.VMEM),
        out_shape=jax.ShapeDtypeStruct((ATT_HEADS, Q_BLOCK, 2 * LANES), F32),
        name="bias_band",
    )(rel_bias, bucket)


def _order_key(s):
    bits = lax.bitcast_convert_type(s, I32)
    return bits ^ ((bits >> 31) & 0x7FFFFFFF)


def _attention_kernel(rb_ref, q_ref, iq_ref, iw_ref, k_ref, v_ref, ki_ref, band_ref, adm_ref, o_ref,
                      key_s, mask_s, log_s, *, jd0, n_sel, idx_bits):
    tq = q_ref.shape[0]
    jd = jd0 + pl.program_id(1)
    nt = jd + 1
    ones = jnp.ones((LANES, LANES), BF16)
    lane = lax.broadcasted_iota(I32, (tq, LANES), 1)
    nn = (((1,), (1,)), ((), ()))

    wi = iw_ref[...] * (IDX_HEADS ** -0.5)
    wcols = [wi[:, h:h + 1] for h in range(IDX_HEADS)]
    iqp = [iq_ref[:, p * LANES:(p + 1) * LANES] for p in range(IDX_HEADS // 2)]

    def score_tile(j):
        row = pl.multiple_of(j * LANES, LANES)
        acc = jnp.zeros((tq, LANES), F32)
        for h in range(IDX_HEADS):
            kih = ki_ref[pl.ds(row, LANES), (h % 2) * LANES:(h % 2 + 1) * LANES]
            d = lax.dot_general(iqp[h // 2], kih, nn, preferred_element_type=F32)
            acc = acc + wcols[h] * jnp.maximum(d * (IDX_DIM ** -0.5), 0.0)
        return acc

    def score_body(j, carry):
        key_s[j] = _order_key(score_tile(j))
        return carry

    lax.fori_loop(0, jd, score_body, 0)
    key_s[jd] = _order_key(jnp.where(adm_ref[...] > 0.0, score_tile(jd), NEG_INF))

    def count(pred_tile):
        def body(j, cnt):
            return cnt + jnp.dot(pred_tile(j).astype(BF16), ones, preferred_element_type=F32)
        return lax.fori_loop(0, nt, body, jnp.zeros((tq, LANES), F32))

    kf = float(n_sel)

    def bit_body(it, thr):
        cand = thr + lax.shift_left(jnp.int32(1), 31 - it)
        cnt = count(lambda j: jnp.where(key_s[j] >= cand, 1.0, 0.0))
        return jnp.where(cnt >= kf, cand, thr)

    thr = lax.fori_loop(0, 32, bit_body, jnp.full((tq, LANES), INT_MIN, I32))

    need = kf - count(lambda j: jnp.where(key_s[j] > thr, 1.0, 0.0))

    def tie_body(it, j0):
        cand = j0 + lax.shift_left(jnp.int32(1), idx_bits - 1 - it)
        cnt = count(lambda j: jnp.where(key_s[j] == thr, jnp.where(lane + j * LANES < cand, 1.0, 0.0), 0.0))
        return jnp.where(cnt < need, cand, j0)

    j0 = lax.fori_loop(0, idx_bits, tie_body, jnp.zeros((tq, LANES), I32))

    def mask_body(j, carry):
        key = key_s[j]
        sel = jnp.where(key > thr, 1.0, jnp.where(key == thr, jnp.where(lane + j * LANES <= j0, 1.0, 0.0), 0.0))
        sel = jnp.where(key > HALF_NEG_KEY, sel, 0.0)
        mask_s[j] = jnp.where(sel > 0.0, 0.0, NEG_INF)
        return carry

    lax.fori_loop(0, nt, mask_body, 0)

    scale = ATT_HEAD_DIM ** -0.5
    for h in range(ATT_HEADS):
        sl = slice(h * ATT_HEAD_DIM, (h + 1) * ATT_HEAD_DIM)
        qh = q_ref[:, sl]
        far = rb_ref[NUM_BUCKETS // 2 - 1, h]
        band_l = band_ref[h, :, :LANES]
        band_r = band_ref[h, :, LANES:]

        def logits(j, bias):
            row = pl.multiple_of(j * LANES, LANES)
            kh = k_ref[pl.ds(row, LANES), sl]
            return lax.dot_general(qh, kh, nn, preferred_element_type=F32) * scale + bias + mask_s[j]

        def log_body(j, mx):
            lg = logits(j, jnp.where(j == jd - 1, band_l, far))
            log_s[j] = lg
            return jnp.maximum(mx, lg)

        mx = lax.fori_loop(0, jd, log_body, jnp.full((tq, LANES), NEG_INF, F32))
        lg = logits(jd, band_r)
        log_s[jd] = lg
        m = jnp.max(jnp.maximum(mx, lg), -1, keepdims=True)

        def pv_body(j, carry):
            lsum, acc = carry
            row = pl.multiple_of(j * LANES, LANES)
            p = jnp.exp(log_s[j] - m)
            vh = v_ref[pl.ds(row, LANES), sl]
            return lsum + p, acc + jnp.dot(p.astype(BF16), vh, preferred_element_type=F32)

        lsum, acc = lax.fori_loop(0, nt, pv_body, (jnp.zeros((tq, LANES), F32), jnp.zeros((tq, ATT_HEAD_DIM), F32)))
        o_ref[:, sl] = (acc / jnp.sum(lsum, -1, keepdims=True)).astype(o_ref.dtype)


def _attention(aq, iq, iw, keys_k, keys_v, keys_i, band, rel_bias, b, t, past_len):
    l = past_len + t
    tq = min(Q_BLOCK, t)
    nqb = t // tq
    assert past_len % LANES == 0 and (tq == Q_BLOCK or nqb == 1) and tq % 16 == 0
    jd0 = past_len // LANES
    ntiles = jd0 + nqb
    lp = ntiles * LANES
    n_sel = min(TOPK_MAX, l // 4)
    pad = ((0, 0), (0, lp - l), (0, 0))
    kk = jnp.pad(keys_k, pad)
    vv = jnp.pad(keys_v, pad)
    z = jnp.zeros_like(keys_i)
    ki2 = jnp.pad(jnp.concatenate([keys_i, z, z, keys_i], -1), pad)
    tt = jnp.arange(tq)[:, None]
    cc = jnp.arange(LANES)[None, :]
    last_tile = (jd0 + nqb - 1) * LANES
    adm = ((cc // CHUNK <= tt // CHUNK) & (last_tile + cc < l if nqb == 1 else cc < LANES)).astype(F32)
    w = ATT_HEADS * ATT_HEAD_DIM
    qspec = lambda width: pl.BlockSpec((tq, width), lambda bi, qi: (bi * nqb + qi, 0))
    kspec = lambda width: pl.BlockSpec((None, lp, width), lambda bi, qi: (bi, 0, 0))
    kern = functools.partial(_attention_kernel, jd0=jd0, n_sel=n_sel, idx_bits=max((lp - 1).bit_length(), 1))
    return pl.pallas_call(
        kern,
        grid=(b, nqb),
        in_specs=[pl.BlockSpec(memory_space=pltpu.SMEM), qspec(w), qspec(IDX_HEADS * IDX_DIM), qspec(IDX_HEADS),
                  kspec(w), kspec(w), kspec(2 * LANES),
                  pl.BlockSpec((ATT_HEADS, tq, 2 * LANES), lambda bi, qi: (0, 0, 0)),
                  pl.BlockSpec((tq, LANES), lambda bi, qi: (0, 0))],
        out_specs=qspec(w),
        out_shape=jax.ShapeDtypeStruct((b * t, w), BF16),
        scratch_shapes=[pltpu.VMEM((ntiles, tq, LANES), I32), pltpu.VMEM((ntiles, tq, LANES), F32),
                        pltpu.VMEM((ntiles, tq, LANES), F32)],
        compiler_params=_cparams(("parallel", "arbitrary")),
        name="attention",
    )(rel_bias, aq, iq, iw, kk, vv, ki2, band, adm)


def _outproj_kernel(x_ref, oret_ref, oatt_ref, wo_ref, g_ref, wr_ref, br_ref, x1_ref, h2_ref, e_ref, gt_ref):
    nr = oret_ref.shape[1]
    mix = jnp.dot(oret_ref[...], wo_ref[:nr, :], preferred_element_type=F32)
    mix = mix + jnp.dot(oatt_ref[...], wo_ref[nr:, :], preferred_element_type=F32)
    x1 = x_ref[...] + mix
    x1_ref[...] = x1
    h2 = (x1 * lax.rsqrt(jnp.mean(jnp.square(x1), -1, keepdims=True) + EPS)) * g_ref[...]
    h2_ref[...] = h2
    lg = jnp.dot(h2.astype(BF16), wr_ref[...], preferred_element_type=F32) + br_ref[...]
    tm = lg.shape[0]
    gl = lg[:, :N_GROUPS]
    gmax = jnp.max(gl, -1, keepdims=True)
    p_top = 1.0 / jnp.sum(jnp.exp(gl - gmax), -1, keepdims=True)
    gi = lax.broadcasted_iota(I32, (tm, N_GROUPS), 1)
    g_top = jnp.min(jnp.where(gl == gmax, gi, N_GROUPS), -1, keepdims=True)
    el = jnp.zeros((tm, EXPERTS_PER_GROUP), F32)
    for g in range(N_GROUPS):
        lo = N_GROUPS + g * EXPERTS_PER_GROUP
        el = jnp.where(g_top == g, lg[:, lo:lo + EXPERTS_PER_GROUP], el)
    ei = lax.broadcasted_iota(I32, (tm, EXPERTS_PER_GROUP), 1)
    v1 = jnp.max(el, -1, keepdims=True)
    i1 = jnp.min(jnp.where(el == v1, ei, EXPERTS_PER_GROUP), -1, keepdims=True)
    el2 = jnp.where(ei == i1, -jnp.inf, el)
    v2 = jnp.max(el2, -1, keepdims=True)
    i2 = jnp.min(jnp.where(el2 == v2, ei, EXPERTS_PER_GROUP), -1, keepdims=True)
    e2 = jnp.exp(v2 - v1)
    den = 1.0 + e2
    two = lax.broadcasted_iota(I32, (tm, TOP_E), 1)
    e_ref[...] = g_top * EXPERTS_PER_GROUP + jnp.where(two == 0, i1, i2)
    gt_ref[...] = jnp.where(two == 0, 1.0 / den, e2 / den) * p_top


def _outproj_router(x2, o_ret, o_att, w_out, g_ffn, w_group, b_group, w_er, b_er):
    n, d = x2.shape
    tm = 256 if n % 256 == 0 else n
    nrt = N_GROUPS + N_EXPERTS
    wr = jnp.pad(jnp.concatenate([w_group, w_er.reshape(d, N_EXPERTS)], 1), ((0, 0), (0, LANES - nrt))).astype(BF16)
    br = jnp.pad(jnp.concatenate([b_group, b_er.reshape(N_EXPERTS)]), (0, LANES - nrt)).reshape(1, LANES)
    rspec = lambda w: pl.BlockSpec((tm, w), lambda i: (i, 0))
    cspec = lambda r, c: pl.BlockSpec((r, c), lambda i: (0, 0))
    mw = w_out.shape[0]
    return pl.pallas_call(
        _outproj_kernel,
        grid=(n // tm,),
        in_specs=[rspec(d), rspec(o_ret.shape[1]), rspec(o_att.shape[1]), cspec(mw, d), cspec(1, d),
                  cspec(d, LANES), cspec(1, LANES)],
        out_specs=[rspec(d), rspec(d), rspec(TOP_E), rspec(TOP_E)],
        out_shape=[jax.ShapeDtypeStruct((n, d), F32), jax.ShapeDtypeStruct((n, d), F32),
                   jax.ShapeDtypeStruct((n, TOP_E), I32), jax.ShapeDtypeStruct((n, TOP_E), F32)],
        compiler_params=_cparams(("parallel",)),
        name="outproj_router",
    )(x2, o_ret, o_att, w_out.astype(BF16), g_ffn.reshape(1, d), wr, br)


def _gather_kernel(idx_ref, src_ref, out_ref, sem, *, rows_per_step):
    base = pl.program_id(0) * rows_per_step

    def row_copy(src_row, dst_row):
        return pltpu.make_async_copy(src_ref.at[pl.ds(src_row, 1)], out_ref.at[pl.ds(dst_row, 1)], sem)

    def issue(r, carry):
        row_copy(idx_ref[0, 0, r], base + r).start()
        return carry

    lax.fori_loop(0, rows_per_step, issue, 0)

    def drain(r, carry):
        row_copy(0, base + r).wait()
        return carry

    lax.fori_loop(0, rows_per_step, drain, 0)


def _gather_rows(src, idx):
    m = idx.shape[0]
    g = 512 if m % 512 == 0 else 128
    assert m % g == 0
    return pl.pallas_call(
        functools.partial(_gather_kernel, rows_per_step=g),
        grid=(m // g,),
        in_specs=[pl.BlockSpec((1, 1, g), lambda i: (i, 0, 0), memory_space=pltpu.SMEM),
                  pl.BlockSpec(memory_space=pl.ANY)],
        out_specs=pl.BlockSpec(memory_space=pl.ANY),
        out_shape=jax.ShapeDtypeStruct((m, src.shape[1]), src.dtype),
        scratch_shapes=[pltpu.SemaphoreType.DMA(())],
        compiler_params=_cparams(("arbitrary",)),
        name="gather_rows",
    )(idx.reshape(m // g, 1, g), src)


def _expert_kernel(blk_e_ref, x_ref, wg_ref, wu_ref, wd_ref, o_ref):
    xb = x_ref[...].astype(BF16)
    hid = jax.nn.silu(jnp.dot(xb, wg_ref[...], preferred_element_type=F32))
    hid = hid * jnp.dot(xb, wu_ref[...], preferred_element_type=F32)
    o_ref[...] = jnp.dot(hid.astype(BF16), wd_ref[...], preferred_element_type=F32)


def _experts(xs, blk_e, w_gate, w_up, w_down, rb):
    rows, d = xs.shape
    de = w_gate.shape[2]
    return pl.pallas_call(
        _expert_kernel,
        grid_spec=pltpu.PrefetchScalarGridSpec(
            num_scalar_prefetch=1,
            grid=(rows // rb,),
            in_specs=[pl.BlockSpec((rb, d), lambda i, be: (i, 0)),
                      pl.BlockSpec((None, d, de), lambda i, be: (be[i], 0, 0)),
                      pl.BlockSpec((None, d, de), lambda i, be: (be[i], 0, 0)),
                      pl.BlockSpec((None, de, d), lambda i, be: (be[i], 0, 0))],
            out_specs=pl.BlockSpec((rb, d), lambda i, be: (i, 0)),
        ),
        out_shape=jax.ShapeDtypeStruct((rows, d), F32),
        compiler_params=_cparams(("arbitrary",)),
        name="experts",
    )(blk_e, xs, w_gate.astype(BF16), w_up.astype(BF16), w_down.astype(BF16))


def _route_plan(expert, rb):
    n = expert.shape[0]
    a = n * TOP_E
    flat_e = expert.reshape(a)
    order = jnp.argsort(flat_e, stable=True).astype(I32)
    se = flat_e[order]
    counts = jnp.bincount(flat_e, length=N_EXPERTS).astype(I32)
    padded = (counts + rb - 1) // rb * rb
    ends = jnp.cumsum(padded)
    dest_sorted = (ends - padded)[se] + jnp.arange(a, dtype=I32) - (jnp.cumsum(counts) - counts)[se]
    n_blocks = (a + N_EXPERTS * (rb - 1) + rb - 1) // rb
    src_tok = jnp.zeros((n_blocks * rb,), I32).at[dest_sorted].set(order // TOP_E)
    dest = jnp.zeros((a,), I32).at[order].set(dest_sorted)
    blk_e = jnp.minimum(jnp.searchsorted(ends, jnp.arange(n_blocks, dtype=I32) * rb, side='right'),
                        N_EXPERTS - 1).astype(I32)
    return src_tok, dest, blk_e


def _combine_kernel(x1_ref, eo_ref, gt_ref, g_ref, y_ref):
    d = x1_ref.shape[1]
    gt = gt_ref[...]
    moe = eo_ref[:, :d] * gt[:, 0:1] + eo_ref[:, d:] * gt[:, 1:2]
    x = x1_ref[...] + moe
    y_ref[...] = (x * lax.rsqrt(jnp.mean(jnp.square(x), -1, keepdims=True) + EPS)) * g_ref[...]


def _combine(x1, eo2, gate, g_final):
    n, d = x1.shape
    tm = 256 if n % 256 == 0 else n
    rspec = lambda w: pl.BlockSpec((tm, w), lambda i: (i, 0))
    return pl.pallas_call(
        _combine_kernel,
        grid=(n // tm,),
        in_specs=[rspec(d), rspec(TOP_E * d), rspec(TOP_E), pl.BlockSpec((1, d), lambda i: (0, 0))],
        out_specs=rspec(d),
        out_shape=jax.ShapeDtypeStruct((n, d), F32),
        compiler_params=_cparams(("parallel",)),
        name="combine",
    )(x1, eo2, gate, g_final.reshape(1, d))


def _layer(x, past_len, s_ret, past_k, past_v, past_ki, band, params, g_final):
    (rel_bias, g_mix, w_in, g_ret, w_out, g_ffn, w_group, b_group, w_er, b_er, w_gate, w_up, w_down) = params
    b, t, d = x.shape
    n = b * t
    x2 = x.reshape(n, d)
    rq, rk, rv, rg, aq, ak, av, akb, avb, iq, ik, iw = _inproj(x2, g_mix, w_in)
    o_ret, s_new = _retention(rq, rk, rv, rg, s_ret, g_ret, b, t, past_len)
    aw = ATT_HEADS * ATT_HEAD_DIM
    keys_k = jnp.concatenate([past_k.reshape(b, past_len, aw).astype(BF16), akb.reshape(b, t, aw)], 1)
    keys_v = jnp.concatenate([past_v.reshape(b, past_len, aw).astype(BF16), avb.reshape(b, t, aw)], 1)
    keys_i = jnp.concatenate([past_ki.astype(BF16), ik.reshape(b, t, IDX_DIM).astype(BF16)], 1)
    o_att = _attention(aq, iq, iw, keys_k, keys_v, keys_i, band, rel_bias, b, t, past_len)
    x1, h2, expert, gate = _outproj_router(x2, o_ret, o_att, w_out, g_ffn, w_group, b_group, w_er, b_er)
    rb = 512 if n >= 8192 else 128
    src_tok, dest, blk_e = _route_plan(expert, rb)
    xs = _gather_rows(h2, src_tok)
    eo = _experts(xs, blk_e, w_gate, w_up, w_down, rb)
    eo2 = _gather_rows(eo, dest).reshape(n, TOP_E * d)
    y = _combine(x1, eo2, gate, g_final)
    return (y.reshape(b, t, d), ak.reshape(1, b, t, ATT_HEADS, ATT_HEAD_DIM),
            av.reshape(1, b, t, ATT_HEADS, ATT_HEAD_DIM), ik.reshape(1, b, t, IDX_DIM), s_new[None])


def kernel(x_prompt, x_sample, cache_attn_k, cache_attn_v, cache_idx_k, state_ret, rel_bias, g_mix, w_in, g_ret, w_out, g_ffn, w_group, b_group, w_expert_router, b_expert_router, w_gate, w_up, w_down, g_final):
    assert g_mix.shape[0] == 1, "single-layer model"
    params = (rel_bias, g_mix[0], w_in[0], g_ret[0], w_out[0], g_ffn[0], w_group[0], b_group[0],
              w_expert_router[0], b_expert_router[0], w_gate[0], w_up[0], w_down[0])
    band = _bias_band(rel_bias)
    nb = x_prompt.shape[0]
    past_len = cache_attn_k.shape[2]
    dt = x_prompt.dtype
    empty_kv = jnp.zeros((nb, 0, ATT_HEADS, ATT_HEAD_DIM), dt)
    empty_ki = jnp.zeros((nb, 0, IDX_DIM), dt)
    s0 = jnp.zeros((nb, RET_HEADS, RET_DK, RET_DV), dt)
    yp, kp, vp, kip, sp = _layer(x_prompt, 0, s0, empty_kv, empty_kv, empty_ki, band, params, g_final)
    ys, ks, vs, kis, ss = _layer(x_sample, past_len, state_ret[0], cache_attn_k[0], cache_attn_v[0],
                                 cache_idx_k[0], band, params, g_final)
    return (yp, ys, kp, vp, kip, sp, ks, vs, kis, ss)
```

```python
import functools
import math

import jax
import jax.numpy as jnp
import numpy as np
from jax import lax
from jax.experimental import pallas as pl
from jax.experimental.pallas import tpu as pltpu

F32 = jnp.float32
BF16 = jnp.bfloat16
I32 = jnp.int32

CHUNK = 64
RET_HEADS = 4
RET_DK = 128
RET_DV = 128
ROPE_BASE = 10000.0
ATT_HEADS = 4
ATT_HEAD_DIM = 128
IDX_HEADS = 8
IDX_DIM = 64
TOPK_MAX = 256
NUM_BUCKETS = 32
MAX_DISTANCE = 128
N_GROUPS = 4
EXPERTS_PER_GROUP = 8
N_EXPERTS = N_GROUPS * EXPERTS_PER_GROUP
TOP_E = 2
EPS = 1e-6
NEG_INF = -1e30

LANES = 128
SUBLANES = 8
VMEM_LIMIT = 56 * 1024 * 1024
RET_CHUNK = 256
INT_MIN = -(2 ** 31)
INT_MAX = 2 ** 31 - 1
BAND_TILES = 3
assert (BAND_TILES - 2) * LANES + 1 >= MAX_DISTANCE


def _f32_key_const(v):
    b = int(np.array(v, np.float32).view(np.int32))
    return b ^ ((b >> 31) & 0x7FFFFFFF)


HALF_NEG_KEY = _f32_key_const(0.5 * NEG_INF)


def _cparams(sem):
    return pltpu.CompilerParams(dimension_semantics=sem, vmem_limit_bytes=VMEM_LIMIT)


def _row_tile(n):
    return 256 if n % 256 == 0 else n


def _inproj_kernel(x_ref, g_ref, wm_ref, wt_ref, rq, rk, rv, rg, aq, ak, av, akb, avb, iq, ik, tail):
    x = x_ref[...]
    h = (x * lax.rsqrt(jnp.mean(jnp.square(x), -1, keepdims=True) + EPS)) * g_ref[...]
    hb = h.astype(BF16)

    def proj(i):
        return jnp.dot(hb, wm_ref[:, i * 512:(i + 1) * 512], preferred_element_type=F32)

    rq[...] = proj(0)
    rk[...] = proj(1)
    rv[...] = proj(2).astype(BF16)
    rg[...] = proj(3)
    aq[...] = proj(4).astype(BF16)
    k = proj(5)
    ak[...] = k
    akb[...] = k.astype(BF16)
    v = proj(6)
    av[...] = v
    avb[...] = v.astype(BF16)
    iq[...] = proj(7).astype(BF16)
    t = jnp.dot(hb, wt_ref[...], preferred_element_type=F32)
    ik[...] = t[:, :IDX_DIM]
    tail[...] = t


def _inproj(x2, g_mix, w_in):
    n, d = x2.shape
    tm = _row_tile(n)
    wm = w_in[:, :4096].astype(BF16)
    wt = jnp.pad(w_in[:, 4096:], ((0, 0), (0, LANES - (w_in.shape[1] - 4096)))).astype(BF16)
    row = lambda w, dt: jax.ShapeDtypeStruct((n, w), dt)
    rspec = lambda w: pl.BlockSpec((tm, w), lambda i: (i, 0))
    widths = [(512, F32), (512, F32), (512, BF16), (512, F32), (512, BF16), (512, F32), (512, F32),
              (512, BF16), (512, BF16), (512, BF16), (IDX_DIM, F32), (LANES, F32)]
    return pl.pallas_call(
        _inproj_kernel,
        grid=(n // tm,),
        in_specs=[rspec(d), pl.BlockSpec((1, d), lambda i: (0, 0)),
                  pl.BlockSpec((d, 4096), lambda i: (0, 0)), pl.BlockSpec((d, LANES), lambda i: (0, 0))],
        out_specs=[rspec(w) for w, _ in widths],
        out_shape=[row(w, dt) for w, dt in widths],
        compiler_params=_cparams(("parallel",)),
        name="inproj",
    )(x2, g_mix.reshape(1, d), wm, wt)


def _retention_kernel(cd_ref, rq_ref, rk_ref, rv_ref, rg_ref, cos_ref, sin_ref, dmat_ref, qd_ref, kd_ref,
                      gret_ref, s0_ref, o_ref, s_ref):
    c = pl.program_id(1)

    @pl.when(c == 0)
    def _():
        s_ref[...] = s0_ref[...]

    cosf = cos_ref[...]
    sinf = sin_ref[...]
    half = RET_DK // 2

    def rot(x):
        return x * cosf + pltpu.roll(x, half, 1) * sinf

    for h in range(RET_HEADS):
        sl = slice(h * RET_DK, (h + 1) * RET_DK)
        q = rot(rq_ref[:, sl])
        k = rot(rk_ref[:, sl]) * (RET_DK ** -0.5)
        v = rv_ref[:, sl]
        s = s_ref[0, h]
        sc = lax.dot_general(q.astype(BF16), k.astype(BF16), (((1,), (1,)), ((), ())),
                             preferred_element_type=F32) * dmat_ref[h]
        o = jnp.dot(sc.astype(BF16), v, preferred_element_type=F32)
        o = o + jnp.dot((q * qd_ref[:, sl]).astype(BF16), s.astype(BF16), preferred_element_type=F32)
        kdt = jnp.transpose(k * kd_ref[:, sl]).astype(BF16)
        s_ref[0, h] = cd_ref[h] * s + jnp.dot(kdt, v, preferred_element_type=F32)
        mu = jnp.mean(o, -1, keepdims=True)
        var = jnp.mean(jnp.square(o - mu), -1, keepdims=True)
        on = (o - mu) * lax.rsqrt(var + EPS) * gret_ref[:, sl]
        o_ref[:, sl] = (jax.nn.silu(rg_ref[:, sl]) * on).astype(o_ref.dtype)


def _retention(rq, rk, rv, rg, s0, g_ret, b, t, past_len):
    cl = min(RET_CHUNK, t)
    nc = t // cl
    half = RET_DK // 2
    pos = (past_len + jnp.arange(t)).astype(F32)
    inv = ROPE_BASE ** (-jnp.arange(half, dtype=F32) / half)
    ang = pos[:, None] * inv[None, :]
    cosf = jnp.concatenate([jnp.cos(ang), jnp.cos(ang)], -1)
    sinf = jnp.concatenate([-jnp.sin(ang), jnp.sin(ang)], -1)
    log_g = jnp.log1p(-jnp.exp2(-5.0 - jnp.arange(RET_HEADS, dtype=F32)))
    i = jnp.arange(cl, dtype=F32)
    diff = i[:, None] - i[None, :]
    dmat = jnp.where(diff[None] >= 0, jnp.exp(jnp.maximum(diff, 0.0)[None] * log_g[:, None, None]), 0.0)
    kd = jnp.repeat(jnp.exp((cl - 1.0 - i)[:, None] * log_g[None, :]), RET_DK, axis=1)
    qd = jnp.repeat(jnp.exp((i + 1.0)[:, None] * log_g[None, :]), RET_DK, axis=1)
    cd = jnp.exp(cl * log_g)
    w = RET_HEADS * RET_DK
    rspec = pl.BlockSpec((cl, w), lambda bi, ci: (bi * nc + ci, 0))
    cspec = lambda shape: pl.BlockSpec(shape, lambda bi, ci: (0,) * len(shape))
    sspec = pl.BlockSpec((1, RET_HEADS, RET_DK, RET_DV), lambda bi, ci: (bi, 0, 0, 0))
    return pl.pallas_call(
        _retention_kernel,
        grid=(b, nc),
        in_specs=[pl.BlockSpec(memory_space=pltpu.SMEM), rspec, rspec, rspec, rspec,
                  pl.BlockSpec((cl, RET_DK), lambda bi, ci: (ci, 0)),
                  pl.BlockSpec((cl, RET_DK), lambda bi, ci: (ci, 0)),
                  cspec((RET_HEADS, cl, cl)), cspec((cl, w)), cspec((cl, w)), cspec((1, w)), sspec],
        out_specs=[rspec, sspec],
        out_shape=[jax.ShapeDtypeStruct((b * t, w), BF16),
                   jax.ShapeDtypeStruct((b, RET_HEADS, RET_DK, RET_DV), F32)],
        compiler_params=_cparams(("parallel", "arbitrary")),
        name="retention",
    )(cd, rq, rk, rv, rg, cosf, sinf, dmat, qd, kd, g_ret.reshape(1, w), s0)


def _t5_bucket(rel):
    nb = NUM_BUCKETS // 2
    max_exact = nb // 2
    base = jnp.where(rel > 0, nb, 0)
    n = jnp.abs(rel)
    nf = jnp.maximum(n, 1).astype(F32)
    large = max_exact + (jnp.log(nf / max_exact) / math.log(MAX_DISTANCE / max_exact) * (nb - max_exact)).astype(I32)
    large = jnp.minimum(large, nb - 1)
    return base + jnp.where(n < max_exact, n, large)


def _band_kernel(rb_ref, bucket_ref, band_ref):
    bucket = bucket_ref[...]
    for h in range(ATT_HEADS):
        acc = jnp.zeros(bucket.shape, F32)
        for j in range(NUM_BUCKETS):
            acc = jnp.where(bucket == j, rb_ref[j, h], acc)
        band_ref[h] = acc


def _bias_band(rel_bias):
    c = jnp.arange(BAND_TILES * LANES, dtype=I32)[:, None]
    t = jnp.arange(LANES, dtype=I32)[None, :]
    bucket = _t5_bucket(c - (BAND_TILES - 1) * LANES - t)
    return pl.pallas_call(
        _band_kernel,
        in_specs=[pl.BlockSpec(memory_space=pltpu.SMEM), pl.BlockSpec(memory_space=pltpu.VMEM)],
        out_specs=pl.BlockSpec(memory_space=pltpu.VMEM),
        out_shape=jax.ShapeDtypeStruct((ATT_HEADS, BAND_TILES * LANES, LANES), F32),
        name="bias_band",
    )(rel_bias, bucket)


def _order_key(s):
    bits = lax.bitcast_convert_type(s, I32)
    return bits ^ ((bits >> 31) & 0x7FFFFFFF)


def _fold8(x):
    return jnp.sum(x.reshape(LANES // SUBLANES, SUBLANES, x.shape[1]), axis=0)


def _foldmax8(x):
    return jnp.max(x.reshape(LANES // SUBLANES, SUBLANES, x.shape[1]), axis=0)


def _attention_kernel(q_ref, iq_ref, tail_ref, k_ref, vt_ref, ki_ref, band_ref, adm_ref, o_ref,
                      iqs_s, key_s, mask_s, log_s, acc_s, *, jd0, n_sel, idx_bits):
    tq = LANES
    jd = jd0 + pl.program_id(1)
    nt = jd + 1
    npair = (nt + 1) // 2
    krow = lax.broadcasted_iota(I32, (LANES, tq), 0)

    qt = jnp.transpose(q_ref[...].astype(F32)).astype(BF16)
    iqt = jnp.transpose(iq_ref[...].astype(F32)).astype(BF16)
    wt = jnp.transpose(tail_ref[...])[IDX_DIM:IDX_DIM + IDX_HEADS] * (IDX_HEADS ** -0.5)
    zpad = jnp.zeros((LANES - IDX_DIM, tq), BF16)
    for h in range(IDX_HEADS):
        iqs_s[:, h * tq:(h + 1) * tq] = jnp.concatenate([iqt[h * IDX_DIM:(h + 1) * IDX_DIM], zpad], axis=0)

    def tile_rows(j):
        return pl.ds(pl.multiple_of(j * LANES, LANES), LANES)

    def score_tile(j):
        d = jnp.dot(ki_ref[tile_rows(j), :], iqs_s[...], preferred_element_type=F32)
        acc = jnp.zeros((LANES, tq), F32)
        for h in range(IDX_HEADS):
            acc = acc + wt[h:h + 1] * jnp.maximum(d[:, h * tq:(h + 1) * tq] * (IDX_DIM ** -0.5), 0.0)
        return acc

    def score_body(jp, carry):
        for u in range(2):
            key_s[2 * jp + u] = _order_key(score_tile(2 * jp + u))
        return carry

    lax.fori_loop(0, npair, score_body, 0)
    key_s[jd] = jnp.where(adm_ref[...] > 0.0, key_s[jd], _f32_key_const(NEG_INF))

    @pl.when(nt % 2 == 1)
    def _():
        key_s[nt] = jnp.full((LANES, tq), INT_MIN, I32)

    def count(pred_tile):
        def body(jp, acc):
            return acc + _fold8(pred_tile(2 * jp)) + _fold8(pred_tile(2 * jp + 1))
        acc = lax.fori_loop(0, npair, body, jnp.zeros((SUBLANES, tq), F32))
        return jnp.sum(acc, axis=0, keepdims=True)

    kf = float(n_sel)

    def bit_body(it, thr):
        cand = thr + lax.shift_left(jnp.int32(1), 31 - it)
        cnt = count(lambda j: jnp.where(key_s[j] >= cand, 1.0, 0.0))
        return jnp.where(cnt >= kf, cand, thr)

    thr = lax.fori_loop(0, 32, bit_body, jnp.full((1, tq), INT_MIN, I32))

    need = kf - count(lambda j: jnp.where(key_s[j] > thr, 1.0, 0.0))
    n_tied = count(lambda j: jnp.where(key_s[j] == thr, 1.0, 0.0))
    surplus = jnp.where(thr > HALF_NEG_KEY, n_tied - need, 0.0)

    def tie_search():
        def tie_body(it, j0):
            cand = j0 + lax.shift_left(jnp.int32(1), idx_bits - 1 - it)
            cnt = count(lambda j: jnp.where(key_s[j] == thr, jnp.where(krow + j * LANES < cand, 1.0, 0.0), 0.0))
            return jnp.where(cnt < need, cand, j0)
        return lax.fori_loop(0, idx_bits, tie_body, jnp.zeros((1, tq), I32))

    j0 = lax.cond(jnp.max(surplus) > 0.0, tie_search, lambda: jnp.full((1, tq), INT_MAX, I32))

    def mask_body(jp, carry):
        for u in range(2):
            j = 2 * jp + u
            key = key_s[j]
            sel = jnp.where(key > thr, 1.0, jnp.where(key == thr, jnp.where(krow + j * LANES <= j0, 1.0, 0.0), 0.0))
            sel = jnp.where(key > HALF_NEG_KEY, sel, 0.0)
            mask_s[j] = jnp.where(sel > 0.0, 0.0, NEG_INF)
        return carry

    lax.fori_loop(0, npair, mask_body, 0)

    scale = ATT_HEAD_DIM ** -0.5
    heads = range(ATT_HEADS)
    qh = [qt[h * ATT_HEAD_DIM:(h + 1) * ATT_HEAD_DIM] for h in heads]

    def log_body(jp, mx):
        mx = list(mx)
        for u in range(2):
            j = 2 * jp + u
            band_rows = tile_rows(jnp.clip(j - jd + BAND_TILES - 1, 0, BAND_TILES - 1))
            msk = mask_s[j]
            for h in heads:
                kh = k_ref[tile_rows(j), h * ATT_HEAD_DIM:(h + 1) * ATT_HEAD_DIM]
                lg = jnp.dot(kh, qh[h], preferred_element_type=F32) * scale + band_ref[h, band_rows, :] + msk
                log_s[h, j] = lg
                mx[h] = jnp.maximum(mx[h], _foldmax8(lg))
        return tuple(mx)

    mx = lax.fori_loop(0, npair, log_body, tuple(jnp.full((SUBLANES, tq), NEG_INF, F32) for _ in heads))
    m = [jnp.max(mx[h], axis=0, keepdims=True) for h in heads]
    acc_s[...] = jnp.zeros(acc_s.shape, F32)

    def pv_body(jp, ls):
        ls = list(ls)
        for h in heads:
            acc = acc_s[h]
            for u in range(2):
                j = 2 * jp + u
                p = jnp.exp(log_s[h, j] - m[h])
                ls[h] = ls[h] + _fold8(p)
                vth = vt_ref[j, h * ATT_HEAD_DIM:(h + 1) * ATT_HEAD_DIM, :]
                acc = acc + jnp.dot(vth, p.astype(BF16), preferred_element_type=F32)
            acc_s[h] = acc
        return tuple(ls)

    ls = lax.fori_loop(0, npair, pv_body, tuple(jnp.zeros((SUBLANES, tq), F32) for _ in heads))
    for h in heads:
        ot = acc_s[h] / jnp.sum(ls[h], axis=0, keepdims=True)
        o_ref[:, h * ATT_HEAD_DIM:(h + 1) * ATT_HEAD_DIM] = jnp.transpose(ot).astype(o_ref.dtype)


def _attention(aq, iq, tail, keys_k, keys_v, keys_i, band, b, t, past_len):
    l = past_len + t
    tq = LANES
    nqb = -(-t // tq)
    tp = nqb * tq
    assert past_len % LANES == 0 and (t % tq == 0 or nqb == 1)
    jd0 = past_len // LANES
    ntiles = jd0 + nqb
    ntp = ntiles + ntiles % 2
    lp = ntp * LANES
    n_sel = min(TOPK_MAX, l // 4)
    if tp != t:
        padq = lambda a: jnp.pad(a.reshape(b, t, -1), ((0, 0), (0, tp - t), (0, 0))).reshape(b * tp, -1)
        aq, iq, tail = padq(aq), padq(iq), padq(tail)
    w = ATT_HEADS * ATT_HEAD_DIM
    pad = ((0, 0), (0, lp - l), (0, 0))
    kk = jnp.pad(keys_k, pad)
    vt = jnp.pad(keys_v, pad).reshape(b, ntp, LANES, w).transpose(0, 1, 3, 2)
    ki2 = jnp.pad(keys_i, ((0, 0), (0, lp - l), (0, LANES - IDX_DIM)))
    cc = jnp.arange(LANES)[:, None]
    tt = jnp.arange(tq)[None, :]
    adm = ((cc // CHUNK <= tt // CHUNK) & (cc < l - (ntiles - 1) * LANES)).astype(F32)
    qspec = lambda width: pl.BlockSpec((tq, width), lambda bi, qi: (bi * nqb + qi, 0))
    kspec = lambda width: pl.BlockSpec((None, lp, width), lambda bi, qi: (bi, 0, 0))
    kern = functools.partial(_attention_kernel, jd0=jd0, n_sel=n_sel, idx_bits=max((lp - 1).bit_length(), 1))
    out = pl.pallas_call(
        kern,
        grid=(b, nqb),
        in_specs=[qspec(w), qspec(IDX_HEADS * IDX_DIM), qspec(LANES),
                  kspec(w), pl.BlockSpec((None, ntp, w, LANES), lambda bi, qi: (bi, 0, 0, 0)), kspec(LANES),
                  pl.BlockSpec((ATT_HEADS, BAND_TILES * LANES, tq), lambda bi, qi: (0, 0, 0)),
                  pl.BlockSpec((LANES, tq), lambda bi, qi: (0, 0))],
        out_specs=qspec(w),
        out_shape=jax.ShapeDtypeStruct((b * tp, w), BF16),
        scratch_shapes=[pltpu.VMEM((LANES, IDX_HEADS * tq), BF16),
                        pltpu.VMEM((ntp, LANES, tq), I32), pltpu.VMEM((ntp, LANES, tq), F32),
                        pltpu.VMEM((ATT_HEADS, ntp, LANES, tq), F32), pltpu.VMEM((ATT_HEADS, ATT_HEAD_DIM, tq), F32)],
        compiler_params=_cparams(("parallel", "arbitrary")),
        name="attention",
    )(aq, iq, tail, kk, vt, ki2, band, adm)
    if tp != t:
        out = out.reshape(b, tp, w)[:, :t].reshape(b * t, w)
    return out


def _store_tile_major(ref, x):
    rows, width = x.shape
    nk = width // LANES
    for c in range(nk):
        ref[pl.ds(c, rows, stride=nk), :] = x[:, c * LANES:(c + 1) * LANES]


def _load_tile_major(ref, rows, nk, first=0, stride=None):
    stride = nk if stride is None else stride
    return jnp.concatenate([ref[pl.ds(first + c, rows, stride=stride), :] for c in range(nk)], axis=1)


def _outproj_kernel(x_ref, oret_ref, oatt_ref, wo_ref, g_ref, wr_ref, br_ref, x1_ref, h2_ref, e_ref, gt_ref):
    nr = oret_ref.shape[1]
    mix = jnp.dot(oret_ref[...], wo_ref[:nr, :], preferred_element_type=F32)
    mix = mix + jnp.dot(oatt_ref[...], wo_ref[nr:, :], preferred_element_type=F32)
    x1 = x_ref[...] + mix
    x1_ref[...] = x1
    h2 = (x1 * lax.rsqrt(jnp.mean(jnp.square(x1), -1, keepdims=True) + EPS)) * g_ref[...]
    _store_tile_major(h2_ref, h2)
    lg = jnp.dot(h2.astype(BF16), wr_ref[...], preferred_element_type=F32) + br_ref[...]
    tm = lg.shape[0]
    gl = lg[:, :N_GROUPS]
    gmax = jnp.max(gl, -1, keepdims=True)
    p_top = 1.0 / jnp.sum(jnp.exp(gl - gmax), -1, keepdims=True)
    gi = lax.broadcasted_iota(I32, (tm, N_GROUPS), 1).astype(F32)
    g_top = jnp.min(jnp.where(gl == gmax, gi, float(N_GROUPS)), -1, keepdims=True)
    el = jnp.zeros((tm, EXPERTS_PER_GROUP), F32)
    for g in range(N_GROUPS):
        lo = N_GROUPS + g * EXPERTS_PER_GROUP
        el = jnp.where(g_top == float(g), lg[:, lo:lo + EXPERTS_PER_GROUP], el)
    ei = lax.broadcasted_iota(I32, (tm, EXPERTS_PER_GROUP), 1).astype(F32)
    v1 = jnp.max(el, -1, keepdims=True)
    i1 = jnp.min(jnp.where(el == v1, ei, float(EXPERTS_PER_GROUP)), -1, keepdims=True)
    el2 = jnp.where(ei == i1, -jnp.inf, el)
    v2 = jnp.max(el2, -1, keepdims=True)
    i2 = jnp.min(jnp.where(el2 == v2, ei, float(EXPERTS_PER_GROUP)), -1, keepdims=True)
    e2 = jnp.exp(v2 - v1)
    den = 1.0 + e2
    two = lax.broadcasted_iota(I32, (tm, TOP_E), 1)
    e_ref[...] = (g_top * EXPERTS_PER_GROUP + jnp.where(two == 0, i1, i2)).astype(I32)
    gt_ref[...] = jnp.where(two == 0, 1.0 / den, e2 / den) * p_top


def _outproj_router(x2, o_ret, o_att, w_out, g_ffn, w_group, b_group, w_er, b_er):
    n, d = x2.shape
    tm = _row_tile(n)
    nk = d // LANES
    nrt = N_GROUPS + N_EXPERTS
    wr = jnp.pad(jnp.concatenate([w_group, w_er.reshape(d, N_EXPERTS)], 1), ((0, 0), (0, LANES - nrt))).astype(BF16)
    br = jnp.pad(jnp.concatenate([b_group, b_er.reshape(N_EXPERTS)]), (0, LANES - nrt)).reshape(1, LANES)
    rspec = lambda w: pl.BlockSpec((tm, w), lambda i: (i, 0))
    cspec = lambda r, c: pl.BlockSpec((r, c), lambda i: (0, 0))
    mw = w_out.shape[0]
    return pl.pallas_call(
        _outproj_kernel,
        grid=(n // tm,),
        in_specs=[rspec(d), rspec(o_ret.shape[1]), rspec(o_att.shape[1]), cspec(mw, d), cspec(1, d),
                  cspec(d, LANES), cspec(1, LANES)],
        out_specs=[rspec(d), pl.BlockSpec((tm * nk, LANES), lambda i: (i, 0)), rspec(TOP_E), rspec(TOP_E)],
        out_shape=[jax.ShapeDtypeStruct((n, d), F32), jax.ShapeDtypeStruct((n * nk, LANES), F32),
                   jax.ShapeDtypeStruct((n, TOP_E), I32), jax.ShapeDtypeStruct((n, TOP_E), F32)],
        compiler_params=_cparams(("parallel",)),
        name="outproj_router",
    )(x2, o_ret, o_att, w_out.astype(BF16), g_ffn.reshape(1, d), wr, br)


def _gather_kernel(idx_ref, src_ref, out_ref, sem, *, rows_per_step):
    base = pl.program_id(0) * rows_per_step

    def row_copy(src_row, dst_row):
        return pltpu.make_async_copy(src_ref.at[src_row], out_ref.at[dst_row], sem)

    def issue(r, carry):
        row_copy(idx_ref[0, 0, r], base + r).start()
        return carry

    lax.fori_loop(0, rows_per_step, issue, 0)

    def drain(r, carry):
        row_copy(0, base + r).wait()
        return carry

    lax.fori_loop(0, rows_per_step, drain, 0)


def _gather_rows(src, idx):
    m = idx.shape[0]
    g = 512 if m % 512 == 0 else 128
    assert m % g == 0
    return pl.pallas_call(
        functools.partial(_gather_kernel, rows_per_step=g),
        grid=(m // g,),
        in_specs=[pl.BlockSpec((1, 1, g), lambda i: (i, 0, 0), memory_space=pltpu.SMEM),
                  pl.BlockSpec(memory_space=pl.ANY)],
        out_specs=pl.BlockSpec(memory_space=pl.ANY),
        out_shape=jax.ShapeDtypeStruct((m,) + src.shape[1:], src.dtype),
        scratch_shapes=[pltpu.SemaphoreType.DMA(())],
        compiler_params=_cparams(("arbitrary",)),
        name="gather_rows",
    )(idx.reshape(m // g, 1, g), src)


def _expert_kernel(blk_e_ref, x_ref, wg_ref, wu_ref, wd_ref, o_ref):
    d = wg_ref.shape[0]
    rb = x_ref.shape[0] * LANES // d
    xb = _load_tile_major(x_ref, rb, d // LANES).astype(BF16)
    hid = jax.nn.silu(jnp.dot(xb, wg_ref[...], preferred_element_type=F32))
    hid = hid * jnp.dot(xb, wu_ref[...], preferred_element_type=F32)
    _store_tile_major(o_ref, jnp.dot(hid.astype(BF16), wd_ref[...], preferred_element_type=F32))


def _experts(xs, blk_e, w_gate, w_up, w_down, rb):
    d, de = w_gate.shape[1:]
    nk = d // LANES
    lines = xs.shape[0]
    return pl.pallas_call(
        _expert_kernel,
        grid_spec=pltpu.PrefetchScalarGridSpec(
            num_scalar_prefetch=1,
            grid=(lines // (rb * nk),),
            in_specs=[pl.BlockSpec((rb * nk, LANES), lambda i, be: (i, 0)),
                      pl.BlockSpec((None, d, de), lambda i, be: (be[i], 0, 0)),
                      pl.BlockSpec((None, d, de), lambda i, be: (be[i], 0, 0)),
                      pl.BlockSpec((None, de, d), lambda i, be: (be[i], 0, 0))],
            out_specs=pl.BlockSpec((rb * nk, LANES), lambda i, be: (i, 0)),
        ),
        out_shape=jax.ShapeDtypeStruct((lines, LANES), F32),
        compiler_params=_cparams(("arbitrary",)),
        name="experts",
    )(blk_e, xs, w_gate.astype(BF16), w_up.astype(BF16), w_down.astype(BF16))


def _route_plan(expert, rb):
    n = expert.shape[0]
    a = n * TOP_E
    flat_e = expert.reshape(a)
    order = jnp.argsort(flat_e, stable=True).astype(I32)
    se = flat_e[order]
    counts = jnp.bincount(flat_e, length=N_EXPERTS).astype(I32)
    padded = (counts + rb - 1) // rb * rb
    ends = jnp.cumsum(padded)
    dest_sorted = (ends - padded)[se] + jnp.arange(a, dtype=I32) - (jnp.cumsum(counts) - counts)[se]
    n_blocks = (a + N_EXPERTS * (rb - 1) + rb - 1) // rb
    src_tok = jnp.zeros((n_blocks * rb,), I32).at[dest_sorted].set(order // TOP_E)
    dest = jnp.zeros((a,), I32).at[order].set(dest_sorted)
    blk_e = jnp.minimum(jnp.searchsorted(ends, jnp.arange(n_blocks, dtype=I32) * rb, side='right'),
                        N_EXPERTS - 1).astype(I32)
    return src_tok, dest, blk_e


def _combine_kernel(x1_ref, eo_ref, gt_ref, g_ref, y_ref):
    tm, d = x1_ref.shape
    nk = d // LANES
    gt = gt_ref[...]
    moe = _load_tile_major(eo_ref, tm, nk, 0, TOP_E * nk) * gt[:, 0:1]
    moe = moe + _load_tile_major(eo_ref, tm, nk, nk, TOP_E * nk) * gt[:, 1:2]
    x = x1_ref[...] + moe
    y_ref[...] = (x * lax.rsqrt(jnp.mean(jnp.square(x), -1, keepdims=True) + EPS)) * g_ref[...]


def _combine(x1, eo2, gate, g_final):
    n, d = x1.shape
    tm = _row_tile(n)
    nk = d // LANES
    rspec = lambda w: pl.BlockSpec((tm, w), lambda i: (i, 0))
    return pl.pallas_call(
        _combine_kernel,
        grid=(n // tm,),
        in_specs=[rspec(d), pl.BlockSpec((tm * TOP_E * nk, LANES), lambda i: (i, 0)), rspec(TOP_E),
                  pl.BlockSpec((1, d), lambda i: (0, 0))],
        out_specs=rspec(d),
        out_shape=jax.ShapeDtypeStruct((n, d), F32),
        compiler_params=_cparams(("parallel",)),
        name="combine",
    )(x1, eo2, gate, g_final.reshape(1, d))


def _layer(x, past_len, s_ret, past_k, past_v, past_ki, band, params, g_final):
    (rel_bias, g_mix, w_in, g_ret, w_out, g_ffn, w_group, b_group, w_er, b_er, w_gate, w_up, w_down) = params
    b, t, d = x.shape
    n = b * t
    nk = d // LANES
    x2 = x.reshape(n, d)
    rq, rk, rv, rg, aq, ak, av, akb, avb, iq, ik, tail = _inproj(x2, g_mix, w_in)
    o_ret, s_new = _retention(rq, rk, rv, rg, s_ret, g_ret, b, t, past_len)
    aw = ATT_HEADS * ATT_HEAD_DIM
    keys_k = jnp.concatenate([past_k.reshape(b, past_len, aw).astype(BF16), akb.reshape(b, t, aw)], 1)
    keys_v = jnp.concatenate([past_v.reshape(b, past_len, aw).astype(BF16), avb.reshape(b, t, aw)], 1)
    keys_i = jnp.concatenate([past_ki.astype(BF16), ik.reshape(b, t, IDX_DIM).astype(BF16)], 1)
    o_att = _attention(aq, iq, tail, keys_k, keys_v, keys_i, band, b, t, past_len)
    x1, h2, expert, gate = _outproj_router(x2, o_ret, o_att, w_out, g_ffn, w_group, b_group, w_er, b_er)
    rb = 512 if n >= 8192 else 128
    src_tok, dest, blk_e = _route_plan(expert, rb)
    xs = _gather_rows(h2.reshape(n, nk, LANES), src_tok)
    eo = _experts(xs.reshape(-1, LANES), blk_e, w_gate, w_up, w_down, rb)
    eo2 = _gather_rows(eo.reshape(-1, nk, LANES), dest).reshape(-1, LANES)
    y = _combine(x1, eo2, gate, g_final)
    return (y.reshape(b, t, d), ak.reshape(1, b, t, ATT_HEADS, ATT_HEAD_DIM),
            av.reshape(1, b, t, ATT_HEADS, ATT_HEAD_DIM), ik.reshape(1, b, t, IDX_DIM), s_new[None])


def kernel(x_prompt, x_sample, cache_attn_k, cache_attn_v, cache_idx_k, state_ret, rel_bias, g_mix, w_in, g_ret, w_out, g_ffn, w_group, b_group, w_expert_router, b_expert_router, w_gate, w_up, w_down, g_final):
    assert g_mix.shape[0] == 1, "single-layer model"
    params = (rel_bias, g_mix[0], w_in[0], g_ret[0], w_out[0], g_ffn[0], w_group[0], b_group[0],
              w_expert_router[0], b_expert_router[0], w_gate[0], w_up[0], w_down[0])
    band = _bias_band(rel_bias)
    nb = x_prompt.shape[0]
    past_len = cache_attn_k.shape[2]
    dt = x_prompt.dtype
    empty_kv = jnp.zeros((nb, 0, ATT_HEADS, ATT_HEAD_DIM), dt)
    empty_ki = jnp.zeros((nb, 0, IDX_DIM), dt)
    s0 = jnp.zeros((nb, RET_HEADS, RET_DK, RET_DV), dt)
    yp, kp, vp, kip, sp = _layer(x_prompt, 0, s0, empty_kv, empty_kv, empty_ki, band, params, g_final)
    ys, ks, vs, kis, ss = _layer(x_sample, past_len, state_ret[0], cache_attn_k[0], cache_attn_v[0],
                                 cache_idx_k[0], band, params, g_final)
    return (yp, ys, kp, vp, kip, sp, ks, vs, kis, ss)
```

```python
import functools
import math

import jax
import jax.numpy as jnp
import numpy as np
from jax import lax
from jax.experimental import pallas as pl
from jax.experimental.pallas import tpu as pltpu

F32 = jnp.float32
BF16 = jnp.bfloat16
I32 = jnp.int32

CHUNK = 64
RET_HEADS = 4
RET_DK = 128
RET_DV = 128
ROPE_BASE = 10000.0
ATT_HEADS = 4
ATT_HEAD_DIM = 128
IDX_HEADS = 8
IDX_DIM = 64
TOPK_MAX = 256
NUM_BUCKETS = 32
MAX_DISTANCE = 128
N_GROUPS = 4
EXPERTS_PER_GROUP = 8
N_EXPERTS = N_GROUPS * EXPERTS_PER_GROUP
TOP_E = 2
EPS = 1e-6
NEG_INF = -1e30

LANES = 128
SUBLANES = 8
VMEM_LIMIT = 56 * 1024 * 1024
RET_CHUNK = 256
INT_MIN = -(2 ** 31)
INT_MAX = 2 ** 31 - 1
BAND_TILES = 3
assert (BAND_TILES - 2) * LANES + 1 >= MAX_DISTANCE


def _f32_key_const(v):
    b = int(np.array(v, np.float32).view(np.int32))
    return b ^ ((b >> 31) & 0x7FFFFFFF)


HALF_NEG_KEY = _f32_key_const(0.5 * NEG_INF)


def _cparams(sem):
    return pltpu.CompilerParams(dimension_semantics=sem, vmem_limit_bytes=VMEM_LIMIT)


def _row_tile(n):
    return 256 if n % 256 == 0 else n


def _inproj_kernel(x_ref, g_ref, wm_ref, wt_ref, rq, rk, rv, rg, aq, ak, av, akb, avb, iq, ik, tail):
    x = x_ref[...]
    h = (x * lax.rsqrt(jnp.mean(jnp.square(x), -1, keepdims=True) + EPS)) * g_ref[...]
    hb = h.astype(BF16)

    def proj(i):
        return jnp.dot(hb, wm_ref[:, i * 512:(i + 1) * 512], preferred_element_type=F32)

    rq[...] = proj(0)
    rk[...] = proj(1)
    rv[...] = proj(2).astype(BF16)
    rg[...] = proj(3)
    aq[...] = proj(4).astype(BF16)
    k = proj(5)
    ak[...] = k
    akb[...] = k.astype(BF16)
    v = proj(6)
    av[...] = v
    avb[...] = v.astype(BF16)
    iq[...] = proj(7).astype(BF16)
    t = jnp.dot(hb, wt_ref[...], preferred_element_type=F32)
    ik[...] = t[:, :IDX_DIM]
    tail[...] = t


def _inproj(x2, g_mix, w_in):
    n, d = x2.shape
    tm = _row_tile(n)
    wm = w_in[:, :4096].astype(BF16)
    wt = jnp.pad(w_in[:, 4096:], ((0, 0), (0, LANES - (w_in.shape[1] - 4096)))).astype(BF16)
    row = lambda w, dt: jax.ShapeDtypeStruct((n, w), dt)
    rspec = lambda w: pl.BlockSpec((tm, w), lambda i: (i, 0))
    widths = [(512, F32), (512, F32), (512, BF16), (512, F32), (512, BF16), (512, F32), (512, F32),
              (512, BF16), (512, BF16), (512, BF16), (IDX_DIM, F32), (LANES, F32)]
    return pl.pallas_call(
        _inproj_kernel,
        grid=(n // tm,),
        in_specs=[rspec(d), pl.BlockSpec((1, d), lambda i: (0, 0)),
                  pl.BlockSpec((d, 4096), lambda i: (0, 0)), pl.BlockSpec((d, LANES), lambda i: (0, 0))],
        out_specs=[rspec(w) for w, _ in widths],
        out_shape=[row(w, dt) for w, dt in widths],
        compiler_params=_cparams(("parallel",)),
        name="inproj",
    )(x2, g_mix.reshape(1, d), wm, wt)


def _retention_kernel(cd_ref, rq_ref, rk_ref, rv_ref, rg_ref, cos_ref, sin_ref, dmat_ref, qd_ref, kd_ref,
                      gret_ref, s0_ref, o_ref, s_ref):
    c = pl.program_id(1)

    @pl.when(c == 0)
    def _():
        s_ref[...] = s0_ref[...]

    cosf = cos_ref[...]
    sinf = sin_ref[...]
    half = RET_DK // 2

    def rot(x):
        return x * cosf + pltpu.roll(x, half, 1) * sinf

    for h in range(RET_HEADS):
        sl = slice(h * RET_DK, (h + 1) * RET_DK)
        q = rot(rq_ref[:, sl])
        k = rot(rk_ref[:, sl]) * (RET_DK ** -0.5)
        v = rv_ref[:, sl]
        s = s_ref[0, h]
        sc = lax.dot_general(q.astype(BF16), k.astype(BF16), (((1,), (1,)), ((), ())),
                             preferred_element_type=F32) * dmat_ref[h]
        o = jnp.dot(sc.astype(BF16), v, preferred_element_type=F32)
        o = o + jnp.dot((q * qd_ref[:, sl]).astype(BF16), s.astype(BF16), preferred_element_type=F32)
        kdt = jnp.transpose(k * kd_ref[:, sl]).astype(BF16)
        s_ref[0, h] = cd_ref[h] * s + jnp.dot(kdt, v, preferred_element_type=F32)
        mu = jnp.mean(o, -1, keepdims=True)
        var = jnp.mean(jnp.square(o - mu), -1, keepdims=True)
        on = (o - mu) * lax.rsqrt(var + EPS) * gret_ref[:, sl]
        o_ref[:, sl] = (jax.nn.silu(rg_ref[:, sl]) * on).astype(o_ref.dtype)


def _retention(rq, rk, rv, rg, s0, g_ret, b, t, past_len):
    cl = min(RET_CHUNK, t)
    nc = t // cl
    half = RET_DK // 2
    pos = (past_len + jnp.arange(t)).astype(F32)
    inv = ROPE_BASE ** (-jnp.arange(half, dtype=F32) / half)
    ang = pos[:, None] * inv[None, :]
    cosf = jnp.concatenate([jnp.cos(ang), jnp.cos(ang)], -1)
    sinf = jnp.concatenate([-jnp.sin(ang), jnp.sin(ang)], -1)
    log_g = jnp.log1p(-jnp.exp2(-5.0 - jnp.arange(RET_HEADS, dtype=F32)))
    i = jnp.arange(cl, dtype=F32)
    diff = i[:, None] - i[None, :]
    dmat = jnp.where(diff[None] >= 0, jnp.exp(jnp.maximum(diff, 0.0)[None] * log_g[:, None, None]), 0.0)
    kd = jnp.repeat(jnp.exp((cl - 1.0 - i)[:, None] * log_g[None, :]), RET_DK, axis=1)
    qd = jnp.repeat(jnp.exp((i + 1.0)[:, None] * log_g[None, :]), RET_DK, axis=1)
    cd = jnp.exp(cl * log_g)
    w = RET_HEADS * RET_DK
    rspec = pl.BlockSpec((cl, w), lambda bi, ci: (bi * nc + ci, 0))
    cspec = lambda shape: pl.BlockSpec(shape, lambda bi, ci: (0,) * len(shape))
    sspec = pl.BlockSpec((1, RET_HEADS, RET_DK, RET_DV), lambda bi, ci: (bi, 0, 0, 0))
    return pl.pallas_call(
        _retention_kernel,
        grid=(b, nc),
        in_specs=[pl.BlockSpec(memory_space=pltpu.SMEM), rspec, rspec, rspec, rspec,
                  pl.BlockSpec((cl, RET_DK), lambda bi, ci: (ci, 0)),
                  pl.BlockSpec((cl, RET_DK), lambda bi, ci: (ci, 0)),
                  cspec((RET_HEADS, cl, cl)), cspec((cl, w)), cspec((cl, w)), cspec((1, w)), sspec],
        out_specs=[rspec, sspec],
        out_shape=[jax.ShapeDtypeStruct((b * t, w), BF16),
                   jax.ShapeDtypeStruct((b, RET_HEADS, RET_DK, RET_DV), F32)],
        compiler_params=_cparams(("parallel", "arbitrary")),
        name="retention",
    )(cd, rq, rk, rv, rg, cosf, sinf, dmat, qd, kd, g_ret.reshape(1, w), s0)


def _t5_bucket(rel):
    nb = NUM_BUCKETS // 2
    max_exact = nb // 2
    base = jnp.where(rel > 0, nb, 0)
    n = jnp.abs(rel)
    nf = jnp.maximum(n, 1).astype(F32)
    large = max_exact + (jnp.log(nf / max_exact) / math.log(MAX_DISTANCE / max_exact) * (nb - max_exact)).astype(I32)
    large = jnp.minimum(large, nb - 1)
    return base + jnp.where(n < max_exact, n, large)


def _band_kernel(rb_ref, bucket_ref, band_ref):
    bucket = bucket_ref[...]
    for h in range(ATT_HEADS):
        acc = jnp.zeros(bucket.shape, F32)
        for j in range(NUM_BUCKETS):
            acc = jnp.where(bucket == j, rb_ref[j, h], acc)
        band_ref[h] = acc


def _bias_band(rel_bias):
    c = jnp.arange(BAND_TILES * LANES, dtype=I32)[:, None]
    t = jnp.arange(LANES, dtype=I32)[None, :]
    bucket = _t5_bucket(c - (BAND_TILES - 1) * LANES - t)
    return pl.pallas_call(
        _band_kernel,
        in_specs=[pl.BlockSpec(memory_space=pltpu.SMEM), pl.BlockSpec(memory_space=pltpu.VMEM)],
        out_specs=pl.BlockSpec(memory_space=pltpu.VMEM),
        out_shape=jax.ShapeDtypeStruct((ATT_HEADS, BAND_TILES * LANES, LANES), F32),
        name="bias_band",
    )(rel_bias, bucket)


def _order_key(s):
    bits = lax.bitcast_convert_type(s, I32)
    return bits ^ ((bits >> 31) & 0x7FFFFFFF)


def _fold8(x):
    return jnp.sum(x.reshape(LANES // SUBLANES, SUBLANES, x.shape[1]), axis=0)


def _foldmax8(x):
    return jnp.max(x.reshape(LANES // SUBLANES, SUBLANES, x.shape[1]), axis=0)


def _attention_kernel(q_ref, iq_ref, tail_ref, k_ref, vt_ref, ki_ref, band_ref, adm_ref, o_ref,
                      iqs_s, key_s, mask_s, log_s, acc_s, *, jd0, n_sel, idx_bits):
    tq = LANES
    jd = jd0 + pl.program_id(1)
    nt = jd + 1
    npair = (nt + 1) // 2
    krow = lax.broadcasted_iota(I32, (LANES, tq), 0)

    qt = jnp.transpose(q_ref[...].astype(F32)).astype(BF16)
    iqt = jnp.transpose(iq_ref[...].astype(F32)).astype(BF16)
    wt = jnp.transpose(tail_ref[...])[IDX_DIM:IDX_DIM + IDX_HEADS] * (IDX_HEADS ** -0.5)
    zpad = jnp.zeros((LANES - IDX_DIM, tq), BF16)
    for h in range(IDX_HEADS):
        iqs_s[:, h * tq:(h + 1) * tq] = jnp.concatenate([iqt[h * IDX_DIM:(h + 1) * IDX_DIM], zpad], axis=0)

    def tile_rows(j):
        return pl.ds(pl.multiple_of(j * LANES, LANES), LANES)

    def score_tile(j):
        d = jnp.dot(ki_ref[tile_rows(j), :], iqs_s[...], preferred_element_type=F32)
        acc = jnp.zeros((LANES, tq), F32)
        for h in range(IDX_HEADS):
            acc = acc + wt[h:h + 1] * jnp.maximum(d[:, h * tq:(h + 1) * tq] * (IDX_DIM ** -0.5), 0.0)
        return acc

    def score_body(jp, carry):
        for u in range(2):
            key_s[2 * jp + u] = _order_key(score_tile(2 * jp + u))
        return carry

    lax.fori_loop(0, npair, score_body, 0)
    key_s[jd] = jnp.where(adm_ref[...] > 0.0, key_s[jd], _f32_key_const(NEG_INF))

    @pl.when(nt % 2 == 1)
    def _():
        key_s[nt] = jnp.full((LANES, tq), INT_MIN, I32)

    def count(pred_tile):
        def body(jp, acc):
            return acc + _fold8(pred_tile(2 * jp)) + _fold8(pred_tile(2 * jp + 1))
        acc = lax.fori_loop(0, npair, body, jnp.zeros((SUBLANES, tq), F32))
        return jnp.sum(acc, axis=0, keepdims=True)

    kf = float(n_sel)

    def bit_body(it, thr):
        cand = thr + lax.shift_left(jnp.int32(1), 31 - it)
        cnt = count(lambda j: jnp.where(key_s[j] >= cand, 1.0, 0.0))
        return jnp.where(cnt >= kf, cand, thr)

    thr = lax.fori_loop(0, 32, bit_body, jnp.full((1, tq), INT_MIN, I32))

    need = kf - count(lambda j: jnp.where(key_s[j] > thr, 1.0, 0.0))
    n_tied = count(lambda j: jnp.where(key_s[j] == thr, 1.0, 0.0))
    surplus = jnp.where(thr > HALF_NEG_KEY, n_tied - need, 0.0)

    def tie_search():
        def tie_body(it, j0):
            cand = j0 + lax.shift_left(jnp.int32(1), idx_bits - 1 - it)
            cnt = count(lambda j: jnp.where(key_s[j] == thr, jnp.where(krow + j * LANES < cand, 1.0, 0.0), 0.0))
            return jnp.where(cnt < need, cand, j0)
        return lax.fori_loop(0, idx_bits, tie_body, jnp.zeros((1, tq), I32))

    j0 = lax.cond(jnp.max(surplus) > 0.0, tie_search, lambda: jnp.full((1, tq), INT_MAX, I32))

    def mask_body(jp, carry):
        for u in range(2):
            j = 2 * jp + u
            key = key_s[j]
            sel = jnp.where(key > thr, 1.0, jnp.where(key == thr, jnp.where(krow + j * LANES <= j0, 1.0, 0.0), 0.0))
            sel = jnp.where(key > HALF_NEG_KEY, sel, 0.0)
            mask_s[j] = jnp.where(sel > 0.0, 0.0, NEG_INF)
        return carry

    lax.fori_loop(0, npair, mask_body, 0)

    scale = ATT_HEAD_DIM ** -0.5
    heads = range(ATT_HEADS)
    qh = [qt[h * ATT_HEAD_DIM:(h + 1) * ATT_HEAD_DIM] for h in heads]

    def log_body(jp, mx):
        mx = list(mx)
        for u in range(2):
            j = 2 * jp + u
            band_rows = tile_rows(jnp.clip(j - jd + BAND_TILES - 1, 0, BAND_TILES - 1))
            msk = mask_s[j]
            for h in heads:
                kh = k_ref[tile_rows(j), h * ATT_HEAD_DIM:(h + 1) * ATT_HEAD_DIM]
                lg = jnp.dot(kh, qh[h], preferred_element_type=F32) * scale + band_ref[h, band_rows, :] + msk
                log_s[h, j] = lg
                mx[h] = jnp.maximum(mx[h], _foldmax8(lg))
        return tuple(mx)

    mx = lax.fori_loop(0, npair, log_body, tuple(jnp.full((SUBLANES, tq), NEG_INF, F32) for _ in heads))
    m = [jnp.max(mx[h], axis=0, keepdims=True) for h in heads]
    acc_s[...] = jnp.zeros(acc_s.shape, F32)

    def pv_body(jp, ls):
        ls = list(ls)
        for h in heads:
            acc = acc_s[h]
            for u in range(2):
                j = 2 * jp + u
                p = jnp.exp(log_s[h, j] - m[h])
                ls[h] = ls[h] + _fold8(p)
                vth = vt_ref[j, h * ATT_HEAD_DIM:(h + 1) * ATT_HEAD_DIM, :]
                acc = acc + jnp.dot(vth, p.astype(BF16), preferred_element_type=F32)
            acc_s[h] = acc
        return tuple(ls)

    ls = lax.fori_loop(0, npair, pv_body, tuple(jnp.zeros((SUBLANES, tq), F32) for _ in heads))
    for h in heads:
        ot = acc_s[h] / jnp.sum(ls[h], axis=0, keepdims=True)
        o_ref[:, h * ATT_HEAD_DIM:(h + 1) * ATT_HEAD_DIM] = jnp.transpose(ot).astype(o_ref.dtype)


def _attention(aq, iq, tail, keys_k, keys_v, keys_i, band, b, t, past_len):
    l = past_len + t
    tq = LANES
    nqb = -(-t // tq)
    tp = nqb * tq
    assert past_len % LANES == 0 and (t % tq == 0 or nqb == 1)
    jd0 = past_len // LANES
    ntiles = jd0 + nqb
    ntp = ntiles + ntiles % 2
    lp = ntp * LANES
    n_sel = min(TOPK_MAX, l // 4)
    if tp != t:
        padq = lambda a: jnp.pad(a.reshape(b, t, -1), ((0, 0), (0, tp - t), (0, 0))).reshape(b * tp, -1)
        aq, iq, tail = padq(aq), padq(iq), padq(tail)
    w = ATT_HEADS * ATT_HEAD_DIM
    pad = ((0, 0), (0, lp - l), (0, 0))
    kk = jnp.pad(keys_k, pad)
    vt = jnp.pad(keys_v, pad).reshape(b, ntp, LANES, w).transpose(0, 1, 3, 2)
    ki2 = jnp.pad(keys_i, ((0, 0), (0, lp - l), (0, LANES - IDX_DIM)))
    cc = jnp.arange(LANES)[:, None]
    tt = jnp.arange(tq)[None, :]
    adm = ((cc // CHUNK <= tt // CHUNK) & (cc < l - (ntiles - 1) * LANES)).astype(F32)
    qspec = lambda width: pl.BlockSpec((tq, width), lambda bi, qi: (bi * nqb + qi, 0))
    kspec = lambda width: pl.BlockSpec((None, lp, width), lambda bi, qi: (bi, 0, 0))
    kern = functools.partial(_attention_kernel, jd0=jd0, n_sel=n_sel, idx_bits=max((lp - 1).bit_length(), 1))
    out = pl.pallas_call(
        kern,
        grid=(b, nqb),
        in_specs=[qspec(w), qspec(IDX_HEADS * IDX_DIM), qspec(LANES),
                  kspec(w), pl.BlockSpec((None, ntp, w, LANES), lambda bi, qi: (bi, 0, 0, 0)), kspec(LANES),
                  pl.BlockSpec((ATT_HEADS, BAND_TILES * LANES, tq), lambda bi, qi: (0, 0, 0)),
                  pl.BlockSpec((LANES, tq), lambda bi, qi: (0, 0))],
        out_specs=qspec(w),
        out_shape=jax.ShapeDtypeStruct((b * tp, w), BF16),
        scratch_shapes=[pltpu.VMEM((LANES, IDX_HEADS * tq), BF16),
                        pltpu.VMEM((ntp, LANES, tq), I32), pltpu.VMEM((ntp, LANES, tq), F32),
                        pltpu.VMEM((ATT_HEADS, ntp, LANES, tq), F32), pltpu.VMEM((ATT_HEADS, ATT_HEAD_DIM, tq), F32)],
        compiler_params=_cparams(("parallel", "arbitrary")),
        name="attention",
    )(aq, iq, tail, kk, vt, ki2, band, adm)
    if tp != t:
        out = out.reshape(b, tp, w)[:, :t].reshape(b * t, w)
    return out


def _store_tile_major(ref, x):
    rows, width = x.shape
    nk = width // LANES
    for c in range(nk):
        ref[pl.ds(c, rows, stride=nk), :] = x[:, c * LANES:(c + 1) * LANES]


def _load_tile_major(ref, rows, nk, first=0, stride=None):
    stride = nk if stride is None else stride
    return jnp.concatenate([ref[pl.ds(first + c, rows, stride=stride), :] for c in range(nk)], axis=1)


def _outproj_kernel(x_ref, oret_ref, oatt_ref, wo_ref, g_ref, wr_ref, br_ref, x1_ref, h2_ref, e_ref, gt_ref):
    nr = oret_ref.shape[1]
    mix = jnp.dot(oret_ref[...], wo_ref[:nr, :], preferred_element_type=F32)
    mix = mix + jnp.dot(oatt_ref[...], wo_ref[nr:, :], preferred_element_type=F32)
    x1 = x_ref[...] + mix
    x1_ref[...] = x1
    h2 = (x1 * lax.rsqrt(jnp.mean(jnp.square(x1), -1, keepdims=True) + EPS)) * g_ref[...]
    _store_tile_major(h2_ref, h2)
    lg = jnp.dot(h2.astype(BF16), wr_ref[...], preferred_element_type=F32) + br_ref[...]
    tm = lg.shape[0]
    gl = lg[:, :N_GROUPS]
    gmax = jnp.max(gl, -1, keepdims=True)
    p_top = 1.0 / jnp.sum(jnp.exp(gl - gmax), -1, keepdims=True)
    gi = lax.broadcasted_iota(I32, (tm, N_GROUPS), 1).astype(F32)
    g_top = jnp.min(jnp.where(gl == gmax, gi, float(N_GROUPS)), -1, keepdims=True)
    el = jnp.zeros((tm, EXPERTS_PER_GROUP), F32)
    for g in range(N_GROUPS):
        lo = N_GROUPS + g * EXPERTS_PER_GROUP
        el = jnp.where(g_top == float(g), lg[:, lo:lo + EXPERTS_PER_GROUP], el)
    ei = lax.broadcasted_iota(I32, (tm, EXPERTS_PER_GROUP), 1).astype(F32)
    v1 = jnp.max(el, -1, keepdims=True)
    i1 = jnp.min(jnp.where(el == v1, ei, float(EXPERTS_PER_GROUP)), -1, keepdims=True)
    el2 = jnp.where(ei == i1, -jnp.inf, el)
    v2 = jnp.max(el2, -1, keepdims=True)
    i2 = jnp.min(jnp.where(el2 == v2, ei, float(EXPERTS_PER_GROUP)), -1, keepdims=True)
    e2 = jnp.exp(v2 - v1)
    den = 1.0 + e2
    two = lax.broadcasted_iota(I32, (tm, TOP_E), 1)
    e_ref[...] = (g_top * EXPERTS_PER_GROUP + jnp.where(two == 0, i1, i2)).astype(I32)
    gt_ref[...] = jnp.where(two == 0, 1.0 / den, e2 / den) * p_top


def _outproj_router(x2, o_ret, o_att, w_out, g_ffn, w_group, b_group, w_er, b_er):
    n, d = x2.shape
    tm = _row_tile(n)
    nk = d // LANES
    nrt = N_GROUPS + N_EXPERTS
    wr = jnp.pad(jnp.concatenate([w_group, w_er.reshape(d, N_EXPERTS)], 1), ((0, 0), (0, LANES - nrt))).astype(BF16)
    br = jnp.pad(jnp.concatenate([b_group, b_er.reshape(N_EXPERTS)]), (0, LANES - nrt)).reshape(1, LANES)
    rspec = lambda w: pl.BlockSpec((tm, w), lambda i: (i, 0))
    cspec = lambda r, c: pl.BlockSpec((r, c), lambda i: (0, 0))
    mw = w_out.shape[0]
    return pl.pallas_call(
        _outproj_kernel,
        grid=(n // tm,),
        in_specs=[rspec(d), rspec(o_ret.shape[1]), rspec(o_att.shape[1]), cspec(mw, d), cspec(1, d),
                  cspec(d, LANES), cspec(1, LANES)],
        out_specs=[rspec(d), pl.BlockSpec((tm * nk, LANES), lambda i: (i, 0)), rspec(TOP_E), rspec(TOP_E)],
        out_shape=[jax.ShapeDtypeStruct((n, d), F32), jax.ShapeDtypeStruct((n * nk, LANES), F32),
                   jax.ShapeDtypeStruct((n, TOP_E), I32), jax.ShapeDtypeStruct((n, TOP_E), F32)],
        compiler_params=_cparams(("parallel",)),
        name="outproj_router",
    )(x2, o_ret, o_att, w_out.astype(BF16), g_ffn.reshape(1, d), wr, br)


def _expert_kernel(blk_e_ref, tok0_ref, tokn_ref, slot_ref, h_ref, wg_ref, wu_ref, wd_ref, o_ref,
                   xbuf, obuf, gsem, ssem):
    i = pl.program_id(0)
    nb = pl.num_programs(0)
    rb, nk = xbuf.shape[1:3]
    cur = i % 2
    nxt = 1 - cur

    def start_gather(tok_ref, buf):
        def body(r, carry):
            pltpu.make_async_copy(h_ref.at[tok_ref[0, 0, r]], xbuf.at[buf, r], gsem.at[buf]).start()
            return carry
        lax.fori_loop(0, rb, body, 0)

    def wait_gather(buf):
        pltpu.make_async_copy(h_ref.at[pl.ds(0, rb)], xbuf.at[buf], gsem.at[buf]).wait()

    def wait_scatter(buf):
        pltpu.make_async_copy(obuf.at[buf], o_ref.at[pl.ds(0, rb)], ssem.at[buf]).wait()

    @pl.when(i == 0)
    def _():
        start_gather(tok0_ref, 0)

    @pl.when(i + 1 < nb)
    def _():
        start_gather(tokn_ref, nxt)

    wait_gather(cur)
    xb = jnp.concatenate([xbuf[cur, :, c, :] for c in range(nk)], axis=1).astype(BF16)
    hid = jax.nn.silu(jnp.dot(xb, wg_ref[...], preferred_element_type=F32))
    hid = hid * jnp.dot(xb, wu_ref[...], preferred_element_type=F32)
    out = jnp.dot(hid.astype(BF16), wd_ref[...], preferred_element_type=F32)

    @pl.when(i >= 2)
    def _():
        wait_scatter(cur)

    for c in range(nk):
        obuf[cur, :, c, :] = out[:, c * LANES:(c + 1) * LANES]

    def scatter(r, carry):
        pltpu.make_async_copy(obuf.at[cur, r], o_ref.at[slot_ref[0, 0, r]], ssem.at[cur]).start()
        return carry

    lax.fori_loop(0, rb, scatter, 0)

    @pl.when(i == nb - 1)
    def _():
        wait_scatter(cur)

        @pl.when(nb >= 2)
        def _():
            wait_scatter(nxt)


def _experts(h2, src_tok, out_slot, blk_e, w_gate, w_up, w_down, rb):
    d, de = w_gate.shape[1:]
    nk = d // LANES
    rows = src_tok.shape[0]
    nb = rows // rb
    idx3 = lambda a: a.reshape(nb, 1, rb)
    ispec = lambda f: pl.BlockSpec((1, 1, rb), f, memory_space=pltpu.SMEM)
    wspec = lambda r, c: pl.BlockSpec((None, r, c), lambda i, be: (be[i], 0, 0))
    return pl.pallas_call(
        _expert_kernel,
        grid_spec=pltpu.PrefetchScalarGridSpec(
            num_scalar_prefetch=1,
            grid=(nb,),
            in_specs=[ispec(lambda i, be: (0, 0, 0)), ispec(lambda i, be: (jnp.minimum(i + 1, nb - 1), 0, 0)),
                      ispec(lambda i, be: (i, 0, 0)), pl.BlockSpec(memory_space=pl.ANY),
                      wspec(d, de), wspec(d, de), wspec(de, d)],
            out_specs=pl.BlockSpec(memory_space=pl.ANY),
            scratch_shapes=[pltpu.VMEM((2, rb, nk, LANES), F32), pltpu.VMEM((2, rb, nk, LANES), F32),
                            pltpu.SemaphoreType.DMA((2,)), pltpu.SemaphoreType.DMA((2,))],
        ),
        out_shape=jax.ShapeDtypeStruct((rows, nk, LANES), F32),
        compiler_params=_cparams(("arbitrary",)),
        name="experts",
    )(blk_e, idx3(src_tok), idx3(src_tok), idx3(out_slot), h2,
      w_gate.astype(BF16), w_up.astype(BF16), w_down.astype(BF16))


def _route_plan(expert, rb):
    n = expert.shape[0]
    a = n * TOP_E
    flat_e = expert.reshape(a)
    order = jnp.argsort(flat_e, stable=True).astype(I32)
    counts = jnp.bincount(flat_e, length=N_EXPERTS).astype(I32)
    padded = (counts + rb - 1) // rb * rb
    ends = jnp.cumsum(padded)
    n_blocks = (a + N_EXPERTS * (rb - 1) + rb - 1) // rb
    blk_e = jnp.minimum(jnp.searchsorted(ends, jnp.arange(n_blocks, dtype=I32) * rb, side='right'),
                        N_EXPERTS - 1).astype(I32)
    r = jnp.arange(n_blocks * rb, dtype=I32)
    e_r = blk_e[r // rb]
    p = r - (ends - padded)[e_r]
    valid = p < counts[e_r]
    aid = order[jnp.clip((jnp.cumsum(counts) - counts)[e_r] + p, 0, a - 1)]
    src_tok = jnp.where(valid, aid // TOP_E, 0)
    out_slot = jnp.where(valid, aid, a - 1 + jnp.cumsum(jnp.logical_not(valid).astype(I32)))
    return src_tok, out_slot, blk_e


def _combine_kernel(x1_ref, eo_ref, gt_ref, g_ref, y_ref):
    tm, d = x1_ref.shape
    nk = d // LANES
    gt = gt_ref[...]
    moe = _load_tile_major(eo_ref, tm, nk, 0, TOP_E * nk) * gt[:, 0:1]
    moe = moe + _load_tile_major(eo_ref, tm, nk, nk, TOP_E * nk) * gt[:, 1:2]
    x = x1_ref[...] + moe
    y_ref[...] = (x * lax.rsqrt(jnp.mean(jnp.square(x), -1, keepdims=True) + EPS)) * g_ref[...]


def _combine(x1, eo2, gate, g_final):
    n, d = x1.shape
    tm = _row_tile(n)
    nk = d // LANES
    rspec = lambda w: pl.BlockSpec((tm, w), lambda i: (i, 0))
    return pl.pallas_call(
        _combine_kernel,
        grid=(n // tm,),
        in_specs=[rspec(d), pl.BlockSpec((tm * TOP_E * nk, LANES), lambda i: (i, 0)), rspec(TOP_E),
                  pl.BlockSpec((1, d), lambda i: (0, 0))],
        out_specs=rspec(d),
        out_shape=jax.ShapeDtypeStruct((n, d), F32),
        compiler_params=_cparams(("parallel",)),
        name="combine",
    )(x1, eo2, gate, g_final.reshape(1, d))


def _layer(x, past_len, s_ret, past_k, past_v, past_ki, band, params, g_final):
    (rel_bias, g_mix, w_in, g_ret, w_out, g_ffn, w_group, b_group, w_er, b_er, w_gate, w_up, w_down) = params
    b, t, d = x.shape
    n = b * t
    nk = d // LANES
    x2 = x.reshape(n, d)
    rq, rk, rv, rg, aq, ak, av, akb, avb, iq, ik, tail = _inproj(x2, g_mix, w_in)
    o_ret, s_new = _retention(rq, rk, rv, rg, s_ret, g_ret, b, t, past_len)
    aw = ATT_HEADS * ATT_HEAD_DIM
    keys_k = jnp.concatenate([past_k.reshape(b, past_len, aw).astype(BF16), akb.reshape(b, t, aw)], 1)
    keys_v = jnp.concatenate([past_v.reshape(b, past_len, aw).astype(BF16), avb.reshape(b, t, aw)], 1)
    keys_i = jnp.concatenate([past_ki.astype(BF16), ik.reshape(b, t, IDX_DIM).astype(BF16)], 1)
    o_att = _attention(aq, iq, tail, keys_k, keys_v, keys_i, band, b, t, past_len)
    x1, h2, expert, gate = _outproj_router(x2, o_ret, o_att, w_out, g_ffn, w_group, b_group, w_er, b_er)
    rb = 512 if n >= 8192 else 128
    src_tok, out_slot, blk_e = _route_plan(expert, rb)
    eo = _experts(h2.reshape(n, nk, LANES), src_tok, out_slot, blk_e, w_gate, w_up, w_down, rb)
    y = _combine(x1, eo.reshape(-1, LANES), gate, g_final)
    return (y.reshape(b, t, d), ak.reshape(1, b, t, ATT_HEADS, ATT_HEAD_DIM),
            av.reshape(1, b, t, ATT_HEADS, ATT_HEAD_DIM), ik.reshape(1, b, t, IDX_DIM), s_new[None])


def kernel(x_prompt, x_sample, cache_attn_k, cache_attn_v, cache_idx_k, state_ret, rel_bias, g_mix, w_in, g_ret, w_out, g_ffn, w_group, b_group, w_expert_router, b_expert_router, w_gate, w_up, w_down, g_final):
    assert g_mix.shape[0] == 1, "single-layer model"
    params = (rel_bias, g_mix[0], w_in[0], g_ret[0], w_out[0], g_ffn[0], w_group[0], b_group[0],
              w_expert_router[0], b_expert_router[0], w_gate[0], w_up[0], w_down[0])
    band = _bias_band(rel_bias)
    nb = x_prompt.shape[0]
    past_len = cache_attn_k.shape[2]
    dt = x_prompt.dtype
    empty_kv = jnp.zeros((nb, 0, ATT_HEADS, ATT_HEAD_DIM), dt)
    empty_ki = jnp.zeros((nb, 0, IDX_DIM), dt)
    s0 = jnp.zeros((nb, RET_HEADS, RET_DK, RET_DV), dt)
    yp, kp, vp, kip, sp = _layer(x_prompt, 0, s0, empty_kv, empty_kv, empty_ki, band, params, g_final)
    ys, ks, vs, kis, ss = _layer(x_sample, past_len, state_ret[0], cache_attn_k[0], cache_attn_v[0],
                                 cache_idx_k[0], band, params, g_final)
    return (yp, ys, kp, vp, kip, sp, ks, vs, kis, ss)
```

```python
import functools
import math

import jax
import jax.numpy as jnp
import numpy as np
from jax import lax
from jax.experimental import pallas as pl
from jax.experimental.pallas import tpu as pltpu

F32 = jnp.float32
BF16 = jnp.bfloat16
I32 = jnp.int32

CHUNK = 64
RET_HEADS = 4
RET_DK = 128
RET_DV = 128
ROPE_BASE = 10000.0
ATT_HEADS = 4
ATT_HEAD_DIM = 128
IDX_HEADS = 8
IDX_DIM = 64
TOPK_MAX = 256
NUM_BUCKETS = 32
MAX_DISTANCE = 128
N_GROUPS = 4
EXPERTS_PER_GROUP = 8
N_EXPERTS = N_GROUPS * EXPERTS_PER_GROUP
TOP_E = 2
EPS = 1e-6
NEG_INF = -1e30

LANES = 128
SUBLANES = 8
VMEM_LIMIT = 56 * 1024 * 1024
RET_CHUNK = 256
DMA_UNROLL = 8
INT_MIN = -(2 ** 31)
INT_MAX = 2 ** 31 - 1
BAND_TILES = 3
assert (BAND_TILES - 2) * LANES + 1 >= MAX_DISTANCE
assert math.log(IDX_DIM, 4).is_integer()


def _f32_key_const(v):
    b = int(np.array(v, np.float32).view(np.int32))
    return b ^ ((b >> 31) & 0x7FFFFFFF)


HALF_NEG_KEY = _f32_key_const(0.5 * NEG_INF)


def _cparams(sem):
    return pltpu.CompilerParams(dimension_semantics=sem, vmem_limit_bytes=VMEM_LIMIT)


def _row_tile(n):
    return 256 if n % 256 == 0 else n


def _inproj_kernel(x_ref, g_ref, wm_ref, wt_ref, rq, rk, rv, rg, aq, ak, av, akb, avb, iq, ik, tail):
    x = x_ref[...]
    h = (x * lax.rsqrt(jnp.mean(jnp.square(x), -1, keepdims=True) + EPS)) * g_ref[...]
    hb = h.astype(BF16)

    def proj(i):
        return jnp.dot(hb, wm_ref[:, i * 512:(i + 1) * 512], preferred_element_type=F32)

    rq[...] = proj(0)
    rk[...] = proj(1)
    rv[...] = proj(2).astype(BF16)
    rg[...] = proj(3)
    aq[...] = proj(4).astype(BF16)
    k = proj(5)
    ak[...] = k
    akb[...] = k.astype(BF16)
    v = proj(6)
    av[...] = v
    avb[...] = v.astype(BF16)
    iq[...] = proj(7).astype(BF16)
    t = jnp.dot(hb, wt_ref[...], preferred_element_type=F32)
    ik[...] = t[:, :IDX_DIM]
    tail[...] = t


def _inproj(x2, g_mix, w_in):
    n, d = x2.shape
    tm = _row_tile(n)
    wm = w_in[:, :4096].astype(BF16)
    wt = jnp.pad(w_in[:, 4096:], ((0, 0), (0, LANES - (w_in.shape[1] - 4096)))).astype(BF16)
    row = lambda w, dt: jax.ShapeDtypeStruct((n, w), dt)
    rspec = lambda w: pl.BlockSpec((tm, w), lambda i: (i, 0))
    widths = [(512, F32), (512, F32), (512, BF16), (512, F32), (512, BF16), (512, F32), (512, F32),
              (512, BF16), (512, BF16), (512, BF16), (IDX_DIM, F32), (LANES, F32)]
    return pl.pallas_call(
        _inproj_kernel,
        grid=(n // tm,),
        in_specs=[rspec(d), pl.BlockSpec((1, d), lambda i: (0, 0)),
                  pl.BlockSpec((d, 4096), lambda i: (0, 0)), pl.BlockSpec((d, LANES), lambda i: (0, 0))],
        out_specs=[rspec(w) for w, _ in widths],
        out_shape=[row(w, dt) for w, dt in widths],
        compiler_params=_cparams(("parallel",)),
        name="inproj",
    )(x2, g_mix.reshape(1, d), wm, wt)


def _retention_kernel(cd_ref, rq_ref, rk_ref, rv_ref, rg_ref, cos_ref, sin_ref, dmat_ref, qd_ref, kd_ref,
                      gret_ref, s0_ref, o_ref, s_ref):
    c = pl.program_id(1)

    @pl.when(c == 0)
    def _():
        s_ref[...] = s0_ref[...]

    cosf = cos_ref[...]
    sinf = sin_ref[...]
    half = RET_DK // 2

    def rot(x):
        return x * cosf + pltpu.roll(x, half, 1) * sinf

    for h in range(RET_HEADS):
        sl = slice(h * RET_DK, (h + 1) * RET_DK)
        q = rot(rq_ref[:, sl])
        k = rot(rk_ref[:, sl]) * (RET_DK ** -0.5)
        v = rv_ref[:, sl]
        s = s_ref[0, h]
        sc = lax.dot_general(q.astype(BF16), k.astype(BF16), (((1,), (1,)), ((), ())),
                             preferred_element_type=F32) * dmat_ref[h]
        o = jnp.dot(sc.astype(BF16), v, preferred_element_type=F32)
        o = o + jnp.dot((q * qd_ref[:, sl]).astype(BF16), s.astype(BF16), preferred_element_type=F32)
        kdt = jnp.transpose(k * kd_ref[:, sl]).astype(BF16)
        s_ref[0, h] = cd_ref[h] * s + jnp.dot(kdt, v, preferred_element_type=F32)
        mu = jnp.mean(o, -1, keepdims=True)
        var = jnp.mean(jnp.square(o - mu), -1, keepdims=True)
        on = (o - mu) * lax.rsqrt(var + EPS) * gret_ref[:, sl]
        o_ref[:, sl] = (jax.nn.silu(rg_ref[:, sl]) * on).astype(o_ref.dtype)


def _retention(rq, rk, rv, rg, s0, g_ret, b, t, past_len):
    cl = min(RET_CHUNK, t)
    nc = t // cl
    half = RET_DK // 2
    pos = (past_len + jnp.arange(t)).astype(F32)
    inv = ROPE_BASE ** (-jnp.arange(half, dtype=F32) / half)
    ang = pos[:, None] * inv[None, :]
    cosf = jnp.concatenate([jnp.cos(ang), jnp.cos(ang)], -1)
    sinf = jnp.concatenate([-jnp.sin(ang), jnp.sin(ang)], -1)
    log_g = jnp.log1p(-jnp.exp2(-5.0 - jnp.arange(RET_HEADS, dtype=F32)))
    i = jnp.arange(cl, dtype=F32)
    diff = i[:, None] - i[None, :]
    dmat = jnp.where(diff[None] >= 0, jnp.exp(jnp.maximum(diff, 0.0)[None] * log_g[:, None, None]), 0.0)
    kd = jnp.repeat(jnp.exp((cl - 1.0 - i)[:, None] * log_g[None, :]), RET_DK, axis=1)
    qd = jnp.repeat(jnp.exp((i + 1.0)[:, None] * log_g[None, :]), RET_DK, axis=1)
    cd = jnp.exp(cl * log_g)
    w = RET_HEADS * RET_DK
    rspec = pl.BlockSpec((cl, w), lambda bi, ci: (bi * nc + ci, 0))
    cspec = lambda shape: pl.BlockSpec(shape, lambda bi, ci: (0,) * len(shape))
    sspec = pl.BlockSpec((1, RET_HEADS, RET_DK, RET_DV), lambda bi, ci: (bi, 0, 0, 0))
    return pl.pallas_call(
        _retention_kernel,
        grid=(b, nc),
        in_specs=[pl.BlockSpec(memory_space=pltpu.SMEM), rspec, rspec, rspec, rspec,
                  pl.BlockSpec((cl, RET_DK), lambda bi, ci: (ci, 0)),
                  pl.BlockSpec((cl, RET_DK), lambda bi, ci: (ci, 0)),
                  cspec((RET_HEADS, cl, cl)), cspec((cl, w)), cspec((cl, w)), cspec((1, w)), sspec],
        out_specs=[rspec, sspec],
        out_shape=[jax.ShapeDtypeStruct((b * t, w), BF16),
                   jax.ShapeDtypeStruct((b, RET_HEADS, RET_DK, RET_DV), F32)],
        compiler_params=_cparams(("parallel", "arbitrary")),
        name="retention",
    )(cd, rq, rk, rv, rg, cosf, sinf, dmat, qd, kd, g_ret.reshape(1, w), s0)


def _t5_bucket(rel):
    nb = NUM_BUCKETS // 2
    max_exact = nb // 2
    base = jnp.where(rel > 0, nb, 0)
    n = jnp.abs(rel)
    nf = jnp.maximum(n, 1).astype(F32)
    large = max_exact + (jnp.log(nf / max_exact) / math.log(MAX_DISTANCE / max_exact) * (nb - max_exact)).astype(I32)
    large = jnp.minimum(large, nb - 1)
    return base + jnp.where(n < max_exact, n, large)


def _band_kernel(rb_ref, bucket_ref, band_ref):
    bucket = bucket_ref[...]
    for h in range(ATT_HEADS):
        acc = jnp.zeros(bucket.shape, F32)
        for j in range(NUM_BUCKETS):
            acc = jnp.where(bucket == j, rb_ref[j, h], acc)
        band_ref[h] = acc


def _bias_band(rel_bias):
    c = jnp.arange(BAND_TILES * LANES, dtype=I32)[:, None]
    t = jnp.arange(LANES, dtype=I32)[None, :]
    bucket = _t5_bucket(c - (BAND_TILES - 1) * LANES - t)
    return pl.pallas_call(
        _band_kernel,
        in_specs=[pl.BlockSpec(memory_space=pltpu.SMEM), pl.BlockSpec(memory_space=pltpu.VMEM)],
        out_specs=pl.BlockSpec(memory_space=pltpu.VMEM),
        out_shape=jax.ShapeDtypeStruct((ATT_HEADS, BAND_TILES * LANES, LANES), F32),
        name="bias_band",
    )(rel_bias, bucket)


def _order_key(s):
    bits = lax.bitcast_convert_type(s, I32)
    return bits ^ ((bits >> 31) & 0x7FFFFFFF)


def _fold8(x, op=jnp.add):
    parts = [x[i * SUBLANES:(i + 1) * SUBLANES] for i in range(x.shape[0] // SUBLANES)]
    while len(parts) > 1:
        parts = [op(parts[i], parts[i + 1]) for i in range(0, len(parts), 2)]
    return parts[0]


def _tile_loop(nt, trip, carry):
    def run(first, trips, width, carry):
        return lax.fori_loop(0, trips, lambda i, c: trip([first + i * width + u for u in range(width)], c), carry)
    carry = run(0, nt // 4, 4, carry)
    carry = run(nt // 4 * 4, (nt // 2) % 2, 2, carry)
    return run(nt // 2 * 2, nt % 2, 1, carry)


def _attention_kernel(q_ref, iq_ref, tail_ref, k_ref, vt_ref, ki_ref, band_ref, adm_ref, o_ref,
                      iqs_s, key_s, mask_s, log_s, acc_s, *, jd0, n_sel, idx_bits):
    tq = LANES
    jd = jd0 + pl.program_id(1)
    nt = jd + 1
    krow = lax.broadcasted_iota(I32, (LANES, tq), 0)

    qt = jnp.transpose(q_ref[...].astype(F32)).astype(BF16)
    iqt = jnp.transpose(iq_ref[...].astype(F32)).astype(BF16)
    wt = jnp.transpose(tail_ref[...])[IDX_DIM:IDX_DIM + IDX_HEADS] * (IDX_HEADS ** -0.5) * (IDX_DIM ** -0.5)
    zpad = jnp.zeros((LANES - IDX_DIM, tq), BF16)
    for h in range(IDX_HEADS):
        iqs_s[:, h * tq:(h + 1) * tq] = jnp.concatenate([iqt[h * IDX_DIM:(h + 1) * IDX_DIM], zpad], axis=0)

    def tile_rows(j):
        return pl.ds(pl.multiple_of(j * LANES, LANES), LANES)

    def score_tile(j):
        d = jnp.dot(ki_ref[tile_rows(j), :], iqs_s[...], preferred_element_type=F32)
        acc = jnp.zeros((LANES, tq), F32)
        for h in range(IDX_HEADS):
            acc = acc + wt[h:h + 1] * jnp.maximum(d[:, h * tq:(h + 1) * tq], 0.0)
        return acc

    def score_trip(js, carry):
        for j in js:
            key_s[j] = _order_key(score_tile(j))
        return carry

    _tile_loop(nt, score_trip, 0)
    key_s[jd] = jnp.where(adm_ref[...] > 0.0, key_s[jd], _f32_key_const(NEG_INF))

    def count(pred_tile):
        def trip(js, acc):
            for j in js:
                acc = acc + _fold8(pred_tile(j))
            return acc
        return jnp.sum(_tile_loop(nt, trip, jnp.zeros((SUBLANES, tq), F32)), axis=0, keepdims=True)

    kf = float(n_sel)

    def bit_body(it, thr):
        cand = thr + lax.shift_left(jnp.int32(1), 31 - it)
        cnt = count(lambda j: jnp.where(key_s[j] >= cand, 1.0, 0.0))
        return jnp.where(cnt >= kf, cand, thr)

    thr = lax.fori_loop(0, 32, bit_body, jnp.full((1, tq), INT_MIN, I32))

    need = kf - count(lambda j: jnp.where(key_s[j] > thr, 1.0, 0.0))
    n_tied = count(lambda j: jnp.where(key_s[j] == thr, 1.0, 0.0))
    surplus = jnp.where(thr > HALF_NEG_KEY, n_tied - need, 0.0)

    def tie_search():
        def tie_body(it, j0):
            cand = j0 + lax.shift_left(jnp.int32(1), idx_bits - 1 - it)
            cnt = count(lambda j: jnp.where(key_s[j] == thr, jnp.where(krow + j * LANES < cand, 1.0, 0.0), 0.0))
            return jnp.where(cnt < need, cand, j0)
        return lax.fori_loop(0, idx_bits, tie_body, jnp.zeros((1, tq), I32))

    j0 = lax.cond(jnp.max(surplus) > 0.0, tie_search, lambda: jnp.full((1, tq), INT_MAX, I32))

    def mask_trip(js, carry):
        for j in js:
            key = key_s[j]
            sel = jnp.where(key > thr, 1.0, jnp.where(key == thr, jnp.where(krow + j * LANES <= j0, 1.0, 0.0), 0.0))
            sel = jnp.where(key > HALF_NEG_KEY, sel, 0.0)
            mask_s[j] = jnp.where(sel > 0.0, 0.0, NEG_INF)
        return carry

    _tile_loop(nt, mask_trip, 0)

    scale = ATT_HEAD_DIM ** -0.5
    heads = range(ATT_HEADS)
    qh = [qt[h * ATT_HEAD_DIM:(h + 1) * ATT_HEAD_DIM] for h in heads]

    def log_trip(js, mx):
        mx = list(mx)
        for j in js:
            band_rows = tile_rows(jnp.clip(j - jd + BAND_TILES - 1, 0, BAND_TILES - 1))
            msk = mask_s[j]
            for h in heads:
                kh = k_ref[tile_rows(j), h * ATT_HEAD_DIM:(h + 1) * ATT_HEAD_DIM]
                lg = jnp.dot(kh, qh[h], preferred_element_type=F32) * scale + band_ref[h, band_rows, :] + msk
                log_s[h, j] = lg
                mx[h] = jnp.maximum(mx[h], _fold8(lg, jnp.maximum))
        return tuple(mx)

    mx = _tile_loop(nt, log_trip, tuple(jnp.full((SUBLANES, tq), NEG_INF, F32) for _ in heads))
    m = [jnp.max(mx[h], axis=0, keepdims=True) for h in heads]
    acc_s[...] = jnp.zeros(acc_s.shape, F32)

    def pv_trip(js, ls):
        ls = list(ls)
        for h in heads:
            acc = acc_s[h]
            for j in js:
                p = jnp.exp(log_s[h, j] - m[h])
                ls[h] = ls[h] + _fold8(p)
                vth = vt_ref[j, h * ATT_HEAD_DIM:(h + 1) * ATT_HEAD_DIM, :]
                acc = acc + jnp.dot(vth, p.astype(BF16), preferred_element_type=F32)
            acc_s[h] = acc
        return tuple(ls)

    ls = _tile_loop(nt, pv_trip, tuple(jnp.zeros((SUBLANES, tq), F32) for _ in heads))
    for h in heads:
        ot = acc_s[h] / jnp.sum(ls[h], axis=0, keepdims=True)
        o_ref[:, h * ATT_HEAD_DIM:(h + 1) * ATT_HEAD_DIM] = jnp.transpose(ot).astype(o_ref.dtype)


def _attention(aq, iq, tail, keys_k, keys_v, keys_i, band, b, t, past_len):
    l = past_len + t
    tq = LANES
    nqb = -(-t // tq)
    tp = nqb * tq
    assert past_len % LANES == 0 and (t % tq == 0 or nqb == 1)
    jd0 = past_len // LANES
    ntiles = jd0 + nqb
    ntp = ntiles
    lp = ntp * LANES
    n_sel = min(TOPK_MAX, l // 4)
    if tp != t:
        padq = lambda a: jnp.pad(a.reshape(b, t, -1), ((0, 0), (0, tp - t), (0, 0))).reshape(b * tp, -1)
        aq, iq, tail = padq(aq), padq(iq), padq(tail)
    w = ATT_HEADS * ATT_HEAD_DIM
    pad = ((0, 0), (0, lp - l), (0, 0))
    kk = jnp.pad(keys_k, pad)
    vt = jnp.pad(keys_v, pad).reshape(b, ntp, LANES, w).transpose(0, 1, 3, 2)
    ki2 = jnp.pad(keys_i, ((0, 0), (0, lp - l), (0, LANES - IDX_DIM)))
    cc = jnp.arange(LANES)[:, None]
    tt = jnp.arange(tq)[None, :]
    adm = ((cc // CHUNK <= tt // CHUNK) & (cc < l - (ntiles - 1) * LANES)).astype(F32)
    qspec = lambda width: pl.BlockSpec((tq, width), lambda bi, qi: (bi * nqb + qi, 0))
    kspec = lambda width: pl.BlockSpec((None, lp, width), lambda bi, qi: (bi, 0, 0))
    kern = functools.partial(_attention_kernel, jd0=jd0, n_sel=n_sel, idx_bits=max((lp - 1).bit_length(), 1))
    out = pl.pallas_call(
        kern,
        grid=(b, nqb),
        in_specs=[qspec(w), qspec(IDX_HEADS * IDX_DIM), qspec(LANES),
                  kspec(w), pl.BlockSpec((None, ntp, w, LANES), lambda bi, qi: (bi, 0, 0, 0)), kspec(LANES),
                  pl.BlockSpec((ATT_HEADS, BAND_TILES * LANES, tq), lambda bi, qi: (0, 0, 0)),
                  pl.BlockSpec((LANES, tq), lambda bi, qi: (0, 0))],
        out_specs=qspec(w),
        out_shape=jax.ShapeDtypeStruct((b * tp, w), BF16),
        scratch_shapes=[pltpu.VMEM((LANES, IDX_HEADS * tq), BF16),
                        pltpu.VMEM((ntp, LANES, tq), I32), pltpu.VMEM((ntp, LANES, tq), F32),
                        pltpu.VMEM((ATT_HEADS, ntp, LANES, tq), F32), pltpu.VMEM((ATT_HEADS, ATT_HEAD_DIM, tq), F32)],
        compiler_params=_cparams(("parallel", "arbitrary")),
        name="attention",
    )(aq, iq, tail, kk, vt, ki2, band, adm)
    if tp != t:
        out = out.reshape(b, tp, w)[:, :t].reshape(b * t, w)
    return out


def _store_tile_major(ref, x):
    rows, width = x.shape
    nk = width // LANES
    for c in range(nk):
        ref[pl.ds(c, rows, stride=nk), :] = x[:, c * LANES:(c + 1) * LANES]


def _load_tile_major(ref, rows, nk, first=0, stride=None):
    stride = nk if stride is None else stride
    return jnp.concatenate([ref[pl.ds(first + c, rows, stride=stride), :] for c in range(nk)], axis=1)


def _outproj_kernel(x_ref, oret_ref, oatt_ref, wo_ref, g_ref, wr_ref, br_ref, x1_ref, h2_ref, e_ref, gt_ref):
    nr = oret_ref.shape[1]
    mix = jnp.dot(oret_ref[...], wo_ref[:nr, :], preferred_element_type=F32)
    mix = mix + jnp.dot(oatt_ref[...], wo_ref[nr:, :], preferred_element_type=F32)
    x1 = x_ref[...] + mix
    x1_ref[...] = x1
    h2 = (x1 * lax.rsqrt(jnp.mean(jnp.square(x1), -1, keepdims=True) + EPS)) * g_ref[...]
    _store_tile_major(h2_ref, h2)
    lg = jnp.dot(h2.astype(BF16), wr_ref[...], preferred_element_type=F32) + br_ref[...]
    tm = lg.shape[0]
    gl = lg[:, :N_GROUPS]
    gmax = jnp.max(gl, -1, keepdims=True)
    p_top = 1.0 / jnp.sum(jnp.exp(gl - gmax), -1, keepdims=True)
    gi = lax.broadcasted_iota(I32, (tm, N_GROUPS), 1).astype(F32)
    g_top = jnp.min(jnp.where(gl == gmax, gi, float(N_GROUPS)), -1, keepdims=True)
    el = jnp.zeros((tm, EXPERTS_PER_GROUP), F32)
    for g in range(N_GROUPS):
        lo = N_GROUPS + g * EXPERTS_PER_GROUP
        el = jnp.where(g_top == float(g), lg[:, lo:lo + EXPERTS_PER_GROUP], el)
    ei = lax.broadcasted_iota(I32, (tm, EXPERTS_PER_GROUP), 1).astype(F32)
    v1 = jnp.max(el, -1, keepdims=True)
    i1 = jnp.min(jnp.where(el == v1, ei, float(EXPERTS_PER_GROUP)), -1, keepdims=True)
    el2 = jnp.where(ei == i1, -jnp.inf, el)
    v2 = jnp.max(el2, -1, keepdims=True)
    i2 = jnp.min(jnp.where(el2 == v2, ei, float(EXPERTS_PER_GROUP)), -1, keepdims=True)
    e2 = jnp.exp(v2 - v1)
    den = 1.0 + e2
    two = lax.broadcasted_iota(I32, (tm, TOP_E), 1)
    e_ref[...] = (g_top * EXPERTS_PER_GROUP + jnp.where(two == 0, i1, i2)).astype(I32)
    gt_ref[...] = jnp.where(two == 0, 1.0 / den, e2 / den) * p_top


def _outproj_router(x2, o_ret, o_att, w_out, g_ffn, w_group, b_group, w_er, b_er):
    n, d = x2.shape
    tm = _row_tile(n)
    nk = d // LANES
    nrt = N_GROUPS + N_EXPERTS
    wr = jnp.pad(jnp.concatenate([w_group, w_er.reshape(d, N_EXPERTS)], 1), ((0, 0), (0, LANES - nrt))).astype(BF16)
    br = jnp.pad(jnp.concatenate([b_group, b_er.reshape(N_EXPERTS)]), (0, LANES - nrt)).reshape(1, LANES)
    rspec = lambda w: pl.BlockSpec((tm, w), lambda i: (i, 0))
    cspec = lambda r, c: pl.BlockSpec((r, c), lambda i: (0, 0))
    mw = w_out.shape[0]
    return pl.pallas_call(
        _outproj_kernel,
        grid=(n // tm,),
        in_specs=[rspec(d), rspec(o_ret.shape[1]), rspec(o_att.shape[1]), cspec(mw, d), cspec(1, d),
                  cspec(d, LANES), cspec(1, LANES)],
        out_specs=[rspec(d), pl.BlockSpec((tm * nk, LANES), lambda i: (i, 0)), rspec(TOP_E), rspec(TOP_E)],
        out_shape=[jax.ShapeDtypeStruct((n, d), F32), jax.ShapeDtypeStruct((n * nk, LANES), F32),
                   jax.ShapeDtypeStruct((n, TOP_E), I32), jax.ShapeDtypeStruct((n, TOP_E), F32)],
        compiler_params=_cparams(("parallel",)),
        name="outproj_router",
    )(x2, o_ret, o_att, w_out.astype(BF16), g_ffn.reshape(1, d), wr, br)


def _expert_kernel(blk_e_ref, tok0_ref, tokn_ref, slot_ref, h_ref, wg_ref, wu_ref, wd_ref, o_ref,
                   xbuf, obuf, gsem, ssem):
    i = pl.program_id(0)
    nb = pl.num_programs(0)
    rb, nk = xbuf.shape[1:3]
    cur = i % 2
    nxt = 1 - cur

    def for_rows(start_row):
        def body(g, carry):
            for u in range(DMA_UNROLL):
                start_row(g * DMA_UNROLL + u, u % 2)
            return carry
        lax.fori_loop(0, rb // DMA_UNROLL, body, 0)

    def start_gather(tok_ref, buf):
        for_rows(lambda r, pri: pltpu.make_async_copy(
            h_ref.at[tok_ref[0, 0, r]], xbuf.at[buf, r], gsem.at[buf]).start(priority=pri))

    def wait_gather(buf):
        pltpu.make_async_copy(h_ref.at[pl.ds(0, rb)], xbuf.at[buf], gsem.at[buf]).wait()

    def wait_scatter(buf):
        pltpu.make_async_copy(obuf.at[buf], o_ref.at[pl.ds(0, rb)], ssem.at[buf]).wait()

    @pl.when(i == 0)
    def _():
        start_gather(tok0_ref, 0)

    @pl.when(i + 1 < nb)
    def _():
        start_gather(tokn_ref, nxt)

    wait_gather(cur)
    xb = jnp.concatenate([xbuf[cur, :, c, :] for c in range(nk)], axis=1).astype(BF16)
    hid = jax.nn.silu(jnp.dot(xb, wg_ref[...], preferred_element_type=F32))
    hid = hid * jnp.dot(xb, wu_ref[...], preferred_element_type=F32)
    out = jnp.dot(hid.astype(BF16), wd_ref[...], preferred_element_type=F32)

    @pl.when(i >= 2)
    def _():
        wait_scatter(cur)

    for c in range(nk):
        obuf[cur, :, c, :] = out[:, c * LANES:(c + 1) * LANES]

    for_rows(lambda r, pri: pltpu.make_async_copy(
        obuf.at[cur, r], o_ref.at[slot_ref[0, 0, r]], ssem.at[cur]).start(priority=pri))

    @pl.when(i == nb - 1)
    def _():
        wait_scatter(cur)

        @pl.when(nb >= 2)
        def _():
            wait_scatter(nxt)


def _experts(h2, src_tok, out_slot, blk_e, w_gate, w_up, w_down, rb):
    d, de = w_gate.shape[1:]
    nk = d // LANES
    rows = src_tok.shape[0]
    nb = rows // rb
    idx3 = lambda a: a.reshape(nb, 1, rb)
    ispec = lambda f: pl.BlockSpec((1, 1, rb), f, memory_space=pltpu.SMEM)
    wspec = lambda r, c: pl.BlockSpec((None, r, c), lambda i, be: (be[i], 0, 0))
    return pl.pallas_call(
        _expert_kernel,
        grid_spec=pltpu.PrefetchScalarGridSpec(
            num_scalar_prefetch=1,
            grid=(nb,),
            in_specs=[ispec(lambda i, be: (0, 0, 0)), ispec(lambda i, be: (jnp.minimum(i + 1, nb - 1), 0, 0)),
                      ispec(lambda i, be: (i, 0, 0)), pl.BlockSpec(memory_space=pl.ANY),
                      wspec(d, de), wspec(d, de), wspec(de, d)],
            out_specs=pl.BlockSpec(memory_space=pl.ANY),
            scratch_shapes=[pltpu.VMEM((2, rb, nk, LANES), F32), pltpu.VMEM((2, rb, nk, LANES), F32),
                            pltpu.SemaphoreType.DMA((2,)), pltpu.SemaphoreType.DMA((2,))],
        ),
        out_shape=jax.ShapeDtypeStruct((rows, nk, LANES), F32),
        compiler_params=_cparams(("arbitrary",)),
        name="experts",
    )(blk_e, idx3(src_tok), idx3(src_tok), idx3(out_slot), h2,
      w_gate.astype(BF16), w_up.astype(BF16), w_down.astype(BF16))


def _route_plan(expert, rb):
    n = expert.shape[0]
    a = n * TOP_E
    flat_e = expert.reshape(a)
    counts = jnp.sum((flat_e[:, None] == jnp.arange(N_EXPERTS, dtype=I32)[None, :]).astype(I32), axis=0)
    padded = (counts + rb - 1) // rb * rb
    ends = jnp.cumsum(padded)
    n_blocks = (a + N_EXPERTS * (rb - 1) + rb - 1) // rb
    rows = n_blocks * rb
    q = jnp.arange(rb - 1, dtype=I32)[None, :]
    e = jnp.arange(N_EXPERTS, dtype=I32)[:, None]
    pad_key = jnp.where(q < (padded - counts)[:, None], 2 * e + 1, 2 * N_EXPERTS).reshape(-1)
    spare = jnp.full((rows - a - N_EXPERTS * (rb - 1),), 2 * N_EXPERTS, I32)
    perm = jnp.argsort(jnp.concatenate([2 * flat_e, pad_key, spare]), stable=True).astype(I32)
    src_tok = jnp.where(perm < a, perm // TOP_E, 0)
    out_slot = perm
    blk_e = jnp.minimum(jnp.sum((ends[None, :] <= jnp.arange(n_blocks, dtype=I32)[:, None] * rb).astype(I32), axis=1),
                        N_EXPERTS - 1)
    return src_tok, out_slot, blk_e


def _combine_kernel(x1_ref, eo_ref, gt_ref, g_ref, y_ref):
    tm, d = x1_ref.shape
    nk = d // LANES
    gt = gt_ref[...]
    moe = _load_tile_major(eo_ref, tm, nk, 0, TOP_E * nk) * gt[:, 0:1]
    moe = moe + _load_tile_major(eo_ref, tm, nk, nk, TOP_E * nk) * gt[:, 1:2]
    x = x1_ref[...] + moe
    y_ref[...] = (x * lax.rsqrt(jnp.mean(jnp.square(x), -1, keepdims=True) + EPS)) * g_ref[...]


def _combine(x1, eo2, gate, g_final):
    n, d = x1.shape
    tm = _row_tile(n)
    nk = d // LANES
    rspec = lambda w: pl.BlockSpec((tm, w), lambda i: (i, 0))
    return pl.pallas_call(
        _combine_kernel,
        grid=(n // tm,),
        in_specs=[rspec(d), pl.BlockSpec((tm * TOP_E * nk, LANES), lambda i: (i, 0)), rspec(TOP_E),
                  pl.BlockSpec((1, d), lambda i: (0, 0))],
        out_specs=rspec(d),
        out_shape=jax.ShapeDtypeStruct((n, d), F32),
        compiler_params=_cparams(("parallel",)),
        name="combine",
    )(x1, eo2, gate, g_final.reshape(1, d))


def _layer(x, past_len, s_ret, past_k, past_v, past_ki, band, params, g_final):
    (rel_bias, g_mix, w_in, g_ret, w_out, g_ffn, w_group, b_group, w_er, b_er, w_gate, w_up, w_down) = params
    b, t, d = x.shape
    n = b * t
    nk = d // LANES
    x2 = x.reshape(n, d)
    rq, rk, rv, rg, aq, ak, av, akb, avb, iq, ik, tail = _inproj(x2, g_mix, w_in)
    o_ret, s_new = _retention(rq, rk, rv, rg, s_ret, g_ret, b, t, past_len)
    aw = ATT_HEADS * ATT_HEAD_DIM
    keys_k = jnp.concatenate([past_k.reshape(b, past_len, aw).astype(BF16), akb.reshape(b, t, aw)], 1)
    keys_v = jnp.concatenate([past_v.reshape(b, past_len, aw).astype(BF16), avb.reshape(b, t, aw)], 1)
    keys_i = jnp.concatenate([past_ki.astype(BF16), ik.reshape(b, t, IDX_DIM).astype(BF16)], 1)
    o_att = _attention(aq, iq, tail, keys_k, keys_v, keys_i, band, b, t, past_len)
    x1, h2, expert, gate = _outproj_router(x2, o_ret, o_att, w_out, g_ffn, w_group, b_group, w_er, b_er)
    rb = 512 if n >= 8192 else 128
    src_tok, out_slot, blk_e = _route_plan(expert, rb)
    eo = _experts(h2.reshape(n, nk, LANES), src_tok, out_slot, blk_e, w_gate, w_up, w_down, rb)
    y = _combine(x1, eo.reshape(-1, LANES), gate, g_final)
    return (y.reshape(b, t, d), ak.reshape(1, b, t, ATT_HEADS, ATT_HEAD_DIM),
            av.reshape(1, b, t, ATT_HEADS, ATT_HEAD_DIM), ik.reshape(1, b, t, IDX_DIM), s_new[None])


def kernel(x_prompt, x_sample, cache_attn_k, cache_attn_v, cache_idx_k, state_ret, rel_bias, g_mix, w_in, g_ret, w_out, g_ffn, w_group, b_group, w_expert_router, b_expert_router, w_gate, w_up, w_down, g_final):
    assert g_mix.shape[0] == 1, "single-layer model"
    params = (rel_bias, g_mix[0], w_in[0], g_ret[0], w_out[0], g_ffn[0], w_group[0], b_group[0],
              w_expert_router[0], b_expert_router[0], w_gate[0], w_up[0], w_down[0])
    band = _bias_band(rel_bias)
    nb = x_prompt.shape[0]
    past_len = cache_attn_k.shape[2]
    dt = x_prompt.dtype
    empty_kv = jnp.zeros((nb, 0, ATT_HEADS, ATT_HEAD_DIM), dt)
    empty_ki = jnp.zeros((nb, 0, IDX_DIM), dt)
    s0 = jnp.zeros((nb, RET_HEADS, RET_DK, RET_DV), dt)
    yp, kp, vp, kip, sp = _layer(x_prompt, 0, s0, empty_kv, empty_kv, empty_ki, band, params, g_final)
    ys, ks, vs, kis, ss = _layer(x_sample, past_len, state_ret[0], cache_attn_k[0], cache_attn_v[0],
                                 cache_idx_k[0], band, params, g_final)
    return (yp, ys, kp, vp, kip, sp, ks, vs, kis, ss)
```

```python
import functools
import math

import jax
import jax.numpy as jnp
import numpy as np
from jax import lax
from jax.experimental import pallas as pl
from jax.experimental.pallas import tpu as pltpu

F32 = jnp.float32
BF16 = jnp.bfloat16
I32 = jnp.int32

CHUNK = 64
RET_HEADS = 4
RET_DK = 128
RET_DV = 128
ROPE_BASE = 10000.0
ATT_HEADS = 4
ATT_HEAD_DIM = 128
IDX_HEADS = 8
IDX_DIM = 64
TOPK_MAX = 256
NUM_BUCKETS = 32
MAX_DISTANCE = 128
N_GROUPS = 4
EXPERTS_PER_GROUP = 8
N_EXPERTS = N_GROUPS * EXPERTS_PER_GROUP
TOP_E = 2
EPS = 1e-6
NEG_INF = -1e30

LANES = 128
SUBLANES = 8
VMEM_LIMIT = 56 * 1024 * 1024
RET_CHUNK = 256
DMA_UNROLL = 8
INT_MIN = -(2 ** 31)
INT_MAX = 2 ** 31 - 1
BAND_TILES = 3
assert (BAND_TILES - 2) * LANES + 1 >= MAX_DISTANCE
assert math.log(IDX_DIM, 4).is_integer()


def _f32_key_const(v):
    b = int(np.array(v, np.float32).view(np.int32))
    return b ^ ((b >> 31) & 0x7FFFFFFF)


HALF_NEG_KEY = _f32_key_const(0.5 * NEG_INF)


def _cparams(sem):
    return pltpu.CompilerParams(dimension_semantics=sem, vmem_limit_bytes=VMEM_LIMIT)


def _row_tile(n):
    return 256 if n % 256 == 0 else n


def _inproj_kernel(x_ref, g_ref, wm_ref, wt_ref, rq, rk, rv, rg, aq, ak, av, akb, avb, iq, ik, tail):
    x = x_ref[...]
    h = (x * lax.rsqrt(jnp.mean(jnp.square(x), -1, keepdims=True) + EPS)) * g_ref[...]
    hb = h.astype(BF16)

    def proj(i):
        return jnp.dot(hb, wm_ref[:, i * 512:(i + 1) * 512], preferred_element_type=F32)

    rq[...] = proj(0)
    rk[...] = proj(1)
    rv[...] = proj(2).astype(BF16)
    rg[...] = proj(3)
    aq[...] = proj(4).astype(BF16)
    k = proj(5)
    _store_tile_major(ak, k)
    akb[...] = k.astype(BF16)
    v = proj(6)
    _store_tile_major(av, v)
    avb[...] = v.astype(BF16)
    iq[...] = proj(7).astype(BF16)
    t = jnp.dot(hb, wt_ref[...], preferred_element_type=F32)
    ik[...] = t[:, :IDX_DIM]
    tail[...] = t


def _inproj(x2, g_mix, w_in):
    n, d = x2.shape
    tm = _row_tile(n)
    wm = w_in[:, :4096].astype(BF16)
    wt = jnp.pad(w_in[:, 4096:], ((0, 0), (0, LANES - (w_in.shape[1] - 4096)))).astype(BF16)
    rspec = lambda w, lines=1: pl.BlockSpec((tm * lines, w), lambda i: (i, 0))
    outs = [(512, F32, 1), (512, F32, 1), (512, BF16, 1), (512, F32, 1), (512, BF16, 1),
            (LANES, F32, ATT_HEADS), (LANES, F32, ATT_HEADS),
            (512, BF16, 1), (512, BF16, 1), (512, BF16, 1), (IDX_DIM, F32, 1), (LANES, F32, 1)]
    return pl.pallas_call(
        _inproj_kernel,
        grid=(n // tm,),
        in_specs=[rspec(d), pl.BlockSpec((1, d), lambda i: (0, 0)),
                  pl.BlockSpec((d, 4096), lambda i: (0, 0)), pl.BlockSpec((d, LANES), lambda i: (0, 0))],
        out_specs=[rspec(w, ln) for w, _, ln in outs],
        out_shape=[jax.ShapeDtypeStruct((n * ln, w), dt) for w, dt, ln in outs],
        compiler_params=_cparams(("parallel",)),
        name="inproj",
    )(x2, g_mix.reshape(1, d), wm, wt)


def _retention_kernel(cd_ref, rq_ref, rk_ref, rv_ref, rg_ref, cos_ref, sin_ref, dmat_ref, qd_ref, kd_ref,
                      gret_ref, s0_ref, o_ref, s_ref):
    c = pl.program_id(1)

    @pl.when(c == 0)
    def _():
        s_ref[...] = s0_ref[...]

    cosf = cos_ref[...]
    sinf = sin_ref[...]
    half = RET_DK // 2

    def rot(x):
        return x * cosf + pltpu.roll(x, half, 1) * sinf

    for h in range(RET_HEADS):
        sl = slice(h * RET_DK, (h + 1) * RET_DK)
        q = rot(rq_ref[:, sl])
        k = rot(rk_ref[:, sl]) * (RET_DK ** -0.5)
        v = rv_ref[:, sl]
        s = s_ref[0, h]
        sc = lax.dot_general(q.astype(BF16), k.astype(BF16), (((1,), (1,)), ((), ())),
                             preferred_element_type=F32) * dmat_ref[h]
        o = jnp.dot(sc.astype(BF16), v, preferred_element_type=F32)
        o = o + jnp.dot((q * qd_ref[:, sl]).astype(BF16), s.astype(BF16), preferred_element_type=F32)
        kdt = jnp.transpose(k * kd_ref[:, sl]).astype(BF16)
        s_ref[0, h] = cd_ref[h] * s + jnp.dot(kdt, v, preferred_element_type=F32)
        mu = jnp.mean(o, -1, keepdims=True)
        var = jnp.mean(jnp.square(o - mu), -1, keepdims=True)
        on = (o - mu) * lax.rsqrt(var + EPS) * gret_ref[:, sl]
        o_ref[:, sl] = (jax.nn.silu(rg_ref[:, sl]) * on).astype(o_ref.dtype)


def _retention(rq, rk, rv, rg, s0, g_ret, b, t, past_len):
    cl = min(RET_CHUNK, t)
    nc = t // cl
    half = RET_DK // 2
    pos = (past_len + jnp.arange(t)).astype(F32)
    inv = ROPE_BASE ** (-jnp.arange(half, dtype=F32) / half)
    ang = pos[:, None] * inv[None, :]
    cosf = jnp.concatenate([jnp.cos(ang), jnp.cos(ang)], -1)
    sinf = jnp.concatenate([-jnp.sin(ang), jnp.sin(ang)], -1)
    log_g = jnp.log1p(-jnp.exp2(-5.0 - jnp.arange(RET_HEADS, dtype=F32)))
    i = jnp.arange(cl, dtype=F32)
    diff = i[:, None] - i[None, :]
    dmat = jnp.where(diff[None] >= 0, jnp.exp(jnp.maximum(diff, 0.0)[None] * log_g[:, None, None]), 0.0)
    kd = jnp.repeat(jnp.exp((cl - 1.0 - i)[:, None] * log_g[None, :]), RET_DK, axis=1)
    qd = jnp.repeat(jnp.exp((i + 1.0)[:, None] * log_g[None, :]), RET_DK, axis=1)
    cd = jnp.exp(cl * log_g)
    w = RET_HEADS * RET_DK
    rspec = pl.BlockSpec((cl, w), lambda bi, ci: (bi * nc + ci, 0))
    cspec = lambda shape: pl.BlockSpec(shape, lambda bi, ci: (0,) * len(shape))
    sspec = pl.BlockSpec((1, RET_HEADS, RET_DK, RET_DV), lambda bi, ci: (bi, 0, 0, 0))
    return pl.pallas_call(
        _retention_kernel,
        grid=(b, nc),
        in_specs=[pl.BlockSpec(memory_space=pltpu.SMEM), rspec, rspec, rspec, rspec,
                  pl.BlockSpec((cl, RET_DK), lambda bi, ci: (ci, 0)),
                  pl.BlockSpec((cl, RET_DK), lambda bi, ci: (ci, 0)),
                  cspec((RET_HEADS, cl, cl)), cspec((cl, w)), cspec((cl, w)), cspec((1, w)), sspec],
        out_specs=[rspec, sspec],
        out_shape=[jax.ShapeDtypeStruct((b * t, w), BF16),
                   jax.ShapeDtypeStruct((b, RET_HEADS, RET_DK, RET_DV), F32)],
        compiler_params=_cparams(("parallel", "arbitrary")),
        name="retention",
    )(cd, rq, rk, rv, rg, cosf, sinf, dmat, qd, kd, g_ret.reshape(1, w), s0)


def _t5_bucket(rel):
    nb = NUM_BUCKETS // 2
    max_exact = nb // 2
    base = jnp.where(rel > 0, nb, 0)
    n = jnp.abs(rel)
    nf = jnp.maximum(n, 1).astype(F32)
    large = max_exact + (jnp.log(nf / max_exact) / math.log(MAX_DISTANCE / max_exact) * (nb - max_exact)).astype(I32)
    large = jnp.minimum(large, nb - 1)
    return base + jnp.where(n < max_exact, n, large)


def _band_kernel(rb_ref, bucket_ref, band_ref):
    bucket = bucket_ref[...]
    for h in range(ATT_HEADS):
        acc = jnp.zeros(bucket.shape, F32)
        for j in range(NUM_BUCKETS):
            acc = jnp.where(bucket == j, rb_ref[j, h], acc)
        band_ref[h] = acc


def _bias_band(rel_bias):
    c = jnp.arange(BAND_TILES * LANES, dtype=I32)[:, None]
    t = jnp.arange(LANES, dtype=I32)[None, :]
    bucket = _t5_bucket(c - (BAND_TILES - 1) * LANES - t)
    return pl.pallas_call(
        _band_kernel,
        in_specs=[pl.BlockSpec(memory_space=pltpu.SMEM), pl.BlockSpec(memory_space=pltpu.VMEM)],
        out_specs=pl.BlockSpec(memory_space=pltpu.VMEM),
        out_shape=jax.ShapeDtypeStruct((ATT_HEADS, BAND_TILES * LANES, LANES), F32),
        name="bias_band",
    )(rel_bias, bucket)


def _order_key(s):
    bits = lax.bitcast_convert_type(s, I32)
    return bits ^ ((bits >> 31) & 0x7FFFFFFF)


def _fold8(x, op=jnp.add):
    parts = [x[i * SUBLANES:(i + 1) * SUBLANES] for i in range(x.shape[0] // SUBLANES)]
    while len(parts) > 1:
        parts = [op(parts[i], parts[i + 1]) for i in range(0, len(parts), 2)]
    return parts[0]


def _tile_loop(nt, trip, carry):
    def run(first, trips, width, carry):
        return lax.fori_loop(0, trips, lambda i, c: trip([first + i * width + u for u in range(width)], c), carry)
    carry = run(0, nt // 4, 4, carry)
    carry = run(nt // 4 * 4, (nt // 2) % 2, 2, carry)
    return run(nt // 2 * 2, nt % 2, 1, carry)


def _attention_kernel(q_ref, iq_ref, tail_ref, k_ref, vt_ref, ki_ref, band_ref, adm_ref, o_ref,
                      iqs_s, key_s, mask_s, log_s, acc_s, *, jd0, n_sel, idx_bits):
    tq = LANES
    jd = jd0 + pl.program_id(1)
    nt = jd + 1
    krow = lax.broadcasted_iota(I32, (LANES, tq), 0)

    qt = jnp.transpose(q_ref[...].astype(F32)).astype(BF16)
    iqt = jnp.transpose(iq_ref[...].astype(F32)).astype(BF16)
    wt = jnp.transpose(tail_ref[...])[IDX_DIM:IDX_DIM + IDX_HEADS] * (IDX_HEADS ** -0.5) * (IDX_DIM ** -0.5)
    zpad = jnp.zeros((LANES - IDX_DIM, tq), BF16)
    for h in range(IDX_HEADS):
        iqs_s[:, h * tq:(h + 1) * tq] = jnp.concatenate([iqt[h * IDX_DIM:(h + 1) * IDX_DIM], zpad], axis=0)

    def tile_rows(j):
        return pl.ds(pl.multiple_of(j * LANES, LANES), LANES)

    def score_tile(j):
        d = jnp.dot(ki_ref[tile_rows(j), :], iqs_s[...], preferred_element_type=F32)
        acc = jnp.zeros((LANES, tq), F32)
        for h in range(IDX_HEADS):
            acc = acc + wt[h:h + 1] * jnp.maximum(d[:, h * tq:(h + 1) * tq], 0.0)
        return acc

    def score_trip(js, carry):
        for j in js:
            key_s[j] = _order_key(score_tile(j))
        return carry

    _tile_loop(nt, score_trip, 0)
    key_s[jd] = jnp.where(adm_ref[...] > 0.0, key_s[jd], _f32_key_const(NEG_INF))

    def count(pred_tile):
        def trip(js, acc):
            for j in js:
                acc = acc + _fold8(pred_tile(j))
            return acc
        return jnp.sum(_tile_loop(nt, trip, jnp.zeros((SUBLANES, tq), F32)), axis=0, keepdims=True)

    kf = float(n_sel)

    def bit_body(it, thr):
        cand = thr + lax.shift_left(jnp.int32(1), 31 - it)
        cnt = count(lambda j: jnp.where(key_s[j] >= cand, 1.0, 0.0))
        return jnp.where(cnt >= kf, cand, thr)

    thr = lax.fori_loop(0, 32, bit_body, jnp.full((1, tq), INT_MIN, I32))

    need = kf - count(lambda j: jnp.where(key_s[j] > thr, 1.0, 0.0))
    n_tied = count(lambda j: jnp.where(key_s[j] == thr, 1.0, 0.0))
    surplus = jnp.where(thr > HALF_NEG_KEY, n_tied - need, 0.0)

    def tie_search():
        def tie_body(it, j0):
            cand = j0 + lax.shift_left(jnp.int32(1), idx_bits - 1 - it)
            cnt = count(lambda j: jnp.where(key_s[j] == thr, jnp.where(krow + j * LANES < cand, 1.0, 0.0), 0.0))
            return jnp.where(cnt < need, cand, j0)
        return lax.fori_loop(0, idx_bits, tie_body, jnp.zeros((1, tq), I32))

    j0 = lax.cond(jnp.max(surplus) > 0.0, tie_search, lambda: jnp.full((1, tq), INT_MAX, I32))

    def mask_trip(js, carry):
        for j in js:
            key = key_s[j]
            sel = jnp.where(key > thr, 1.0, jnp.where(key == thr, jnp.where(krow + j * LANES <= j0, 1.0, 0.0), 0.0))
            sel = jnp.where(key > HALF_NEG_KEY, sel, 0.0)
            mask_s[j] = jnp.where(sel > 0.0, 0.0, NEG_INF)
        return carry

    _tile_loop(nt, mask_trip, 0)

    scale = ATT_HEAD_DIM ** -0.5
    heads = range(ATT_HEADS)
    qh = [qt[h * ATT_HEAD_DIM:(h + 1) * ATT_HEAD_DIM] for h in heads]

    def log_trip(js, mx):
        mx = list(mx)
        for j in js:
            band_rows = tile_rows(jnp.clip(j - jd + BAND_TILES - 1, 0, BAND_TILES - 1))
            msk = mask_s[j]
            for h in heads:
                kh = k_ref[tile_rows(j), h * ATT_HEAD_DIM:(h + 1) * ATT_HEAD_DIM]
                lg = jnp.dot(kh, qh[h], preferred_element_type=F32) * scale + band_ref[h, band_rows, :] + msk
                log_s[h, j] = lg
                mx[h] = jnp.maximum(mx[h], _fold8(lg, jnp.maximum))
        return tuple(mx)

    mx = _tile_loop(nt, log_trip, tuple(jnp.full((SUBLANES, tq), NEG_INF, F32) for _ in heads))
    m = [jnp.max(mx[h], axis=0, keepdims=True) for h in heads]
    acc_s[...] = jnp.zeros(acc_s.shape, F32)

    def pv_trip(js, ls):
        ls = list(ls)
        for h in heads:
            acc = acc_s[h]
            for j in js:
                p = jnp.exp(log_s[h, j] - m[h])
                ls[h] = ls[h] + _fold8(p)
                vth = vt_ref[j, h * ATT_HEAD_DIM:(h + 1) * ATT_HEAD_DIM, :]
                acc = acc + jnp.dot(vth, p.astype(BF16), preferred_element_type=F32)
            acc_s[h] = acc
        return tuple(ls)

    ls = _tile_loop(nt, pv_trip, tuple(jnp.zeros((SUBLANES, tq), F32) for _ in heads))
    for h in heads:
        ot = acc_s[h] / jnp.sum(ls[h], axis=0, keepdims=True)
        o_ref[:, h * ATT_HEAD_DIM:(h + 1) * ATT_HEAD_DIM] = jnp.transpose(ot).astype(o_ref.dtype)


def _attention(aq, iq, tail, keys_k, keys_v, keys_i, band, b, t, past_len):
    l = past_len + t
    tq = LANES
    nqb = -(-t // tq)
    tp = nqb * tq
    assert past_len % LANES == 0 and (t % tq == 0 or nqb == 1)
    jd0 = past_len // LANES
    ntiles = jd0 + nqb
    ntp = ntiles
    lp = ntp * LANES
    n_sel = min(TOPK_MAX, l // 4)
    if tp != t:
        padq = lambda a: jnp.pad(a.reshape(b, t, -1), ((0, 0), (0, tp - t), (0, 0))).reshape(b * tp, -1)
        aq, iq, tail = padq(aq), padq(iq), padq(tail)
    w = ATT_HEADS * ATT_HEAD_DIM
    pad = ((0, 0), (0, lp - l), (0, 0))
    kk = jnp.pad(keys_k, pad)
    vt = jnp.pad(keys_v, pad).reshape(b, ntp, LANES, w).transpose(0, 1, 3, 2)
    ki2 = jnp.pad(keys_i, ((0, 0), (0, lp - l), (0, LANES - IDX_DIM)))
    cc = jnp.arange(LANES)[:, None]
    tt = jnp.arange(tq)[None, :]
    adm = ((cc // CHUNK <= tt // CHUNK) & (cc < l - (ntiles - 1) * LANES)).astype(F32)
    qspec = lambda width: pl.BlockSpec((tq, width), lambda bi, qi: (bi * nqb + qi, 0))
    kspec = lambda width: pl.BlockSpec((None, lp, width), lambda bi, qi: (bi, 0, 0))
    kern = functools.partial(_attention_kernel, jd0=jd0, n_sel=n_sel, idx_bits=max((lp - 1).bit_length(), 1))
    out = pl.pallas_call(
        kern,
        grid=(b, nqb),
        in_specs=[qspec(w), qspec(IDX_HEADS * IDX_DIM), qspec(LANES),
                  kspec(w), pl.BlockSpec((None, ntp, w, LANES), lambda bi, qi: (bi, 0, 0, 0)), kspec(LANES),
                  pl.BlockSpec((ATT_HEADS, BAND_TILES * LANES, tq), lambda bi, qi: (0, 0, 0)),
                  pl.BlockSpec((LANES, tq), lambda bi, qi: (0, 0))],
        out_specs=qspec(w),
        out_shape=jax.ShapeDtypeStruct((b * tp, w), BF16),
        scratch_shapes=[pltpu.VMEM((LANES, IDX_HEADS * tq), BF16),
                        pltpu.VMEM((ntp, LANES, tq), I32), pltpu.VMEM((ntp, LANES, tq), F32),
                        pltpu.VMEM((ATT_HEADS, ntp, LANES, tq), F32), pltpu.VMEM((ATT_HEADS, ATT_HEAD_DIM, tq), F32)],
        compiler_params=_cparams(("parallel", "arbitrary")),
        name="attention",
    )(aq, iq, tail, kk, vt, ki2, band, adm)
    if tp != t:
        out = out.reshape(b, tp, w)[:, :t].reshape(b * t, w)
    return out


def _store_tile_major(ref, x):
    rows, width = x.shape
    nk = width // LANES
    for c in range(nk):
        ref[pl.ds(c, rows, stride=nk), :] = x[:, c * LANES:(c + 1) * LANES]


def _load_tile_major(ref, rows, nk, first=0, stride=None):
    stride = nk if stride is None else stride
    return jnp.concatenate([ref[pl.ds(first + c, rows, stride=stride), :] for c in range(nk)], axis=1)


def _outproj_kernel(x_ref, oret_ref, oatt_ref, wo_ref, g_ref, wr_ref, br_ref, x1_ref, h2_ref, e_ref, gt_ref):
    nr = oret_ref.shape[1]
    mix = jnp.dot(oret_ref[...], wo_ref[:nr, :], preferred_element_type=F32)
    mix = mix + jnp.dot(oatt_ref[...], wo_ref[nr:, :], preferred_element_type=F32)
    x1 = x_ref[...] + mix
    x1_ref[...] = x1
    h2 = (x1 * lax.rsqrt(jnp.mean(jnp.square(x1), -1, keepdims=True) + EPS)) * g_ref[...]
    _store_tile_major(h2_ref, h2)
    lg = jnp.dot(h2.astype(BF16), wr_ref[...], preferred_element_type=F32) + br_ref[...]
    tm = lg.shape[0]
    gl = lg[:, :N_GROUPS]
    gmax = jnp.max(gl, -1, keepdims=True)
    p_top = 1.0 / jnp.sum(jnp.exp(gl - gmax), -1, keepdims=True)
    gi = lax.broadcasted_iota(I32, (tm, N_GROUPS), 1).astype(F32)
    g_top = jnp.min(jnp.where(gl == gmax, gi, float(N_GROUPS)), -1, keepdims=True)
    el = jnp.zeros((tm, EXPERTS_PER_GROUP), F32)
    for g in range(N_GROUPS):
        lo = N_GROUPS + g * EXPERTS_PER_GROUP
        el = jnp.where(g_top == float(g), lg[:, lo:lo + EXPERTS_PER_GROUP], el)
    ei = lax.broadcasted_iota(I32, (tm, EXPERTS_PER_GROUP), 1).astype(F32)
    v1 = jnp.max(el, -1, keepdims=True)
    i1 = jnp.min(jnp.where(el == v1, ei, float(EXPERTS_PER_GROUP)), -1, keepdims=True)
    el2 = jnp.where(ei == i1, -jnp.inf, el)
    v2 = jnp.max(el2, -1, keepdims=True)
    i2 = jnp.min(jnp.where(el2 == v2, ei, float(EXPERTS_PER_GROUP)), -1, keepdims=True)
    e2 = jnp.exp(v2 - v1)
    den = 1.0 + e2
    two = lax.broadcasted_iota(I32, (tm, TOP_E), 1)
    e_ref[...] = (g_top * EXPERTS_PER_GROUP + jnp.where(two == 0, i1, i2)).astype(I32)
    gt_ref[...] = jnp.where(two == 0, 1.0 / den, e2 / den) * p_top


def _outproj_router(x2, o_ret, o_att, w_out, g_ffn, w_group, b_group, w_er, b_er):
    n, d = x2.shape
    tm = _row_tile(n)
    nk = d // LANES
    nrt = N_GROUPS + N_EXPERTS
    wr = jnp.pad(jnp.concatenate([w_group, w_er.reshape(d, N_EXPERTS)], 1), ((0, 0), (0, LANES - nrt))).astype(BF16)
    br = jnp.pad(jnp.concatenate([b_group, b_er.reshape(N_EXPERTS)]), (0, LANES - nrt)).reshape(1, LANES)
    rspec = lambda w: pl.BlockSpec((tm, w), lambda i: (i, 0))
    cspec = lambda r, c: pl.BlockSpec((r, c), lambda i: (0, 0))
    mw = w_out.shape[0]
    return pl.pallas_call(
        _outproj_kernel,
        grid=(n // tm,),
        in_specs=[rspec(d), rspec(o_ret.shape[1]), rspec(o_att.shape[1]), cspec(mw, d), cspec(1, d),
                  cspec(d, LANES), cspec(1, LANES)],
        out_specs=[rspec(d), pl.BlockSpec((tm * nk, LANES), lambda i: (i, 0)), rspec(TOP_E), rspec(TOP_E)],
        out_shape=[jax.ShapeDtypeStruct((n, d), F32), jax.ShapeDtypeStruct((n * nk, LANES), F32),
                   jax.ShapeDtypeStruct((n, TOP_E), I32), jax.ShapeDtypeStruct((n, TOP_E), F32)],
        compiler_params=_cparams(("parallel",)),
        name="outproj_router",
    )(x2, o_ret, o_att, w_out.astype(BF16), g_ffn.reshape(1, d), wr, br)


def _expert_kernel(blk_e_ref, tok0_ref, tokn_ref, slot_ref, h_ref, wg_ref, wu_ref, wd_ref, o_ref,
                   xbuf, obuf, gsem, ssem):
    i = pl.program_id(0)
    nb = pl.num_programs(0)
    nk = wg_ref.shape[0] // LANES
    rb = xbuf.shape[0] // (2 * nk)
    blk = rb * nk
    cur = i % 2
    nxt = 1 - cur

    def lines(first, count):
        return pl.ds(pl.multiple_of(first, nk), count)

    def for_rows(start_row):
        def body(g, carry):
            for u in range(DMA_UNROLL):
                start_row(g * DMA_UNROLL + u, u % 2)
            return carry
        lax.fori_loop(0, rb // DMA_UNROLL, body, 0)

    def start_gather(tok_ref, buf):
        for_rows(lambda r, pri: pltpu.make_async_copy(
            h_ref.at[lines(tok_ref[0, 0, r], nk)], xbuf.at[lines(buf * blk + r * nk, nk)],
            gsem.at[buf]).start(priority=pri))

    def wait_gather(buf):
        pltpu.make_async_copy(h_ref.at[pl.ds(0, blk)], xbuf.at[lines(buf * blk, blk)], gsem.at[buf]).wait()

    def wait_scatter(buf):
        pltpu.make_async_copy(obuf.at[lines(buf * blk, blk)], o_ref.at[pl.ds(0, blk)], ssem.at[buf]).wait()

    @pl.when(i == 0)
    def _():
        start_gather(tok0_ref, 0)

    @pl.when(i + 1 < nb)
    def _():
        start_gather(tokn_ref, nxt)

    wait_gather(cur)
    base = cur * blk
    xb = jnp.concatenate([xbuf[pl.ds(base + c, rb, stride=nk), :] for c in range(nk)], axis=1).astype(BF16)
    hid = jax.nn.silu(jnp.dot(xb, wg_ref[...], preferred_element_type=F32))
    hid = hid * jnp.dot(xb, wu_ref[...], preferred_element_type=F32)
    out = jnp.dot(hid.astype(BF16), wd_ref[...], preferred_element_type=F32)

    @pl.when(i >= 2)
    def _():
        wait_scatter(cur)

    for c in range(nk):
        obuf[pl.ds(base + c, rb, stride=nk), :] = out[:, c * LANES:(c + 1) * LANES]

    for_rows(lambda r, pri: pltpu.make_async_copy(
        obuf.at[lines(base + r * nk, nk)], o_ref.at[lines(slot_ref[0, 0, r], nk)], ssem.at[cur]).start(priority=pri))

    @pl.when(i == nb - 1)
    def _():
        wait_scatter(cur)

        @pl.when(nb >= 2)
        def _():
            wait_scatter(nxt)


def _experts(h2, src_tok, out_slot, blk_e, w_gate, w_up, w_down, rb):
    d, de = w_gate.shape[1:]
    nk = d // LANES
    rows = src_tok.shape[0]
    nb = rows // rb
    idx3 = lambda a: (a * nk).reshape(nb, 1, rb)
    ispec = lambda f: pl.BlockSpec((1, 1, rb), f, memory_space=pltpu.SMEM)
    wspec = lambda r, c: pl.BlockSpec((None, r, c), lambda i, be: (be[i], 0, 0))
    return pl.pallas_call(
        _expert_kernel,
        grid_spec=pltpu.PrefetchScalarGridSpec(
            num_scalar_prefetch=1,
            grid=(nb,),
            in_specs=[ispec(lambda i, be: (0, 0, 0)), ispec(lambda i, be: (jnp.minimum(i + 1, nb - 1), 0, 0)),
                      ispec(lambda i, be: (i, 0, 0)), pl.BlockSpec(memory_space=pl.ANY),
                      wspec(d, de), wspec(d, de), wspec(de, d)],
            out_specs=pl.BlockSpec(memory_space=pl.ANY),
            scratch_shapes=[pltpu.VMEM((2 * rb * nk, LANES), F32), pltpu.VMEM((2 * rb * nk, LANES), F32),
                            pltpu.SemaphoreType.DMA((2,)), pltpu.SemaphoreType.DMA((2,))],
        ),
        out_shape=jax.ShapeDtypeStruct((rows * nk, LANES), F32),
        compiler_params=_cparams(("arbitrary",)),
        name="experts",
    )(blk_e, idx3(src_tok), idx3(src_tok), idx3(out_slot), h2,
      w_gate.astype(BF16), w_up.astype(BF16), w_down.astype(BF16))


def _route_plan(expert, rb):
    n = expert.shape[0]
    a = n * TOP_E
    flat_e = expert.reshape(a)
    counts = jnp.sum((flat_e[:, None] == jnp.arange(N_EXPERTS, dtype=I32)[None, :]).astype(I32), axis=0)
    padded = (counts + rb - 1) // rb * rb
    ends = jnp.cumsum(padded)
    n_blocks = (a + N_EXPERTS * (rb - 1) + rb - 1) // rb
    rows = n_blocks * rb
    q = jnp.arange(rb - 1, dtype=I32)[None, :]
    e = jnp.arange(N_EXPERTS, dtype=I32)[:, None]
    pad_key = jnp.where(q < (padded - counts)[:, None], 2 * e + 1, 2 * N_EXPERTS).reshape(-1)
    spare = jnp.full((rows - a - N_EXPERTS * (rb - 1),), 2 * N_EXPERTS, I32)
    perm = jnp.argsort(jnp.concatenate([2 * flat_e, pad_key, spare]), stable=True).astype(I32)
    src_tok = jnp.where(perm < a, perm // TOP_E, 0)
    out_slot = perm
    blk_e = jnp.minimum(jnp.sum((ends[None, :] <= jnp.arange(n_blocks, dtype=I32)[:, None] * rb).astype(I32), axis=1),
                        N_EXPERTS - 1)
    return src_tok, out_slot, blk_e


def _combine_kernel(x1_ref, eo_ref, gt_ref, g_ref, y_ref):
    tm, d = x1_ref.shape
    nk = d // LANES
    gt = gt_ref[...]
    moe = _load_tile_major(eo_ref, tm, nk, 0, TOP_E * nk) * gt[:, 0:1]
    moe = moe + _load_tile_major(eo_ref, tm, nk, nk, TOP_E * nk) * gt[:, 1:2]
    x = x1_ref[...] + moe
    y_ref[...] = (x * lax.rsqrt(jnp.mean(jnp.square(x), -1, keepdims=True) + EPS)) * g_ref[...]


def _combine(x1, eo2, gate, g_final):
    n, d = x1.shape
    tm = _row_tile(n)
    nk = d // LANES
    rspec = lambda w: pl.BlockSpec((tm, w), lambda i: (i, 0))
    return pl.pallas_call(
        _combine_kernel,
        grid=(n // tm,),
        in_specs=[rspec(d), pl.BlockSpec((tm * TOP_E * nk, LANES), lambda i: (i, 0)), rspec(TOP_E),
                  pl.BlockSpec((1, d), lambda i: (0, 0))],
        out_specs=rspec(d),
        out_shape=jax.ShapeDtypeStruct((n, d), F32),
        compiler_params=_cparams(("parallel",)),
        name="combine",
    )(x1, eo2, gate, g_final.reshape(1, d))


def _layer(x, past_len, s_ret, past_k, past_v, past_ki, band, params, g_final):
    (rel_bias, g_mix, w_in, g_ret, w_out, g_ffn, w_group, b_group, w_er, b_er, w_gate, w_up, w_down) = params
    b, t, d = x.shape
    n = b * t
    nk = d // LANES
    x2 = x.reshape(n, d)
    rq, rk, rv, rg, aq, ak, av, akb, avb, iq, ik, tail = _inproj(x2, g_mix, w_in)
    o_ret, s_new = _retention(rq, rk, rv, rg, s_ret, g_ret, b, t, past_len)
    aw = ATT_HEADS * ATT_HEAD_DIM
    keys_k = jnp.concatenate([past_k.reshape(b, past_len, aw).astype(BF16), akb.reshape(b, t, aw)], 1)
    keys_v = jnp.concatenate([past_v.reshape(b, past_len, aw).astype(BF16), avb.reshape(b, t, aw)], 1)
    keys_i = jnp.concatenate([past_ki.astype(BF16), ik.reshape(b, t, IDX_DIM).astype(BF16)], 1)
    o_att = _attention(aq, iq, tail, keys_k, keys_v, keys_i, band, b, t, past_len)
    x1, h2, expert, gate = _outproj_router(x2, o_ret, o_att, w_out, g_ffn, w_group, b_group, w_er, b_er)
    rb = 512 if n >= 8192 else 128
    src_tok, out_slot, blk_e = _route_plan(expert, rb)
    eo = _experts(h2, src_tok, out_slot, blk_e, w_gate, w_up, w_down, rb)
    y = _combine(x1, eo, gate, g_final)
    return (y.reshape(b, t, d), ak.reshape(1, b, t, ATT_HEADS, ATT_HEAD_DIM),
            av.reshape(1, b, t, ATT_HEADS, ATT_HEAD_DIM), ik.reshape(1, b, t, IDX_DIM), s_new[None])


def kernel(x_prompt, x_sample, cache_attn_k, cache_attn_v, cache_idx_k, state_ret, rel_bias, g_mix, w_in, g_ret, w_out, g_ffn, w_group, b_group, w_expert_router, b_expert_router, w_gate, w_up, w_down, g_final):
    assert g_mix.shape[0] == 1, "single-layer model"
    params = (rel_bias, g_mix[0], w_in[0], g_ret[0], w_out[0], g_ffn[0], w_group[0], b_group[0],
              w_expert_router[0], b_expert_router[0], w_gate[0], w_up[0], w_down[0])
    band = _bias_band(rel_bias)
    nb = x_prompt.shape[0]
    past_len = cache_attn_k.shape[2]
    dt = x_prompt.dtype
    empty_kv = jnp.zeros((nb, 0, ATT_HEADS, ATT_HEAD_DIM), dt)
    empty_ki = jnp.zeros((nb, 0, IDX_DIM), dt)
    s0 = jnp.zeros((nb, RET_HEADS, RET_DK, RET_DV), dt)
    yp, kp, vp, kip, sp = _layer(x_prompt, 0, s0, empty_kv, empty_kv, empty_ki, band, params, g_final)
    ys, ks, vs, kis, ss = _layer(x_sample, past_len, state_ret[0], cache_attn_k[0], cache_attn_v[0],
                                 cache_idx_k[0], band, params, g_final)
    return (yp, ys, kp, vp, kip, sp, ks, vs, kis, ss)
```

```python
import functools
import math

import jax
import jax.numpy as jnp
import numpy as np
from jax import lax
from jax.experimental import pallas as pl
from jax.experimental.pallas import tpu as pltpu

F32 = jnp.float32
BF16 = jnp.bfloat16
I32 = jnp.int32

CHUNK = 64
RET_HEADS = 4
RET_DK = 128
RET_DV = 128
ROPE_BASE = 10000.0
ATT_HEADS = 4
ATT_HEAD_DIM = 128
IDX_HEADS = 8
IDX_DIM = 64
TOPK_MAX = 256
NUM_BUCKETS = 32
MAX_DISTANCE = 128
N_GROUPS = 4
EXPERTS_PER_GROUP = 8
N_EXPERTS = N_GROUPS * EXPERTS_PER_GROUP
TOP_E = 2
EPS = 1e-6
NEG_INF = -1e30

LANES = 128
SUBLANES = 8
VMEM_LIMIT = 56 * 1024 * 1024
RET_CHUNK = 256
DMA_UNROLL = 8
INT_MIN = -(2 ** 31)
INT_MAX = 2 ** 31 - 1
BAND_TILES = 3
assert (BAND_TILES - 2) * LANES + 1 >= MAX_DISTANCE
assert math.log(IDX_DIM, 4).is_integer()


def _f32_key_const(v):
    b = int(np.array(v, np.float32).view(np.int32))
    return b ^ ((b >> 31) & 0x7FFFFFFF)


HALF_NEG_KEY = _f32_key_const(0.5 * NEG_INF)


def _cparams(sem):
    return pltpu.CompilerParams(dimension_semantics=sem, vmem_limit_bytes=VMEM_LIMIT)


def _row_tile(n, want=256):
    return want if n % want == 0 else n


def _inproj_kernel(x_ref, g_ref, wm_ref, wt_ref, rq, rk, rv, rg, aq, ak, av, akb, avb, iq, ik, tail):
    x = x_ref[...]
    h = (x * lax.rsqrt(jnp.mean(jnp.square(x), -1, keepdims=True) + EPS)) * g_ref[...]
    hb = h.astype(BF16)

    def proj(i):
        return jnp.dot(hb, wm_ref[:, i * 512:(i + 1) * 512], preferred_element_type=F32)

    rq[...] = proj(0)
    rk[...] = proj(1)
    rv[...] = proj(2).astype(BF16)
    rg[...] = proj(3)
    aq[...] = proj(4).astype(BF16)
    k = proj(5)
    _store_tile_major(ak, k)
    akb[...] = k.astype(BF16)
    v = proj(6)
    _store_tile_major(av, v)
    avb[...] = v.astype(BF16)
    iq[...] = proj(7).astype(BF16)
    t = jnp.dot(hb, wt_ref[...], preferred_element_type=F32)
    ik[...] = t[:, :IDX_DIM]
    tail[...] = t


def _inproj(x2, g_mix, w_in):
    n, d = x2.shape
    tm = _row_tile(n)
    wm = w_in[:, :4096].astype(BF16)
    wt = jnp.pad(w_in[:, 4096:], ((0, 0), (0, LANES - (w_in.shape[1] - 4096)))).astype(BF16)
    rspec = lambda w, lines=1: pl.BlockSpec((tm * lines, w), lambda i: (i, 0))
    outs = [(512, F32, 1), (512, F32, 1), (512, BF16, 1), (512, F32, 1), (512, BF16, 1),
            (LANES, F32, ATT_HEADS), (LANES, F32, ATT_HEADS),
            (512, BF16, 1), (512, BF16, 1), (512, BF16, 1), (IDX_DIM, F32, 1), (LANES, F32, 1)]
    return pl.pallas_call(
        _inproj_kernel,
        grid=(n // tm,),
        in_specs=[rspec(d), pl.BlockSpec((1, d), lambda i: (0, 0)),
                  pl.BlockSpec((d, 4096), lambda i: (0, 0)), pl.BlockSpec((d, LANES), lambda i: (0, 0))],
        out_specs=[rspec(w, ln) for w, _, ln in outs],
        out_shape=[jax.ShapeDtypeStruct((n * ln, w), dt) for w, dt, ln in outs],
        compiler_params=_cparams(("parallel",)),
        name="inproj",
    )(x2, g_mix.reshape(1, d), wm, wt)


def _retention_kernel(cd_ref, rq_ref, rk_ref, rv_ref, rg_ref, cos_ref, sin_ref, dmat_ref, qd_ref, kd_ref,
                      gret_ref, s0_ref, o_ref, s_ref):
    c = pl.program_id(1)

    @pl.when(c == 0)
    def _():
        s_ref[...] = s0_ref[...]

    cosf = cos_ref[...]
    sinf = sin_ref[...]
    half = RET_DK // 2

    def rot(x):
        return x * cosf + pltpu.roll(x, half, 1) * sinf

    for h in range(RET_HEADS):
        sl = slice(h * RET_DK, (h + 1) * RET_DK)
        q = rot(rq_ref[:, sl])
        k = rot(rk_ref[:, sl]) * (RET_DK ** -0.5)
        v = rv_ref[:, sl]
        s = s_ref[0, h]
        sc = lax.dot_general(q.astype(BF16), k.astype(BF16), (((1,), (1,)), ((), ())),
                             preferred_element_type=F32) * dmat_ref[h]
        o = jnp.dot(sc.astype(BF16), v, preferred_element_type=F32)
        o = o + jnp.dot((q * qd_ref[:, sl]).astype(BF16), s.astype(BF16), preferred_element_type=F32)
        kdt = jnp.transpose(k * kd_ref[:, sl]).astype(BF16)
        s_ref[0, h] = cd_ref[h] * s + jnp.dot(kdt, v, preferred_element_type=F32)
        mu = jnp.mean(o, -1, keepdims=True)
        var = jnp.mean(jnp.square(o - mu), -1, keepdims=True)
        on = (o - mu) * lax.rsqrt(var + EPS) * gret_ref[:, sl]
        o_ref[:, sl] = (jax.nn.silu(rg_ref[:, sl]) * on).astype(o_ref.dtype)


def _retention(rq, rk, rv, rg, s0, g_ret, b, t, past_len):
    cl = min(RET_CHUNK, t)
    nc = t // cl
    half = RET_DK // 2
    pos = (past_len + jnp.arange(t)).astype(F32)
    inv = ROPE_BASE ** (-jnp.arange(half, dtype=F32) / half)
    ang = pos[:, None] * inv[None, :]
    cosf = jnp.concatenate([jnp.cos(ang), jnp.cos(ang)], -1)
    sinf = jnp.concatenate([-jnp.sin(ang), jnp.sin(ang)], -1)
    log_g = jnp.log1p(-jnp.exp2(-5.0 - jnp.arange(RET_HEADS, dtype=F32)))
    i = jnp.arange(cl, dtype=F32)
    diff = i[:, None] - i[None, :]
    dmat = jnp.where(diff[None] >= 0, jnp.exp(jnp.maximum(diff, 0.0)[None] * log_g[:, None, None]), 0.0)
    kd = jnp.repeat(jnp.exp((cl - 1.0 - i)[:, None] * log_g[None, :]), RET_DK, axis=1)
    qd = jnp.repeat(jnp.exp((i + 1.0)[:, None] * log_g[None, :]), RET_DK, axis=1)
    cd = jnp.exp(cl * log_g)
    w = RET_HEADS * RET_DK
    rspec = pl.BlockSpec((cl, w), lambda bi, ci: (bi * nc + ci, 0))
    cspec = lambda shape: pl.BlockSpec(shape, lambda bi, ci: (0,) * len(shape))
    sspec = pl.BlockSpec((1, RET_HEADS, RET_DK, RET_DV), lambda bi, ci: (bi, 0, 0, 0))
    return pl.pallas_call(
        _retention_kernel,
        grid=(b, nc),
        in_specs=[pl.BlockSpec(memory_space=pltpu.SMEM), rspec, rspec, rspec, rspec,
                  pl.BlockSpec((cl, RET_DK), lambda bi, ci: (ci, 0)),
                  pl.BlockSpec((cl, RET_DK), lambda bi, ci: (ci, 0)),
                  cspec((RET_HEADS, cl, cl)), cspec((cl, w)), cspec((cl, w)), cspec((1, w)), sspec],
        out_specs=[rspec, sspec],
        out_shape=[jax.ShapeDtypeStruct((b * t, w), BF16),
                   jax.ShapeDtypeStruct((b, RET_HEADS, RET_DK, RET_DV), F32)],
        compiler_params=_cparams(("parallel", "arbitrary")),
        name="retention",
    )(cd, rq, rk, rv, rg, cosf, sinf, dmat, qd, kd, g_ret.reshape(1, w), s0)


def _t5_bucket(rel):
    nb = NUM_BUCKETS // 2
    max_exact = nb // 2
    base = jnp.where(rel > 0, nb, 0)
    n = jnp.abs(rel)
    nf = jnp.maximum(n, 1).astype(F32)
    large = max_exact + (jnp.log(nf / max_exact) / math.log(MAX_DISTANCE / max_exact) * (nb - max_exact)).astype(I32)
    large = jnp.minimum(large, nb - 1)
    return base + jnp.where(n < max_exact, n, large)


def _band_kernel(rb_ref, bucket_ref, band_ref):
    bucket = bucket_ref[...]
    for h in range(ATT_HEADS):
        acc = jnp.zeros(bucket.shape, F32)
        for j in range(NUM_BUCKETS):
            acc = jnp.where(bucket == j, rb_ref[j, h], acc)
        band_ref[h] = acc


def _bias_band(rel_bias):
    c = jnp.arange(BAND_TILES * LANES, dtype=I32)[:, None]
    t = jnp.arange(LANES, dtype=I32)[None, :]
    bucket = _t5_bucket(c - (BAND_TILES - 1) * LANES - t)
    return pl.pallas_call(
        _band_kernel,
        in_specs=[pl.BlockSpec(memory_space=pltpu.SMEM), pl.BlockSpec(memory_space=pltpu.VMEM)],
        out_specs=pl.BlockSpec(memory_space=pltpu.VMEM),
        out_shape=jax.ShapeDtypeStruct((ATT_HEADS, BAND_TILES * LANES, LANES), F32),
        name="bias_band",
    )(rel_bias, bucket)


def _order_key(s):
    bits = lax.bitcast_convert_type(s, I32)
    return bits ^ ((bits >> 31) & 0x7FFFFFFF)


def _fold8(x, op=jnp.add):
    parts = [x[i * SUBLANES:(i + 1) * SUBLANES] for i in range(x.shape[0] // SUBLANES)]
    while len(parts) > 1:
        parts = [op(parts[i], parts[i + 1]) for i in range(0, len(parts), 2)]
    return parts[0]


GROUP = LANES // SUBLANES


def _sort_network(n):
    pairs = []
    p = 1
    while p < n:
        k = p
        while k >= 1:
            for j in range(k % p, n - k, 2 * k):
                for i in range(min(k, n - j - k)):
                    if (i + j) // (2 * p) == (i + j + k) // (2 * p):
                        pairs.append((i + j, i + j + k))
            k //= 2
        p *= 2
    return pairs


def _sort_desc(vals):
    vals = list(vals)
    for a, b in _sort_network(len(vals)):
        vals[a], vals[b] = jnp.maximum(vals[a], vals[b]), jnp.minimum(vals[a], vals[b])
    return vals


def _tile_loop(nt, trip, carry):
    def run(first, trips, width, carry):
        return lax.fori_loop(0, trips, lambda i, c: trip([first + i * width + u for u in range(width)], c), carry)
    carry = run(0, nt // 4, 4, carry)
    carry = run(nt // 4 * 4, (nt // 2) % 2, 2, carry)
    return run(nt // 2 * 2, nt % 2, 1, carry)


def _attention_kernel(q_ref, iq_ref, tail_ref, k_ref, vt_ref, ki_ref, band_ref, adm_ref, o_ref,
                      iqs_s, key_s, ks_s, mask_s, log_s, acc_s, *, jd0, n_sel, idx_bits):
    tq = LANES
    jd = jd0 + pl.program_id(1)
    nt = jd + 1
    krow = lax.broadcasted_iota(I32, (LANES, tq), 0)

    qt = jnp.transpose(q_ref[...].astype(F32)).astype(BF16)
    iqt = jnp.transpose(iq_ref[...].astype(F32)).astype(BF16)
    wt = jnp.transpose(tail_ref[...])[IDX_DIM:IDX_DIM + IDX_HEADS] * (IDX_HEADS ** -0.5) * (IDX_DIM ** -0.5)
    zpad = jnp.zeros((LANES - IDX_DIM, tq), BF16)
    for h in range(IDX_HEADS):
        iqs_s[:, h * tq:(h + 1) * tq] = jnp.concatenate([iqt[h * IDX_DIM:(h + 1) * IDX_DIM], zpad], axis=0)

    def tile_rows(j):
        return pl.ds(pl.multiple_of(j * LANES, LANES), LANES)

    def score_tile(j):
        d = jnp.dot(ki_ref[tile_rows(j), :], iqs_s[...], preferred_element_type=F32)
        acc = jnp.zeros((LANES, tq), F32)
        for h in range(IDX_HEADS):
            acc = acc + wt[h:h + 1] * jnp.maximum(d[:, h * tq:(h + 1) * tq], 0.0)
        return acc

    def score_trip(js, carry):
        for j in js:
            adm = jnp.where(j == jd, adm_ref[...], 1.0)
            key = jnp.where(adm > 0.0, _order_key(score_tile(j)), _f32_key_const(NEG_INF))
            key_s[j] = key
            srt = _sort_desc([key[g * SUBLANES:(g + 1) * SUBLANES] for g in range(GROUP)])
            for g in range(GROUP):
                ks_s[j, g * SUBLANES:(g + 1) * SUBLANES] = srt[g]
        return carry

    _tile_loop(nt, score_trip, 0)
    nquad = (nt + 3) // 4

    def fill_body(j, carry):
        ks_s[j] = jnp.full((LANES, tq), INT_MIN, I32)
        return carry

    lax.fori_loop(nt, nquad * 4, fill_body, 0)

    def count(pred_tile):
        def trip(js, acc):
            for j in js:
                acc = acc + _fold8(pred_tile(j))
            return acc
        return jnp.sum(_tile_loop(nt, trip, jnp.zeros((SUBLANES, tq), F32)), axis=0, keepdims=True)

    def count_ge(cand):
        def one(m):
            return jnp.where(m, 1.0, 0.0)

        def trip(q, accs):
            a8, a4, a2, a1 = accs
            for u in range(4):
                v = [ks_s[4 * q + u, g * SUBLANES:(g + 1) * SUBLANES] for g in range(GROUP)]
                t1 = v[7] >= cand
                t2 = jnp.where(t1, v[11], v[3]) >= cand
                t3 = jnp.where(t1, jnp.where(t2, v[13], v[9]), jnp.where(t2, v[5], v[1])) >= cand
                t4 = jnp.where(t1, jnp.where(t2, jnp.where(t3, v[14], v[12]), jnp.where(t3, v[10], v[8])),
                               jnp.where(t2, jnp.where(t3, v[6], v[4]), jnp.where(t3, v[2], v[0]))) >= cand
                a8, a4, a2 = a8 + one(t1), a4 + one(t2), a2 + one(t3)
                a1 = a1 + one(t4) + one(v[15] >= cand)
            return a8, a4, a2, a1

        zero = jnp.zeros((SUBLANES, tq), F32)
        a8, a4, a2, a1 = lax.fori_loop(0, nquad, trip, (zero, zero, zero, zero))
        return jnp.sum(8.0 * a8 + 4.0 * a4 + 2.0 * a2 + a1, axis=0, keepdims=True)

    kf = float(n_sel)

    def bit_body(it, thr):
        cand = thr + lax.shift_left(jnp.int32(1), 31 - it)
        return jnp.where(count_ge(cand) >= kf, cand, thr)

    thr = lax.fori_loop(0, 32, bit_body, jnp.full((1, tq), INT_MIN, I32))

    need = kf - count(lambda j: jnp.where(key_s[j] > thr, 1.0, 0.0))
    n_tied = count(lambda j: jnp.where(key_s[j] == thr, 1.0, 0.0))
    surplus = jnp.where(thr > HALF_NEG_KEY, n_tied - need, 0.0)

    def tie_search():
        def tie_body(it, j0):
            cand = j0 + lax.shift_left(jnp.int32(1), idx_bits - 1 - it)
            cnt = count(lambda j: jnp.where(key_s[j] == thr, jnp.where(krow + j * LANES < cand, 1.0, 0.0), 0.0))
            return jnp.where(cnt < need, cand, j0)
        return lax.fori_loop(0, idx_bits, tie_body, jnp.zeros((1, tq), I32))

    j0 = lax.cond(jnp.max(surplus) > 0.0, tie_search, lambda: jnp.full((1, tq), INT_MAX, I32))

    def mask_trip(js, carry):
        for j in js:
            key = key_s[j]
            sel = jnp.where(key > thr, 1.0, jnp.where(key == thr, jnp.where(krow + j * LANES <= j0, 1.0, 0.0), 0.0))
            sel = jnp.where(key > HALF_NEG_KEY, sel, 0.0)
            mask_s[j] = jnp.where(sel > 0.0, 0.0, NEG_INF)
        return carry

    _tile_loop(nt, mask_trip, 0)

    scale = ATT_HEAD_DIM ** -0.5
    heads = range(ATT_HEADS)
    qh = [qt[h * ATT_HEAD_DIM:(h + 1) * ATT_HEAD_DIM] for h in heads]

    def log_trip(js, mx):
        mx = list(mx)
        for j in js:
            band_rows = tile_rows(jnp.clip(j - jd + BAND_TILES - 1, 0, BAND_TILES - 1))
            msk = mask_s[j]
            for h in heads:
                kh = k_ref[tile_rows(j), h * ATT_HEAD_DIM:(h + 1) * ATT_HEAD_DIM]
                lg = jnp.dot(kh, qh[h], preferred_element_type=F32) * scale + band_ref[h, band_rows, :] + msk
                log_s[h, j] = lg
                mx[h] = jnp.maximum(mx[h], _fold8(lg, jnp.maximum))
        return tuple(mx)

    mx = _tile_loop(nt, log_trip, tuple(jnp.full((SUBLANES, tq), NEG_INF, F32) for _ in heads))
    m = [jnp.max(mx[h], axis=0, keepdims=True) for h in heads]
    acc_s[...] = jnp.zeros(acc_s.shape, F32)

    def pv_trip(js, ls):
        ls = list(ls)
        for h in heads:
            acc = acc_s[h]
            for j in js:
                p = jnp.exp(log_s[h, j] - m[h])
                ls[h] = ls[h] + _fold8(p)
                vth = vt_ref[j, h * ATT_HEAD_DIM:(h + 1) * ATT_HEAD_DIM, :]
                acc = acc + jnp.dot(vth, p.astype(BF16), preferred_element_type=F32)
            acc_s[h] = acc
        return tuple(ls)

    ls = _tile_loop(nt, pv_trip, tuple(jnp.zeros((SUBLANES, tq), F32) for _ in heads))
    for h in heads:
        ot = acc_s[h] / jnp.sum(ls[h], axis=0, keepdims=True)
        o_ref[:, h * ATT_HEAD_DIM:(h + 1) * ATT_HEAD_DIM] = jnp.transpose(ot).astype(o_ref.dtype)


def _attention(aq, iq, tail, keys_k, keys_v, keys_i, band, b, t, past_len):
    l = past_len + t
    tq = LANES
    nqb = -(-t // tq)
    tp = nqb * tq
    assert past_len % LANES == 0 and (t % tq == 0 or nqb == 1)
    jd0 = past_len // LANES
    ntiles = jd0 + nqb
    ntp = ntiles
    lp = ntp * LANES
    n_sel = min(TOPK_MAX, l // 4)
    if tp != t:
        padq = lambda a: jnp.pad(a.reshape(b, t, -1), ((0, 0), (0, tp - t), (0, 0))).reshape(b * tp, -1)
        aq, iq, tail = padq(aq), padq(iq), padq(tail)
    w = ATT_HEADS * ATT_HEAD_DIM
    pad = ((0, 0), (0, lp - l), (0, 0))
    kk = jnp.pad(keys_k, pad)
    vt = jnp.pad(keys_v, pad).reshape(b, ntp, LANES, w).transpose(0, 1, 3, 2)
    ki2 = jnp.pad(keys_i, ((0, 0), (0, lp - l), (0, LANES - IDX_DIM)))
    cc = jnp.arange(LANES)[:, None]
    tt = jnp.arange(tq)[None, :]
    adm = ((cc // CHUNK <= tt // CHUNK) & (cc < l - (ntiles - 1) * LANES)).astype(F32)
    qspec = lambda width: pl.BlockSpec((tq, width), lambda bi, qi: (bi * nqb + qi, 0))
    kspec = lambda width: pl.BlockSpec((None, lp, width), lambda bi, qi: (bi, 0, 0))
    kern = functools.partial(_attention_kernel, jd0=jd0, n_sel=n_sel, idx_bits=max((lp - 1).bit_length(), 1))
    out = pl.pallas_call(
        kern,
        grid=(b, nqb),
        in_specs=[qspec(w), qspec(IDX_HEADS * IDX_DIM), qspec(LANES),
                  kspec(w), pl.BlockSpec((None, ntp, w, LANES), lambda bi, qi: (bi, 0, 0, 0)), kspec(LANES),
                  pl.BlockSpec((ATT_HEADS, BAND_TILES * LANES, tq), lambda bi, qi: (0, 0, 0)),
                  pl.BlockSpec((LANES, tq), lambda bi, qi: (0, 0))],
        out_specs=qspec(w),
        out_shape=jax.ShapeDtypeStruct((b * tp, w), BF16),
        scratch_shapes=[pltpu.VMEM((LANES, IDX_HEADS * tq), BF16),
                        pltpu.VMEM((ntp, LANES, tq), I32), pltpu.VMEM((-(-ntp // 4) * 4, LANES, tq), I32),
                        pltpu.VMEM((ntp, LANES, tq), F32),
                        pltpu.VMEM((ATT_HEADS, ntp, LANES, tq), F32), pltpu.VMEM((ATT_HEADS, ATT_HEAD_DIM, tq), F32)],
        compiler_params=_cparams(("parallel", "arbitrary")),
        name="attention",
    )(aq, iq, tail, kk, vt, ki2, band, adm)
    if tp != t:
        out = out.reshape(b, tp, w)[:, :t].reshape(b * t, w)
    return out


def _store_tile_major(ref, x):
    rows, width = x.shape
    nk = width // LANES
    for c in range(nk):
        ref[pl.ds(c, rows, stride=nk), :] = x[:, c * LANES:(c + 1) * LANES]


def _load_tile_major(ref, rows, nk, first=0, stride=None):
    stride = nk if stride is None else stride
    return jnp.concatenate([ref[pl.ds(first + c, rows, stride=stride), :] for c in range(nk)], axis=1)


def _outproj_kernel(x_ref, oret_ref, oatt_ref, wo_ref, g_ref, wr_ref, br_ref, x1_ref, h2_ref, e_ref, gt_ref):
    nr = oret_ref.shape[1]
    mix = jnp.dot(oret_ref[...], wo_ref[:nr, :], preferred_element_type=F32)
    mix = mix + jnp.dot(oatt_ref[...], wo_ref[nr:, :], preferred_element_type=F32)
    x1 = x_ref[...] + mix
    x1_ref[...] = x1
    h2 = (x1 * lax.rsqrt(jnp.mean(jnp.square(x1), -1, keepdims=True) + EPS)) * g_ref[...]
    _store_tile_major(h2_ref, h2)
    lg = jnp.dot(h2.astype(BF16), wr_ref[...], preferred_element_type=F32) + br_ref[...]
    tm = lg.shape[0]
    gl = lg[:, :N_GROUPS]
    gmax = jnp.max(gl, -1, keepdims=True)
    p_top = 1.0 / jnp.sum(jnp.exp(gl - gmax), -1, keepdims=True)
    gi = lax.broadcasted_iota(I32, (tm, N_GROUPS), 1).astype(F32)
    g_top = jnp.min(jnp.where(gl == gmax, gi, float(N_GROUPS)), -1, keepdims=True)
    el = jnp.zeros((tm, EXPERTS_PER_GROUP), F32)
    for g in range(N_GROUPS):
        lo = N_GROUPS + g * EXPERTS_PER_GROUP
        el = jnp.where(g_top == float(g), lg[:, lo:lo + EXPERTS_PER_GROUP], el)
    ei = lax.broadcasted_iota(I32, (tm, EXPERTS_PER_GROUP), 1).astype(F32)
    v1 = jnp.max(el, -1, keepdims=True)
    i1 = jnp.min(jnp.where(el == v1, ei, float(EXPERTS_PER_GROUP)), -1, keepdims=True)
    el2 = jnp.where(ei == i1, -jnp.inf, el)
    v2 = jnp.max(el2, -1, keepdims=True)
    i2 = jnp.min(jnp.where(el2 == v2, ei, float(EXPERTS_PER_GROUP)), -1, keepdims=True)
    e2 = jnp.exp(v2 - v1)
    den = 1.0 + e2
    two = lax.broadcasted_iota(I32, (tm, TOP_E), 1)
    e_ref[...] = (g_top * EXPERTS_PER_GROUP + jnp.where(two == 0, i1, i2)).astype(I32)
    gt_ref[...] = jnp.where(two == 0, 1.0 / den, e2 / den) * p_top


def _outproj_router(x2, o_ret, o_att, w_out, g_ffn, w_group, b_group, w_er, b_er):
    n, d = x2.shape
    tm = _row_tile(n, 512)
    nk = d // LANES
    nrt = N_GROUPS + N_EXPERTS
    wr = jnp.pad(jnp.concatenate([w_group, w_er.reshape(d, N_EXPERTS)], 1), ((0, 0), (0, LANES - nrt))).astype(BF16)
    br = jnp.pad(jnp.concatenate([b_group, b_er.reshape(N_EXPERTS)]), (0, LANES - nrt)).reshape(1, LANES)
    rspec = lambda w: pl.BlockSpec((tm, w), lambda i: (i, 0))
    cspec = lambda r, c: pl.BlockSpec((r, c), lambda i: (0, 0))
    mw = w_out.shape[0]
    return pl.pallas_call(
        _outproj_kernel,
        grid=(n // tm,),
        in_specs=[rspec(d), rspec(o_ret.shape[1]), rspec(o_att.shape[1]), cspec(mw, d), cspec(1, d),
                  cspec(d, LANES), cspec(1, LANES)],
        out_specs=[rspec(d), pl.BlockSpec((tm * nk, LANES), lambda i: (i, 0)), rspec(TOP_E), rspec(TOP_E)],
        out_shape=[jax.ShapeDtypeStruct((n, d), F32), jax.ShapeDtypeStruct((n * nk, LANES), F32),
                   jax.ShapeDtypeStruct((n, TOP_E), I32), jax.ShapeDtypeStruct((n, TOP_E), F32)],
        compiler_params=_cparams(("parallel",)),
        name="outproj_router",
    )(x2, o_ret, o_att, w_out.astype(BF16), g_ffn.reshape(1, d), wr, br)


def _expert_kernel(blk_e_ref, tok0_ref, tokn_ref, slot_ref, h_ref, wg_ref, wu_ref, wd_ref, o_ref,
                   xbuf, obuf, gsem, ssem):
    i = pl.program_id(0)
    nb = pl.num_programs(0)
    nk = wg_ref.shape[0] // LANES
    rb = xbuf.shape[0] // (2 * nk)
    blk = rb * nk
    cur = i % 2
    nxt = 1 - cur

    def lines(first, count):
        return pl.ds(pl.multiple_of(first, nk), count)

    def for_rows(start_row):
        def body(g, carry):
            for u in range(DMA_UNROLL):
                start_row(g * DMA_UNROLL + u, u % 2)
            return carry
        lax.fori_loop(0, rb // DMA_UNROLL, body, 0)

    def start_gather(tok_ref, buf):
        for_rows(lambda r, pri: pltpu.make_async_copy(
            h_ref.at[lines(tok_ref[0, 0, r], nk)], xbuf.at[lines(buf * blk + r * nk, nk)],
            gsem.at[buf]).start(priority=pri))

    def wait_gather(buf):
        pltpu.make_async_copy(h_ref.at[pl.ds(0, blk)], xbuf.at[lines(buf * blk, blk)], gsem.at[buf]).wait()

    def wait_scatter(buf):
        pltpu.make_async_copy(obuf.at[lines(buf * blk, blk)], o_ref.at[pl.ds(0, blk)], ssem.at[buf]).wait()

    @pl.when(i == 0)
    def _():
        start_gather(tok0_ref, 0)

    @pl.when(i + 1 < nb)
    def _():
        start_gather(tokn_ref, nxt)

    wait_gather(cur)
    base = cur * blk
    xb = jnp.concatenate([xbuf[pl.ds(base + c, rb, stride=nk), :] for c in range(nk)], axis=1).astype(BF16)
    hid = jax.nn.silu(jnp.dot(xb, wg_ref[...], preferred_element_type=F32))
    hid = hid * jnp.dot(xb, wu_ref[...], preferred_element_type=F32)
    out = jnp.dot(hid.astype(BF16), wd_ref[...], preferred_element_type=F32)

    @pl.when(i >= 2)
    def _():
        wait_scatter(cur)

    for c in range(nk):
        obuf[pl.ds(base + c, rb, stride=nk), :] = out[:, c * LANES:(c + 1) * LANES]

    for_rows(lambda r, pri: pltpu.make_async_copy(
        obuf.at[lines(base + r * nk, nk)], o_ref.at[lines(slot_ref[0, 0, r], nk)], ssem.at[cur]).start(priority=pri))

    @pl.when(i == nb - 1)
    def _():
        wait_scatter(cur)

        @pl.when(nb >= 2)
        def _():
            wait_scatter(nxt)


def _experts(h2, src_tok, out_slot, blk_e, w_gate, w_up, w_down, rb):
    d, de = w_gate.shape[1:]
    nk = d // LANES
    rows = src_tok.shape[0]
    nb = rows // rb
    idx3 = lambda a: (a * nk).reshape(nb, 1, rb)
    ispec = lambda f: pl.BlockSpec((1, 1, rb), f, memory_space=pltpu.SMEM)
    wspec = lambda r, c: pl.BlockSpec((None, r, c), lambda i, be: (be[i], 0, 0))
    return pl.pallas_call(
        _expert_kernel,
        grid_spec=pltpu.PrefetchScalarGridSpec(
            num_scalar_prefetch=1,
            grid=(nb,),
            in_specs=[ispec(lambda i, be: (0, 0, 0)), ispec(lambda i, be: (jnp.minimum(i + 1, nb - 1), 0, 0)),
                      ispec(lambda i, be: (i, 0, 0)), pl.BlockSpec(memory_space=pl.ANY),
                      wspec(d, de), wspec(d, de), wspec(de, d)],
            out_specs=pl.BlockSpec(memory_space=pl.ANY),
            scratch_shapes=[pltpu.VMEM((2 * rb * nk, LANES), F32), pltpu.VMEM((2 * rb * nk, LANES), F32),
                            pltpu.SemaphoreType.DMA((2,)), pltpu.SemaphoreType.DMA((2,))],
        ),
        out_shape=jax.ShapeDtypeStruct((rows * nk, LANES), F32),
        compiler_params=_cparams(("arbitrary",)),
        name="experts",
    )(blk_e, idx3(src_tok), idx3(src_tok), idx3(out_slot), h2,
      w_gate.astype(BF16), w_up.astype(BF16), w_down.astype(BF16))


def _route_plan(expert, rb):
    n = expert.shape[0]
    a = n * TOP_E
    flat_e = expert.reshape(a)
    counts = jnp.sum((flat_e[:, None] == jnp.arange(N_EXPERTS, dtype=I32)[None, :]).astype(I32), axis=0)
    padded = (counts + rb - 1) // rb * rb
    ends = jnp.cumsum(padded)
    n_blocks = (a + N_EXPERTS * (rb - 1) + rb - 1) // rb
    rows = n_blocks * rb
    q = jnp.arange(rb - 1, dtype=I32)[None, :]
    e = jnp.arange(N_EXPERTS, dtype=I32)[:, None]
    pad_key = jnp.where(q < (padded - counts)[:, None], 2 * e + 1, 2 * N_EXPERTS).reshape(-1)
    spare = jnp.full((rows - a - N_EXPERTS * (rb - 1),), 2 * N_EXPERTS, I32)
    perm = jnp.argsort(jnp.concatenate([2 * flat_e, pad_key, spare]), stable=True).astype(I32)
    src_tok = jnp.where(perm < a, perm // TOP_E, 0)
    out_slot = perm
    blk_e = jnp.minimum(jnp.sum((ends[None, :] <= jnp.arange(n_blocks, dtype=I32)[:, None] * rb).astype(I32), axis=1),
                        N_EXPERTS - 1)
    return src_tok, out_slot, blk_e


def _combine_kernel(x1_ref, eo_ref, gt_ref, g_ref, y_ref):
    tm, d = x1_ref.shape
    nk = d // LANES
    gt = gt_ref[...]
    moe = _load_tile_major(eo_ref, tm, nk, 0, TOP_E * nk) * gt[:, 0:1]
    moe = moe + _load_tile_major(eo_ref, tm, nk, nk, TOP_E * nk) * gt[:, 1:2]
    x = x1_ref[...] + moe
    y_ref[...] = (x * lax.rsqrt(jnp.mean(jnp.square(x), -1, keepdims=True) + EPS)) * g_ref[...]


def _combine(x1, eo2, gate, g_final):
    n, d = x1.shape
    tm = _row_tile(n, 512)
    nk = d // LANES
    rspec = lambda w: pl.BlockSpec((tm, w), lambda i: (i, 0))
    return pl.pallas_call(
        _combine_kernel,
        grid=(n // tm,),
        in_specs=[rspec(d), pl.BlockSpec((tm * TOP_E * nk, LANES), lambda i: (i, 0)), rspec(TOP_E),
                  pl.BlockSpec((1, d), lambda i: (0, 0))],
        out_specs=rspec(d),
        out_shape=jax.ShapeDtypeStruct((n, d), F32),
        compiler_params=_cparams(("parallel",)),
        name="combine",
    )(x1, eo2, gate, g_final.reshape(1, d))


def _layer(x, past_len, s_ret, past_k, past_v, past_ki, band, params, g_final):
    (rel_bias, g_mix, w_in, g_ret, w_out, g_ffn, w_group, b_group, w_er, b_er, w_gate, w_up, w_down) = params
    b, t, d = x.shape
    n = b * t
    nk = d // LANES
    x2 = x.reshape(n, d)
    rq, rk, rv, rg, aq, ak, av, akb, avb, iq, ik, tail = _inproj(x2, g_mix, w_in)
    o_ret, s_new = _retention(rq, rk, rv, rg, s_ret, g_ret, b, t, past_len)
    aw = ATT_HEADS * ATT_HEAD_DIM
    keys_k = jnp.concatenate([past_k.reshape(b, past_len, aw).astype(BF16), akb.reshape(b, t, aw)], 1)
    keys_v = jnp.concatenate([past_v.reshape(b, past_len, aw).astype(BF16), avb.reshape(b, t, aw)], 1)
    keys_i = jnp.concatenate([past_ki.astype(BF16), ik.reshape(b, t, IDX_DIM).astype(BF16)], 1)
    o_att = _attention(aq, iq, tail, keys_k, keys_v, keys_i, band, b, t, past_len)
    x1, h2, expert, gate = _outproj_router(x2, o_ret, o_att, w_out, g_ffn, w_group, b_group, w_er, b_er)
    rb = 256 if n >= 8192 else 128
    src_tok, out_slot, blk_e = _route_plan(expert, rb)
    eo = _experts(h2, src_tok, out_slot, blk_e, w_gate, w_up, w_down, rb)
    y = _combine(x1, eo, gate, g_final)
    return (y.reshape(b, t, d), ak.reshape(1, b, t, ATT_HEADS, ATT_HEAD_DIM),
            av.reshape(1, b, t, ATT_HEADS, ATT_HEAD_DIM), ik.reshape(1, b, t, IDX_DIM), s_new[None])


def kernel(x_prompt, x_sample, cache_attn_k, cache_attn_v, cache_idx_k, state_ret, rel_bias, g_mix, w_in, g_ret, w_out, g_ffn, w_group, b_group, w_expert_router, b_expert_router, w_gate, w_up, w_down, g_final):
    assert g_mix.shape[0] == 1, "single-layer model"
    params = (rel_bias, g_mix[0], w_in[0], g_ret[0], w_out[0], g_ffn[0], w_group[0], b_group[0],
              w_expert_router[0], b_expert_router[0], w_gate[0], w_up[0], w_down[0])
    band = _bias_band(rel_bias)
    nb = x_prompt.shape[0]
    past_len = cache_attn_k.shape[2]
    dt = x_prompt.dtype
    empty_kv = jnp.zeros((nb, 0, ATT_HEADS, ATT_HEAD_DIM), dt)
    empty_ki = jnp.zeros((nb, 0, IDX_DIM), dt)
    s0 = jnp.zeros((nb, RET_HEADS, RET_DK, RET_DV), dt)
    yp, kp, vp, kip, sp = _layer(x_prompt, 0, s0, empty_kv, empty_kv, empty_ki, band, params, g_final)
    ys, ks, vs, kis, ss = _layer(x_sample, past_len, state_ret[0], cache_attn_k[0], cache_attn_v[0],
                                 cache_idx_k[0], band, params, g_final)
    return (yp, ys, kp, vp, kip, sp, ks, vs, kis, ss)
```

```python
import functools
import math

import jax
import jax.numpy as jnp
import numpy as np
from jax import lax
from jax.experimental import pallas as pl
from jax.experimental.pallas import tpu as pltpu

F32 = jnp.float32
BF16 = jnp.bfloat16
I32 = jnp.int32

CHUNK = 64
RET_HEADS = 4
RET_DK = 128
RET_DV = 128
ROPE_BASE = 10000.0
ATT_HEADS = 4
ATT_HEAD_DIM = 128
IDX_HEADS = 8
IDX_DIM = 64
TOPK_MAX = 256
NUM_BUCKETS = 32
MAX_DISTANCE = 128
N_GROUPS = 4
EXPERTS_PER_GROUP = 8
N_EXPERTS = N_GROUPS * EXPERTS_PER_GROUP
TOP_E = 2
EPS = 1e-6
NEG_INF = -1e30

LANES = 128
SUBLANES = 8
VMEM_LIMIT = 56 * 1024 * 1024
RET_CHUNK = 256
DMA_UNROLL = 8
INT_MIN = -(2 ** 31)
INT_MAX = 2 ** 31 - 1
BAND_TILES = 3
assert (BAND_TILES - 2) * LANES + 1 >= MAX_DISTANCE
assert math.log(IDX_DIM, 4).is_integer()


def _f32_key_const(v):
    b = int(np.array(v, np.float32).view(np.int32))
    return b ^ ((b >> 31) & 0x7FFFFFFF)


HALF_NEG_KEY = _f32_key_const(0.5 * NEG_INF)


def _cparams(sem):
    return pltpu.CompilerParams(dimension_semantics=sem, vmem_limit_bytes=VMEM_LIMIT)


def _row_tile(n, want=256):
    return want if n % want == 0 else n


def _inproj_kernel(x_ref, g_ref, wm_ref, wt_ref, wqt_ref, wiqt_ref, wtt_ref,
                   rq, rk, rv, rg, ak, av, akb, avb, ik, aqt, iqt, tailt):
    x = x_ref[...]
    h = (x * lax.rsqrt(jnp.mean(jnp.square(x), -1, keepdims=True) + EPS)) * g_ref[...]
    hb = h.astype(BF16)

    def proj(i):
        return jnp.dot(hb, wm_ref[:, i * 512:(i + 1) * 512], preferred_element_type=F32)

    def proj_t(wt_rows):
        return lax.dot_general(wt_rows, hb, (((1,), (1,)), ((), ())), preferred_element_type=F32)

    rq[...] = proj(0)
    rk[...] = proj(1)
    rv[...] = proj(2).astype(BF16)
    rg[...] = proj(3)
    aqt[...] = proj_t(wqt_ref[...]).astype(BF16)
    k = proj(5)
    _store_tile_major(ak, k)
    akb[...] = k.astype(BF16)
    v = proj(6)
    _store_tile_major(av, v)
    avb[...] = v.astype(BF16)
    iqt[...] = proj_t(wiqt_ref[...]).astype(BF16)
    ik[...] = jnp.dot(hb, wt_ref[...], preferred_element_type=F32)[:, :IDX_DIM]
    tailt[...] = proj_t(wtt_ref[...])


def _inproj(x2, g_mix, w_in):
    n, d = x2.shape
    tm = _row_tile(n)
    wm = w_in[:, :4096].astype(BF16)
    wt = jnp.pad(w_in[:, 4096:], ((0, 0), (0, LANES - (w_in.shape[1] - 4096)))).astype(BF16)
    rspec = lambda w, lines=1: pl.BlockSpec((tm * lines, w), lambda i: (i, 0))
    outs = [(512, F32, 1), (512, F32, 1), (512, BF16, 1), (512, F32, 1),
            (LANES, F32, ATT_HEADS), (LANES, F32, ATT_HEADS),
            (512, BF16, 1), (512, BF16, 1), (IDX_DIM, F32, 1)]
    outs_t = [(512, BF16), (512, BF16), (LANES, F32)]
    whole = lambda a: pl.BlockSpec(a.shape, lambda i: (0, 0))
    wqt, wiqt, wtt = wm[:, 4 * 512:5 * 512].T, wm[:, 7 * 512:8 * 512].T, wt.T
    return pl.pallas_call(
        _inproj_kernel,
        grid=(n // tm,),
        in_specs=[rspec(d), pl.BlockSpec((1, d), lambda i: (0, 0)), whole(wm), whole(wt),
                  whole(wqt), whole(wiqt), whole(wtt)],
        out_specs=[rspec(w, ln) for w, _, ln in outs] + [pl.BlockSpec((w, tm), lambda i: (0, i)) for w, _ in outs_t],
        out_shape=[jax.ShapeDtypeStruct((n * ln, w), dt) for w, dt, ln in outs]
        + [jax.ShapeDtypeStruct((w, n), dt) for w, dt in outs_t],
        compiler_params=_cparams(("parallel",)),
        name="inproj",
    )(x2, g_mix.reshape(1, d), wm, wt, wqt, wiqt, wtt)


def _retention_kernel(cd_ref, rq_ref, rk_ref, rv_ref, rg_ref, cos_ref, sin_ref, dmat_ref, qd_ref, kd_ref,
                      gret_ref, s0_ref, o_ref, s_ref):
    c = pl.program_id(1)

    @pl.when(c == 0)
    def _():
        s_ref[...] = s0_ref[...]

    cosf = cos_ref[...]
    sinf = sin_ref[...]
    half = RET_DK // 2

    def rot(x):
        return x * cosf + pltpu.roll(x, half, 1) * sinf

    for h in range(RET_HEADS):
        sl = slice(h * RET_DK, (h + 1) * RET_DK)
        q = rot(rq_ref[:, sl])
        k = rot(rk_ref[:, sl]) * (RET_DK ** -0.5)
        v = rv_ref[:, sl]
        s = s_ref[0, h]
        sc = lax.dot_general(q.astype(BF16), k.astype(BF16), (((1,), (1,)), ((), ())),
                             preferred_element_type=F32) * dmat_ref[h]
        o = jnp.dot(sc.astype(BF16), v, preferred_element_type=F32)
        o = o + jnp.dot((q * qd_ref[:, sl]).astype(BF16), s.astype(BF16), preferred_element_type=F32)
        kdt = jnp.transpose(k * kd_ref[:, sl]).astype(BF16)
        s_ref[0, h] = cd_ref[h] * s + jnp.dot(kdt, v, preferred_element_type=F32)
        mu = jnp.mean(o, -1, keepdims=True)
        var = jnp.mean(jnp.square(o - mu), -1, keepdims=True)
        on = (o - mu) * lax.rsqrt(var + EPS) * gret_ref[:, sl]
        o_ref[:, sl] = (jax.nn.silu(rg_ref[:, sl]) * on).astype(o_ref.dtype)


def _retention(rq, rk, rv, rg, s0, g_ret, b, t, past_len):
    cl = min(RET_CHUNK, t)
    nc = t // cl
    half = RET_DK // 2
    pos = (past_len + jnp.arange(t)).astype(F32)
    inv = ROPE_BASE ** (-jnp.arange(half, dtype=F32) / half)
    ang = pos[:, None] * inv[None, :]
    cosf = jnp.concatenate([jnp.cos(ang), jnp.cos(ang)], -1)
    sinf = jnp.concatenate([-jnp.sin(ang), jnp.sin(ang)], -1)
    log_g = jnp.log1p(-jnp.exp2(-5.0 - jnp.arange(RET_HEADS, dtype=F32)))
    i = jnp.arange(cl, dtype=F32)
    diff = i[:, None] - i[None, :]
    dmat = jnp.where(diff[None] >= 0, jnp.exp(jnp.maximum(diff, 0.0)[None] * log_g[:, None, None]), 0.0)
    kd = jnp.repeat(jnp.exp((cl - 1.0 - i)[:, None] * log_g[None, :]), RET_DK, axis=1)
    qd = jnp.repeat(jnp.exp((i + 1.0)[:, None] * log_g[None, :]), RET_DK, axis=1)
    cd = jnp.exp(cl * log_g)
    w = RET_HEADS * RET_DK
    rspec = pl.BlockSpec((cl, w), lambda bi, ci: (bi * nc + ci, 0))
    cspec = lambda shape: pl.BlockSpec(shape, lambda bi, ci: (0,) * len(shape))
    sspec = pl.BlockSpec((1, RET_HEADS, RET_DK, RET_DV), lambda bi, ci: (bi, 0, 0, 0))
    return pl.pallas_call(
        _retention_kernel,
        grid=(b, nc),
        in_specs=[pl.BlockSpec(memory_space=pltpu.SMEM), rspec, rspec, rspec, rspec,
                  pl.BlockSpec((cl, RET_DK), lambda bi, ci: (ci, 0)),
                  pl.BlockSpec((cl, RET_DK), lambda bi, ci: (ci, 0)),
                  cspec((RET_HEADS, cl, cl)), cspec((cl, w)), cspec((cl, w)), cspec((1, w)), sspec],
        out_specs=[rspec, sspec],
        out_shape=[jax.ShapeDtypeStruct((b * t, w), BF16),
                   jax.ShapeDtypeStruct((b, RET_HEADS, RET_DK, RET_DV), F32)],
        compiler_params=_cparams(("parallel", "arbitrary")),
        name="retention",
    )(cd, rq, rk, rv, rg, cosf, sinf, dmat, qd, kd, g_ret.reshape(1, w), s0)


def _t5_bucket(rel):
    nb = NUM_BUCKETS // 2
    max_exact = nb // 2
    base = jnp.where(rel > 0, nb, 0)
    n = jnp.abs(rel)
    nf = jnp.maximum(n, 1).astype(F32)
    large = max_exact + (jnp.log(nf / max_exact) / math.log(MAX_DISTANCE / max_exact) * (nb - max_exact)).astype(I32)
    large = jnp.minimum(large, nb - 1)
    return base + jnp.where(n < max_exact, n, large)


def _band_kernel(rb_ref, bucket_ref, band_ref):
    bucket = bucket_ref[...]
    for h in range(ATT_HEADS):
        acc = jnp.zeros(bucket.shape, F32)
        for j in range(NUM_BUCKETS):
            acc = jnp.where(bucket == j, rb_ref[j, h], acc)
        band_ref[h] = acc


def _bias_band(rel_bias):
    c = jnp.arange(BAND_TILES * LANES, dtype=I32)[:, None]
    t = jnp.arange(LANES, dtype=I32)[None, :]
    bucket = _t5_bucket(c - (BAND_TILES - 1) * LANES - t)
    return pl.pallas_call(
        _band_kernel,
        in_specs=[pl.BlockSpec(memory_space=pltpu.SMEM), pl.BlockSpec(memory_space=pltpu.VMEM)],
        out_specs=pl.BlockSpec(memory_space=pltpu.VMEM),
        out_shape=jax.ShapeDtypeStruct((ATT_HEADS, BAND_TILES * LANES, LANES), F32),
        name="bias_band",
    )(rel_bias, bucket)


def _order_key(s):
    bits = lax.bitcast_convert_type(s, I32)
    return bits ^ ((bits >> 31) & 0x7FFFFFFF)


def _fold8(x, op=jnp.add):
    parts = [x[i * SUBLANES:(i + 1) * SUBLANES] for i in range(x.shape[0] // SUBLANES)]
    while len(parts) > 1:
        parts = [op(parts[i], parts[i + 1]) for i in range(0, len(parts), 2)]
    return parts[0]


GROUP = LANES // SUBLANES


def _sort_network(n):
    pairs = []
    p = 1
    while p < n:
        k = p
        while k >= 1:
            for j in range(k % p, n - k, 2 * k):
                for i in range(min(k, n - j - k)):
                    if (i + j) // (2 * p) == (i + j + k) // (2 * p):
                        pairs.append((i + j, i + j + k))
            k //= 2
        p *= 2
    return pairs


def _sort_desc(vals):
    vals = list(vals)
    for a, b in _sort_network(len(vals)):
        vals[a], vals[b] = jnp.maximum(vals[a], vals[b]), jnp.minimum(vals[a], vals[b])
    return vals


def _tile_loop(nt, trip, carry):
    def run(first, trips, width, carry):
        return lax.fori_loop(0, trips, lambda i, c: trip([first + i * width + u for u in range(width)], c), carry)
    carry = run(0, nt // 4, 4, carry)
    carry = run(nt // 4 * 4, (nt // 2) % 2, 2, carry)
    return run(nt // 2 * 2, nt % 2, 1, carry)


def _attention_kernel(q_ref, iq_ref, tail_ref, k_ref, vt_ref, ki_ref, band_ref, adm_ref, o_ref,
                      iqs_s, key_s, ks_s, mask_s, log_s, acc_s, *, jd0, n_sel, idx_bits):
    tq = LANES
    jd = jd0 + pl.program_id(1)
    nt = jd + 1
    krow = lax.broadcasted_iota(I32, (LANES, tq), 0)

    qt = q_ref[...]
    iqt = iq_ref[...]
    wt = tail_ref[IDX_DIM:IDX_DIM + IDX_HEADS, :] * (IDX_HEADS ** -0.5) * (IDX_DIM ** -0.5)
    zpad = jnp.zeros((LANES - IDX_DIM, tq), BF16)
    for h in range(IDX_HEADS):
        iqs_s[:, h * tq:(h + 1) * tq] = jnp.concatenate([iqt[h * IDX_DIM:(h + 1) * IDX_DIM], zpad], axis=0)

    def tile_rows(j):
        return pl.ds(pl.multiple_of(j * LANES, LANES), LANES)

    def score_tile(j):
        d = jnp.dot(ki_ref[tile_rows(j), :], iqs_s[...], preferred_element_type=F32)
        acc = jnp.zeros((LANES, tq), F32)
        for h in range(IDX_HEADS):
            acc = acc + wt[h:h + 1] * jnp.maximum(d[:, h * tq:(h + 1) * tq], 0.0)
        return acc

    def score_trip(js, carry):
        for j in js:
            adm = jnp.where(j == jd, adm_ref[...], 1.0)
            key = jnp.where(adm > 0.0, _order_key(score_tile(j)), _f32_key_const(NEG_INF))
            key_s[j] = key
            srt = _sort_desc([key[g * SUBLANES:(g + 1) * SUBLANES] for g in range(GROUP)])
            for g in range(GROUP):
                ks_s[j, g * SUBLANES:(g + 1) * SUBLANES] = srt[g]
        return carry

    _tile_loop(nt, score_trip, 0)
    nquad = (nt + 3) // 4

    def fill_body(j, carry):
        ks_s[j] = jnp.full((LANES, tq), INT_MIN, I32)
        return carry

    lax.fori_loop(nt, nquad * 4, fill_body, 0)

    def count(pred_tile):
        def trip(js, acc):
            for j in js:
                acc = acc + _fold8(pred_tile(j))
            return acc
        return jnp.sum(_tile_loop(nt, trip, jnp.zeros((SUBLANES, tq), F32)), axis=0, keepdims=True)

    def count_ge(cand):
        def one(m):
            return jnp.where(m, 1.0, 0.0)

        def trip(q, accs):
            a8, a4, a2, a1 = accs
            for u in range(4):
                v = [ks_s[4 * q + u, g * SUBLANES:(g + 1) * SUBLANES] for g in range(GROUP)]
                t1 = v[7] >= cand
                t2 = jnp.where(t1, v[11], v[3]) >= cand
                t3 = jnp.where(t1, jnp.where(t2, v[13], v[9]), jnp.where(t2, v[5], v[1])) >= cand
                t4 = jnp.where(t1, jnp.where(t2, jnp.where(t3, v[14], v[12]), jnp.where(t3, v[10], v[8])),
                               jnp.where(t2, jnp.where(t3, v[6], v[4]), jnp.where(t3, v[2], v[0]))) >= cand
                a8, a4, a2 = a8 + one(t1), a4 + one(t2), a2 + one(t3)
                a1 = a1 + one(t4) + one(v[15] >= cand)
            return a8, a4, a2, a1

        zero = jnp.zeros((SUBLANES, tq), F32)
        a8, a4, a2, a1 = lax.fori_loop(0, nquad, trip, (zero, zero, zero, zero))
        return jnp.sum(8.0 * a8 + 4.0 * a4 + 2.0 * a2 + a1, axis=0, keepdims=True)

    kf = float(n_sel)

    def bit_body(it, thr):
        cand = thr + lax.shift_left(jnp.int32(1), 31 - it)
        return jnp.where(count_ge(cand) >= kf, cand, thr)

    thr = lax.fori_loop(0, 32, bit_body, jnp.full((1, tq), INT_MIN, I32))

    need = kf - count(lambda j: jnp.where(key_s[j] > thr, 1.0, 0.0))
    n_tied = count(lambda j: jnp.where(key_s[j] == thr, 1.0, 0.0))
    surplus = jnp.where(thr > HALF_NEG_KEY, n_tied - need, 0.0)

    def tie_search():
        def tie_body(it, j0):
            cand = j0 + lax.shift_left(jnp.int32(1), idx_bits - 1 - it)
            cnt = count(lambda j: jnp.where(key_s[j] == thr, jnp.where(krow + j * LANES < cand, 1.0, 0.0), 0.0))
            return jnp.where(cnt < need, cand, j0)
        return lax.fori_loop(0, idx_bits, tie_body, jnp.zeros((1, tq), I32))

    j0 = lax.cond(jnp.max(surplus) > 0.0, tie_search, lambda: jnp.full((1, tq), INT_MAX, I32))

    def mask_trip(js, carry):
        for j in js:
            key = key_s[j]
            sel = jnp.where(key > thr, 1.0, jnp.where(key == thr, jnp.where(krow + j * LANES <= j0, 1.0, 0.0), 0.0))
            sel = jnp.where(key > HALF_NEG_KEY, sel, 0.0)
            mask_s[j] = jnp.where(sel > 0.0, 0.0, NEG_INF)
        return carry

    _tile_loop(nt, mask_trip, 0)

    scale = ATT_HEAD_DIM ** -0.5
    heads = range(ATT_HEADS)
    qh = [qt[h * ATT_HEAD_DIM:(h + 1) * ATT_HEAD_DIM] for h in heads]

    def log_trip(js, mx):
        mx = list(mx)
        for j in js:
            band_rows = tile_rows(jnp.clip(j - jd + BAND_TILES - 1, 0, BAND_TILES - 1))
            msk = mask_s[j]
            for h in heads:
                kh = k_ref[tile_rows(j), h * ATT_HEAD_DIM:(h + 1) * ATT_HEAD_DIM]
                lg = jnp.dot(kh, qh[h], preferred_element_type=F32) * scale + band_ref[h, band_rows, :] + msk
                log_s[h, j] = lg
                mx[h] = jnp.maximum(mx[h], _fold8(lg, jnp.maximum))
        return tuple(mx)

    mx = _tile_loop(nt, log_trip, tuple(jnp.full((SUBLANES, tq), NEG_INF, F32) for _ in heads))
    m = [jnp.max(mx[h], axis=0, keepdims=True) for h in heads]
    acc_s[...] = jnp.zeros(acc_s.shape, F32)

    def pv_trip(js, ls):
        ls = list(ls)
        for h in heads:
            acc = acc_s[h]
            for j in js:
                p = jnp.exp(log_s[h, j] - m[h])
                ls[h] = ls[h] + _fold8(p)
                vth = vt_ref[j, h * ATT_HEAD_DIM:(h + 1) * ATT_HEAD_DIM, :]
                acc = acc + jnp.dot(vth, p.astype(BF16), preferred_element_type=F32)
            acc_s[h] = acc
        return tuple(ls)

    ls = _tile_loop(nt, pv_trip, tuple(jnp.zeros((SUBLANES, tq), F32) for _ in heads))
    for h in heads:
        ot = acc_s[h] / jnp.sum(ls[h], axis=0, keepdims=True)
        o_ref[:, h * ATT_HEAD_DIM:(h + 1) * ATT_HEAD_DIM] = jnp.transpose(ot).astype(o_ref.dtype)


def _attention(aqt, iqt, tailt, keys_k, keys_v, keys_i, band, b, t, past_len):
    l = past_len + t
    tq = LANES
    nqb = -(-t // tq)
    tp = nqb * tq
    assert past_len % LANES == 0 and (t % tq == 0 or nqb == 1)
    jd0 = past_len // LANES
    ntiles = jd0 + nqb
    ntp = ntiles
    lp = ntp * LANES
    n_sel = min(TOPK_MAX, l // 4)
    if tp != t:
        padq = lambda a: jnp.pad(a.reshape(-1, b, t), ((0, 0), (0, 0), (0, tp - t))).reshape(-1, b * tp)
        aqt, iqt, tailt = padq(aqt), padq(iqt), padq(tailt)
    w = ATT_HEADS * ATT_HEAD_DIM
    pad = ((0, 0), (0, lp - l), (0, 0))
    kk = jnp.pad(keys_k, pad)
    vt = jnp.pad(keys_v, pad).reshape(b, ntp, LANES, w).transpose(0, 1, 3, 2)
    ki2 = jnp.pad(keys_i, ((0, 0), (0, lp - l), (0, LANES - IDX_DIM)))
    cc = jnp.arange(LANES)[:, None]
    tt = jnp.arange(tq)[None, :]
    adm = ((cc // CHUNK <= tt // CHUNK) & (cc < l - (ntiles - 1) * LANES)).astype(F32)
    qspec = lambda width: pl.BlockSpec((tq, width), lambda bi, qi: (bi * nqb + qi, 0))
    qtspec = lambda rows: pl.BlockSpec((rows, tq), lambda bi, qi: (0, bi * nqb + qi))
    kspec = lambda width: pl.BlockSpec((None, lp, width), lambda bi, qi: (bi, 0, 0))
    kern = functools.partial(_attention_kernel, jd0=jd0, n_sel=n_sel, idx_bits=max((lp - 1).bit_length(), 1))
    out = pl.pallas_call(
        kern,
        grid=(b, nqb),
        in_specs=[qtspec(w), qtspec(IDX_HEADS * IDX_DIM), qtspec(LANES),
                  kspec(w), pl.BlockSpec((None, ntp, w, LANES), lambda bi, qi: (bi, 0, 0, 0)), kspec(LANES),
                  pl.BlockSpec((ATT_HEADS, BAND_TILES * LANES, tq), lambda bi, qi: (0, 0, 0)),
                  pl.BlockSpec((LANES, tq), lambda bi, qi: (0, 0))],
        out_specs=qspec(w),
        out_shape=jax.ShapeDtypeStruct((b * tp, w), BF16),
        scratch_shapes=[pltpu.VMEM((LANES, IDX_HEADS * tq), BF16),
                        pltpu.VMEM((ntp, LANES, tq), I32), pltpu.VMEM((-(-ntp // 4) * 4, LANES, tq), I32),
                        pltpu.VMEM((ntp, LANES, tq), F32),
                        pltpu.VMEM((ATT_HEADS, ntp, LANES, tq), F32), pltpu.VMEM((ATT_HEADS, ATT_HEAD_DIM, tq), F32)],
        compiler_params=_cparams(("parallel", "arbitrary")),
        name="attention",
    )(aqt, iqt, tailt, kk, vt, ki2, band, adm)
    if tp != t:
        out = out.reshape(b, tp, w)[:, :t].reshape(b * t, w)
    return out


def _store_tile_major(ref, x):
    rows, width = x.shape
    nk = width // LANES
    for c in range(nk):
        ref[pl.ds(c, rows, stride=nk), :] = x[:, c * LANES:(c + 1) * LANES]


def _load_tile_major(ref, rows, nk, first=0, stride=None):
    stride = nk if stride is None else stride
    return jnp.concatenate([ref[pl.ds(first + c, rows, stride=stride), :] for c in range(nk)], axis=1)


def _outproj_kernel(x_ref, oret_ref, oatt_ref, wo_ref, g_ref, wr_ref, br_ref, x1_ref, h2_ref, e_ref, gt_ref):
    nr = oret_ref.shape[1]
    mix = jnp.dot(oret_ref[...], wo_ref[:nr, :], preferred_element_type=F32)
    mix = mix + jnp.dot(oatt_ref[...], wo_ref[nr:, :], preferred_element_type=F32)
    x1 = x_ref[...] + mix
    x1_ref[...] = x1
    h2 = (x1 * lax.rsqrt(jnp.mean(jnp.square(x1), -1, keepdims=True) + EPS)) * g_ref[...]
    _store_tile_major(h2_ref, h2)
    lg = jnp.dot(h2.astype(BF16), wr_ref[...], preferred_element_type=F32) + br_ref[...]
    tm = lg.shape[0]
    gl = lg[:, :N_GROUPS]
    gmax = jnp.max(gl, -1, keepdims=True)
    p_top = 1.0 / jnp.sum(jnp.exp(gl - gmax), -1, keepdims=True)
    gi = lax.broadcasted_iota(I32, (tm, N_GROUPS), 1).astype(F32)
    g_top = jnp.min(jnp.where(gl == gmax, gi, float(N_GROUPS)), -1, keepdims=True)
    el = jnp.zeros((tm, EXPERTS_PER_GROUP), F32)
    for g in range(N_GROUPS):
        lo = N_GROUPS + g * EXPERTS_PER_GROUP
        el = jnp.where(g_top == float(g), lg[:, lo:lo + EXPERTS_PER_GROUP], el)
    ei = lax.broadcasted_iota(I32, (tm, EXPERTS_PER_GROUP), 1).astype(F32)
    v1 = jnp.max(el, -1, keepdims=True)
    i1 = jnp.min(jnp.where(el == v1, ei, float(EXPERTS_PER_GROUP)), -1, keepdims=True)
    el2 = jnp.where(ei == i1, -jnp.inf, el)
    v2 = jnp.max(el2, -1, keepdims=True)
    i2 = jnp.min(jnp.where(el2 == v2, ei, float(EXPERTS_PER_GROUP)), -1, keepdims=True)
    e2 = jnp.exp(v2 - v1)
    den = 1.0 + e2
    two = lax.broadcasted_iota(I32, (tm, TOP_E), 1)
    e_ref[...] = (g_top * EXPERTS_PER_GROUP + jnp.where(two == 0, i1, i2)).astype(I32)
    gt_ref[...] = jnp.where(two == 0, 1.0 / den, e2 / den) * p_top


def _outproj_router(x2, o_ret, o_att, w_out, g_ffn, w_group, b_group, w_er, b_er):
    n, d = x2.shape
    tm = _row_tile(n, 512)
    nk = d // LANES
    nrt = N_GROUPS + N_EXPERTS
    wr = jnp.pad(jnp.concatenate([w_group, w_er.reshape(d, N_EXPERTS)], 1), ((0, 0), (0, LANES - nrt))).astype(BF16)
    br = jnp.pad(jnp.concatenate([b_group, b_er.reshape(N_EXPERTS)]), (0, LANES - nrt)).reshape(1, LANES)
    rspec = lambda w: pl.BlockSpec((tm, w), lambda i: (i, 0))
    cspec = lambda r, c: pl.BlockSpec((r, c), lambda i: (0, 0))
    mw = w_out.shape[0]
    return pl.pallas_call(
        _outproj_kernel,
        grid=(n // tm,),
        in_specs=[rspec(d), rspec(o_ret.shape[1]), rspec(o_att.shape[1]), cspec(mw, d), cspec(1, d),
                  cspec(d, LANES), cspec(1, LANES)],
        out_specs=[rspec(d), pl.BlockSpec((tm * nk, LANES), lambda i: (i, 0)), rspec(TOP_E), rspec(TOP_E)],
        out_shape=[jax.ShapeDtypeStruct((n, d), F32), jax.ShapeDtypeStruct((n * nk, LANES), F32),
                   jax.ShapeDtypeStruct((n, TOP_E), I32), jax.ShapeDtypeStruct((n, TOP_E), F32)],
        compiler_params=_cparams(("parallel",)),
        name="outproj_router",
    )(x2, o_ret, o_att, w_out.astype(BF16), g_ffn.reshape(1, d), wr, br)


def _expert_kernel(blk_e_ref, tok0_ref, tokn_ref, slot_ref, h_ref, wg_ref, wu_ref, wd_ref, o_ref,
                   xbuf, obuf, gsem, ssem):
    i = pl.program_id(0)
    nb = pl.num_programs(0)
    nk = wg_ref.shape[0] // LANES
    rb = xbuf.shape[0] // (2 * nk)
    blk = rb * nk
    cur = i % 2
    nxt = 1 - cur

    def lines(first, count):
        return pl.ds(pl.multiple_of(first, nk), count)

    def for_rows(start_row):
        def body(g, carry):
            for u in range(DMA_UNROLL):
                start_row(g * DMA_UNROLL + u, u % 2)
            return carry
        lax.fori_loop(0, rb // DMA_UNROLL, body, 0)

    def start_gather(tok_ref, buf):
        for_rows(lambda r, pri: pltpu.make_async_copy(
            h_ref.at[lines(tok_ref[0, 0, r], nk)], xbuf.at[lines(buf * blk + r * nk, nk)],
            gsem.at[buf]).start(priority=pri))

    def wait_gather(buf):
        pltpu.make_async_copy(h_ref.at[pl.ds(0, blk)], xbuf.at[lines(buf * blk, blk)], gsem.at[buf]).wait()

    def wait_scatter(buf):
        pltpu.make_async_copy(obuf.at[lines(buf * blk, blk)], o_ref.at[pl.ds(0, blk)], ssem.at[buf]).wait()

    @pl.when(i == 0)
    def _():
        start_gather(tok0_ref, 0)

    @pl.when(i + 1 < nb)
    def _():
        start_gather(tokn_ref, nxt)

    wait_gather(cur)
    base = cur * blk
    xb = jnp.concatenate([xbuf[pl.ds(base + c, rb, stride=nk), :] for c in range(nk)], axis=1).astype(BF16)
    hid = jax.nn.silu(jnp.dot(xb, wg_ref[...].astype(BF16), preferred_element_type=F32))
    hid = hid * jnp.dot(xb, wu_ref[...].astype(BF16), preferred_element_type=F32)
    out = jnp.dot(hid.astype(BF16), wd_ref[...].astype(BF16), preferred_element_type=F32)

    @pl.when(i >= 2)
    def _():
        wait_scatter(cur)

    for c in range(nk):
        obuf[pl.ds(base + c, rb, stride=nk), :] = out[:, c * LANES:(c + 1) * LANES]

    for_rows(lambda r, pri: pltpu.make_async_copy(
        obuf.at[lines(base + r * nk, nk)], o_ref.at[lines(slot_ref[0, 0, r], nk)], ssem.at[cur]).start(priority=pri))

    @pl.when(i == nb - 1)
    def _():
        wait_scatter(cur)

        @pl.when(nb >= 2)
        def _():
            wait_scatter(nxt)


def _experts(h2, src_tok, out_slot, blk_e, w_gate, w_up, w_down, rb):
    d, de = w_gate.shape[1:]
    nk = d // LANES
    rows = src_tok.shape[0]
    nb = rows // rb
    idx3 = lambda a: (a * nk).reshape(nb, 1, rb)
    ispec = lambda f: pl.BlockSpec((1, 1, rb), f, memory_space=pltpu.SMEM)
    wspec = lambda r, c: pl.BlockSpec((None, r, c), lambda i, be: (be[i], 0, 0))
    return pl.pallas_call(
        _expert_kernel,
        grid_spec=pltpu.PrefetchScalarGridSpec(
            num_scalar_prefetch=1,
            grid=(nb,),
            in_specs=[ispec(lambda i, be: (0, 0, 0)), ispec(lambda i, be: (jnp.minimum(i + 1, nb - 1), 0, 0)),
                      ispec(lambda i, be: (i, 0, 0)), pl.BlockSpec(memory_space=pl.ANY),
                      wspec(d, de), wspec(d, de), wspec(de, d)],
            out_specs=pl.BlockSpec(memory_space=pl.ANY),
            scratch_shapes=[pltpu.VMEM((2 * rb * nk, LANES), F32), pltpu.VMEM((2 * rb * nk, LANES), F32),
                            pltpu.SemaphoreType.DMA((2,)), pltpu.SemaphoreType.DMA((2,))],
        ),
        out_shape=jax.ShapeDtypeStruct((rows * nk, LANES), F32),
        compiler_params=_cparams(("arbitrary",)),
        name="experts",
    )(blk_e, idx3(src_tok), idx3(src_tok), idx3(out_slot), h2, w_gate, w_up, w_down)


def _route_plan(expert, rb):
    n = expert.shape[0]
    a = n * TOP_E
    flat_e = expert.reshape(a)
    counts = jnp.sum((flat_e[:, None] == jnp.arange(N_EXPERTS, dtype=I32)[None, :]).astype(I32), axis=0)
    padded = (counts + rb - 1) // rb * rb
    ends = jnp.cumsum(padded)
    n_blocks = (a + N_EXPERTS * (rb - 1) + rb - 1) // rb
    rows = n_blocks * rb
    q = jnp.arange(rb - 1, dtype=I32)[None, :]
    e = jnp.arange(N_EXPERTS, dtype=I32)[:, None]
    pad_key = jnp.where(q < (padded - counts)[:, None], 2 * e + 1, 2 * N_EXPERTS).reshape(-1)
    spare = jnp.full((rows - a - N_EXPERTS * (rb - 1),), 2 * N_EXPERTS, I32)
    perm = jnp.argsort(jnp.concatenate([2 * flat_e, pad_key, spare]), stable=True).astype(I32)
    src_tok = jnp.where(perm < a, perm // TOP_E, 0)
    out_slot = perm
    blk_e = jnp.minimum(jnp.sum((ends[None, :] <= jnp.arange(n_blocks, dtype=I32)[:, None] * rb).astype(I32), axis=1),
                        N_EXPERTS - 1)
    return src_tok, out_slot, blk_e


def _combine_kernel(x1_ref, eo_ref, gt_ref, g_ref, y_ref):
    tm, d = x1_ref.shape
    nk = d // LANES
    gt = gt_ref[...]
    moe = _load_tile_major(eo_ref, tm, nk, 0, TOP_E * nk) * gt[:, 0:1]
    moe = moe + _load_tile_major(eo_ref, tm, nk, nk, TOP_E * nk) * gt[:, 1:2]
    x = x1_ref[...] + moe
    y_ref[...] = (x * lax.rsqrt(jnp.mean(jnp.square(x), -1, keepdims=True) + EPS)) * g_ref[...]


def _combine(x1, eo2, gate, g_final):
    n, d = x1.shape
    tm = _row_tile(n, 512)
    nk = d // LANES
    rspec = lambda w: pl.BlockSpec((tm, w), lambda i: (i, 0))
    return pl.pallas_call(
        _combine_kernel,
        grid=(n // tm,),
        in_specs=[rspec(d), pl.BlockSpec((tm * TOP_E * nk, LANES), lambda i: (i, 0)), rspec(TOP_E),
                  pl.BlockSpec((1, d), lambda i: (0, 0))],
        out_specs=rspec(d),
        out_shape=jax.ShapeDtypeStruct((n, d), F32),
        compiler_params=_cparams(("parallel",)),
        name="combine",
    )(x1, eo2, gate, g_final.reshape(1, d))


def _layer(x, past_len, s_ret, past_k, past_v, past_ki, band, params, g_final):
    (rel_bias, g_mix, w_in, g_ret, w_out, g_ffn, w_group, b_group, w_er, b_er, w_gate, w_up, w_down) = params
    b, t, d = x.shape
    n = b * t
    nk = d // LANES
    x2 = x.reshape(n, d)
    rq, rk, rv, rg, ak, av, akb, avb, ik, aqt, iqt, tailt = _inproj(x2, g_mix, w_in)
    o_ret, s_new = _retention(rq, rk, rv, rg, s_ret, g_ret, b, t, past_len)
    aw = ATT_HEADS * ATT_HEAD_DIM
    keys_k = jnp.concatenate([past_k.reshape(b, past_len, aw).astype(BF16), akb.reshape(b, t, aw)], 1)
    keys_v = jnp.concatenate([past_v.reshape(b, past_len, aw).astype(BF16), avb.reshape(b, t, aw)], 1)
    keys_i = jnp.concatenate([past_ki.astype(BF16), ik.reshape(b, t, IDX_DIM).astype(BF16)], 1)
    o_att = _attention(aqt, iqt, tailt, keys_k, keys_v, keys_i, band, b, t, past_len)
    x1, h2, expert, gate = _outproj_router(x2, o_ret, o_att, w_out, g_ffn, w_group, b_group, w_er, b_er)
    rb = 256 if n >= 8192 else 128
    src_tok, out_slot, blk_e = _route_plan(expert, rb)
    eo = _experts(h2, src_tok, out_slot, blk_e, w_gate, w_up, w_down, rb)
    y = _combine(x1, eo, gate, g_final)
    return (y.reshape(b, t, d), ak.reshape(1, b, t, ATT_HEADS, ATT_HEAD_DIM),
            av.reshape(1, b, t, ATT_HEADS, ATT_HEAD_DIM), ik.reshape(1, b, t, IDX_DIM), s_new[None])


def kernel(x_prompt, x_sample, cache_attn_k, cache_attn_v, cache_idx_k, state_ret, rel_bias, g_mix, w_in, g_ret, w_out, g_ffn, w_group, b_group, w_expert_router, b_expert_router, w_gate, w_up, w_down, g_final):
    assert g_mix.shape[0] == 1, "single-layer model"
    params = (rel_bias, g_mix[0], w_in[0], g_ret[0], w_out[0], g_ffn[0], w_group[0], b_group[0],
              w_expert_router[0], b_expert_router[0], w_gate[0], w_up[0], w_down[0])
    band = _bias_band(rel_bias)
    nb = x_prompt.shape[0]
    past_len = cache_attn_k.shape[2]
    dt = x_prompt.dtype
    empty_kv = jnp.zeros((nb, 0, ATT_HEADS, ATT_HEAD_DIM), dt)
    empty_ki = jnp.zeros((nb, 0, IDX_DIM), dt)
    s0 = jnp.zeros((nb, RET_HEADS, RET_DK, RET_DV), dt)
    yp, kp, vp, kip, sp = _layer(x_prompt, 0, s0, empty_kv, empty_kv, empty_ki, band, params, g_final)
    ys, ks, vs, kis, ss = _layer(x_sample, past_len, state_ret[0], cache_attn_k[0], cache_attn_v[0],
                                 cache_idx_k[0], band, params, g_final)
    return (yp, ys, kp, vp, kip, sp, ks, vs, kis, ss)
```

```python
import functools
import math

import jax
import jax.numpy as jnp
import numpy as np
from jax import lax
from jax.experimental import pallas as pl
from jax.experimental.pallas import tpu as pltpu

F32 = jnp.float32
BF16 = jnp.bfloat16
I32 = jnp.int32

CHUNK = 64
RET_HEADS = 4
RET_DK = 128
RET_DV = 128
ROPE_BASE = 10000.0
ATT_HEADS = 4
ATT_HEAD_DIM = 128
IDX_HEADS = 8
IDX_DIM = 64
TOPK_MAX = 256
NUM_BUCKETS = 32
MAX_DISTANCE = 128
N_GROUPS = 4
EXPERTS_PER_GROUP = 8
N_EXPERTS = N_GROUPS * EXPERTS_PER_GROUP
TOP_E = 2
EPS = 1e-6
NEG_INF = -1e30

LANES = 128
SUBLANES = 8
VMEM_LIMIT = 56 * 1024 * 1024
RET_CHUNK = 256
DMA_UNROLL = 8
INT_MIN = -(2 ** 31)
INT_MAX = 2 ** 31 - 1
BAND_TILES = 3
assert (BAND_TILES - 2) * LANES + 1 >= MAX_DISTANCE
assert math.log(IDX_DIM, 4).is_integer()


def _f32_key_const(v):
    b = int(np.array(v, np.float32).view(np.int32))
    return b ^ ((b >> 31) & 0x7FFFFFFF)


HALF_NEG_KEY = _f32_key_const(0.5 * NEG_INF)


def _cparams(sem):
    return pltpu.CompilerParams(dimension_semantics=sem, vmem_limit_bytes=VMEM_LIMIT)


def _row_tile(n, want=256):
    return want if n % want == 0 else n


def _inproj_kernel(x_ref, g_ref, wm_ref, wt_ref, wqt_ref, wiqt_ref, wtt_ref,
                   rq, rk, rv, rg, ak, av, akb, avb, ik, aqt, iqt, tailt):
    x = x_ref[...]
    h = (x * lax.rsqrt(jnp.mean(jnp.square(x), -1, keepdims=True) + EPS)) * g_ref[...]
    hb = h.astype(BF16)

    def proj(i):
        return jnp.dot(hb, wm_ref[:, i * 512:(i + 1) * 512], preferred_element_type=F32)

    def proj_t(wt_rows):
        return lax.dot_general(wt_rows, hb, (((1,), (1,)), ((), ())), preferred_element_type=F32)

    rq[...] = proj(0)
    rk[...] = proj(1)
    rv[...] = proj(2).astype(BF16)
    rg[...] = proj(3)
    aqt[...] = proj_t(wqt_ref[...]).astype(BF16)
    k = proj(5)
    _store_tile_major(ak, k)
    akb[...] = k.astype(BF16)
    v = proj(6)
    _store_tile_major(av, v)
    avb[...] = v.astype(BF16)
    iqt[...] = proj_t(wiqt_ref[...]).astype(BF16)
    ik[...] = jnp.dot(hb, wt_ref[...], preferred_element_type=F32)[:, :IDX_DIM]
    tailt[...] = proj_t(wtt_ref[...])


def _inproj(x2, g_mix, w_in):
    n, d = x2.shape
    tm = _row_tile(n)
    wm = w_in[:, :4096].astype(BF16)
    wt = jnp.pad(w_in[:, 4096:], ((0, 0), (0, LANES - (w_in.shape[1] - 4096)))).astype(BF16)
    rspec = lambda w, lines=1: pl.BlockSpec((tm * lines, w), lambda i: (i, 0))
    outs = [(512, F32, 1), (512, F32, 1), (512, BF16, 1), (512, F32, 1),
            (LANES, F32, ATT_HEADS), (LANES, F32, ATT_HEADS),
            (512, BF16, 1), (512, BF16, 1), (IDX_DIM, F32, 1)]
    outs_t = [(512, BF16), (512, BF16), (LANES, F32)]
    whole = lambda a: pl.BlockSpec(a.shape, lambda i: (0, 0))
    wqt, wiqt, wtt = wm[:, 4 * 512:5 * 512].T, wm[:, 7 * 512:8 * 512].T, wt.T
    return pl.pallas_call(
        _inproj_kernel,
        grid=(n // tm,),
        in_specs=[rspec(d), pl.BlockSpec((1, d), lambda i: (0, 0)), whole(wm), whole(wt),
                  whole(wqt), whole(wiqt), whole(wtt)],
        out_specs=[rspec(w, ln) for w, _, ln in outs] + [pl.BlockSpec((w, tm), lambda i: (0, i)) for w, _ in outs_t],
        out_shape=[jax.ShapeDtypeStruct((n * ln, w), dt) for w, dt, ln in outs]
        + [jax.ShapeDtypeStruct((w, n), dt) for w, dt in outs_t],
        compiler_params=_cparams(("parallel",)),
        name="inproj",
    )(x2, g_mix.reshape(1, d), wm, wt, wqt, wiqt, wtt)


def _retention_kernel(cd_ref, rq_ref, rk_ref, rv_ref, rg_ref, cos_ref, sin_ref, dmat_ref, qd_ref, kd_ref,
                      gret_ref, s0_ref, o_ref, s_ref):
    c = pl.program_id(1)

    @pl.when(c == 0)
    def _():
        s_ref[...] = s0_ref[...]

    cosf = cos_ref[...]
    sinf = sin_ref[...]
    half = RET_DK // 2

    def rot(x):
        return x * cosf + pltpu.roll(x, half, 1) * sinf

    for h in range(RET_HEADS):
        sl = slice(h * RET_DK, (h + 1) * RET_DK)
        q = rot(rq_ref[:, sl])
        k = rot(rk_ref[:, sl]) * (RET_DK ** -0.5)
        v = rv_ref[:, sl]
        s = s_ref[0, h]
        sc = lax.dot_general(q.astype(BF16), k.astype(BF16), (((1,), (1,)), ((), ())),
                             preferred_element_type=F32) * dmat_ref[h]
        o = jnp.dot(sc.astype(BF16), v, preferred_element_type=F32)
        o = o + jnp.dot((q * qd_ref[:, sl]).astype(BF16), s.astype(BF16), preferred_element_type=F32)
        kdt = jnp.transpose(k * kd_ref[:, sl]).astype(BF16)
        s_ref[0, h] = cd_ref[h] * s + jnp.dot(kdt, v, preferred_element_type=F32)
        mu = jnp.mean(o, -1, keepdims=True)
        var = jnp.mean(jnp.square(o - mu), -1, keepdims=True)
        on = (o - mu) * lax.rsqrt(var + EPS) * gret_ref[:, sl]
        o_ref[:, sl] = (jax.nn.silu(rg_ref[:, sl]) * on).astype(o_ref.dtype)


def _retention(rq, rk, rv, rg, s0, g_ret, b, t, past_len):
    cl = min(RET_CHUNK, t)
    nc = t // cl
    half = RET_DK // 2
    pos = (past_len + jnp.arange(t)).astype(F32)
    inv = ROPE_BASE ** (-jnp.arange(half, dtype=F32) / half)
    ang = pos[:, None] * inv[None, :]
    cosf = jnp.concatenate([jnp.cos(ang), jnp.cos(ang)], -1)
    sinf = jnp.concatenate([-jnp.sin(ang), jnp.sin(ang)], -1)
    log_g = jnp.log1p(-jnp.exp2(-5.0 - jnp.arange(RET_HEADS, dtype=F32)))
    i = jnp.arange(cl, dtype=F32)
    diff = i[:, None] - i[None, :]
    dmat = jnp.where(diff[None] >= 0, jnp.exp(jnp.maximum(diff, 0.0)[None] * log_g[:, None, None]), 0.0)
    kd = jnp.repeat(jnp.exp((cl - 1.0 - i)[:, None] * log_g[None, :]), RET_DK, axis=1)
    qd = jnp.repeat(jnp.exp((i + 1.0)[:, None] * log_g[None, :]), RET_DK, axis=1)
    cd = jnp.exp(cl * log_g)
    w = RET_HEADS * RET_DK
    rspec = pl.BlockSpec((cl, w), lambda bi, ci: (bi * nc + ci, 0))
    cspec = lambda shape: pl.BlockSpec(shape, lambda bi, ci: (0,) * len(shape))
    sspec = pl.BlockSpec((1, RET_HEADS, RET_DK, RET_DV), lambda bi, ci: (bi, 0, 0, 0))
    return pl.pallas_call(
        _retention_kernel,
        grid=(b, nc),
        in_specs=[pl.BlockSpec(memory_space=pltpu.SMEM), rspec, rspec, rspec, rspec,
                  pl.BlockSpec((cl, RET_DK), lambda bi, ci: (ci, 0)),
                  pl.BlockSpec((cl, RET_DK), lambda bi, ci: (ci, 0)),
                  cspec((RET_HEADS, cl, cl)), cspec((cl, w)), cspec((cl, w)), cspec((1, w)), sspec],
        out_specs=[rspec, sspec],
        out_shape=[jax.ShapeDtypeStruct((b * t, w), BF16),
                   jax.ShapeDtypeStruct((b, RET_HEADS, RET_DK, RET_DV), F32)],
        compiler_params=_cparams(("parallel", "arbitrary")),
        name="retention",
    )(cd, rq, rk, rv, rg, cosf, sinf, dmat, qd, kd, g_ret.reshape(1, w), s0)


def _t5_bucket(rel):
    nb = NUM_BUCKETS // 2
    max_exact = nb // 2
    base = jnp.where(rel > 0, nb, 0)
    n = jnp.abs(rel)
    nf = jnp.maximum(n, 1).astype(F32)
    large = max_exact + (jnp.log(nf / max_exact) / math.log(MAX_DISTANCE / max_exact) * (nb - max_exact)).astype(I32)
    large = jnp.minimum(large, nb - 1)
    return base + jnp.where(n < max_exact, n, large)


def _band_kernel(rb_ref, bucket_ref, band_ref):
    bucket = bucket_ref[...]
    for h in range(ATT_HEADS):
        acc = jnp.zeros(bucket.shape, F32)
        for j in range(NUM_BUCKETS):
            acc = jnp.where(bucket == j, rb_ref[j, h], acc)
        band_ref[h] = acc


def _bias_band(rel_bias):
    c = jnp.arange(BAND_TILES * LANES, dtype=I32)[:, None]
    t = jnp.arange(LANES, dtype=I32)[None, :]
    bucket = _t5_bucket(c - (BAND_TILES - 1) * LANES - t)
    return pl.pallas_call(
        _band_kernel,
        in_specs=[pl.BlockSpec(memory_space=pltpu.SMEM), pl.BlockSpec(memory_space=pltpu.VMEM)],
        out_specs=pl.BlockSpec(memory_space=pltpu.VMEM),
        out_shape=jax.ShapeDtypeStruct((ATT_HEADS, BAND_TILES * LANES, LANES), F32),
        name="bias_band",
    )(rel_bias, bucket)


def _order_key(s):
    bits = lax.bitcast_convert_type(s, I32)
    return bits ^ ((bits >> 31) & 0x7FFFFFFF)


def _fold8(x, op=jnp.add):
    parts = [x[i * SUBLANES:(i + 1) * SUBLANES] for i in range(x.shape[0] // SUBLANES)]
    while len(parts) > 1:
        parts = [op(parts[i], parts[i + 1]) for i in range(0, len(parts), 2)]
    return parts[0]


GROUP = LANES // SUBLANES


def _sort_network(n):
    pairs = []
    p = 1
    while p < n:
        k = p
        while k >= 1:
            for j in range(k % p, n - k, 2 * k):
                for i in range(min(k, n - j - k)):
                    if (i + j) // (2 * p) == (i + j + k) // (2 * p):
                        pairs.append((i + j, i + j + k))
            k //= 2
        p *= 2
    return pairs


def _sort_desc(vals):
    vals = list(vals)
    for a, b in _sort_network(len(vals)):
        vals[a], vals[b] = jnp.maximum(vals[a], vals[b]), jnp.minimum(vals[a], vals[b])
    return vals


def _tile_loop(nt, trip, carry):
    def run(first, trips, width, carry):
        return lax.fori_loop(0, trips, lambda i, c: trip([first + i * width + u for u in range(width)], c), carry)
    carry = run(0, nt // 4, 4, carry)
    carry = run(nt // 4 * 4, (nt // 2) % 2, 2, carry)
    return run(nt // 2 * 2, nt % 2, 1, carry)


def _attention_kernel(q_ref, iq_ref, tail_ref, k_ref, vt_ref, ki_ref, band_ref, adm_ref, o_ref,
                      iqs_s, key_s, ks_s, mask_s, log_s, acc_s, *, jd0, n_sel, idx_bits):
    tq = q_ref.shape[1]
    qb = tq // LANES
    jd = jd0 + qb * pl.program_id(1) + qb - 1
    nt = jd + 1
    classes = band_ref.shape[1] // LANES
    krow = lax.broadcasted_iota(I32, (LANES, tq), 0)

    qt = q_ref[...]
    iqt = iq_ref[...]
    wt = tail_ref[IDX_DIM:IDX_DIM + IDX_HEADS, :] * (IDX_HEADS ** -0.5) * (IDX_DIM ** -0.5)
    zpad = jnp.zeros((LANES - IDX_DIM, tq), BF16)
    for h in range(IDX_HEADS):
        iqs_s[:, h * tq:(h + 1) * tq] = jnp.concatenate([iqt[h * IDX_DIM:(h + 1) * IDX_DIM], zpad], axis=0)

    def tile_rows(j):
        return pl.ds(pl.multiple_of(j * LANES, LANES), LANES)

    def class_rows(j):
        return tile_rows(jnp.clip(j - jd + classes - 1, 0, classes - 1))

    def score_tile(j):
        d = jnp.dot(ki_ref[tile_rows(j), :], iqs_s[...], preferred_element_type=F32)
        acc = jnp.zeros((LANES, tq), F32)
        for h in range(IDX_HEADS):
            acc = acc + wt[h:h + 1] * jnp.maximum(d[:, h * tq:(h + 1) * tq], 0.0)
        return acc

    def score_trip(js, carry):
        for j in js:
            key = jnp.where(adm_ref[class_rows(j), :] > 0.0, _order_key(score_tile(j)), _f32_key_const(NEG_INF))
            key_s[j] = key
            srt = _sort_desc([key[g * SUBLANES:(g + 1) * SUBLANES] for g in range(GROUP)])
            for g in range(GROUP):
                ks_s[j, g * SUBLANES:(g + 1) * SUBLANES] = srt[g]
        return carry

    _tile_loop(nt, score_trip, 0)
    nquad = (nt + 3) // 4

    def fill_body(j, carry):
        ks_s[j] = jnp.full((LANES, tq), INT_MIN, I32)
        return carry

    lax.fori_loop(nt, nquad * 4, fill_body, 0)

    def count(pred_tile):
        def trip(js, acc):
            for j in js:
                acc = acc + _fold8(pred_tile(j))
            return acc
        return jnp.sum(_tile_loop(nt, trip, jnp.zeros((SUBLANES, tq), F32)), axis=0, keepdims=True)

    def count_ge(cand):
        def one(m):
            return jnp.where(m, 1.0, 0.0)

        def trip(q, accs):
            a8, a4, a2, a1 = accs
            for u in range(4):
                v = [ks_s[4 * q + u, g * SUBLANES:(g + 1) * SUBLANES] for g in range(GROUP)]
                t1 = v[7] >= cand
                t2 = jnp.where(t1, v[11], v[3]) >= cand
                t3 = jnp.where(t1, jnp.where(t2, v[13], v[9]), jnp.where(t2, v[5], v[1])) >= cand
                t4 = jnp.where(t1, jnp.where(t2, jnp.where(t3, v[14], v[12]), jnp.where(t3, v[10], v[8])),
                               jnp.where(t2, jnp.where(t3, v[6], v[4]), jnp.where(t3, v[2], v[0]))) >= cand
                a8, a4, a2 = a8 + one(t1), a4 + one(t2), a2 + one(t3)
                a1 = a1 + one(t4) + one(v[15] >= cand)
            return a8, a4, a2, a1

        zero = jnp.zeros((SUBLANES, tq), F32)
        a8, a4, a2, a1 = lax.fori_loop(0, nquad, trip, (zero, zero, zero, zero))
        return jnp.sum(8.0 * a8 + 4.0 * a4 + 2.0 * a2 + a1, axis=0, keepdims=True)

    kf = float(n_sel)

    def bit_body(it, thr):
        cand = thr + lax.shift_left(jnp.int32(1), 31 - it)
        return jnp.where(count_ge(cand) >= kf, cand, thr)

    thr = lax.fori_loop(0, 32, bit_body, jnp.full((1, tq), INT_MIN, I32))

    need = kf - count(lambda j: jnp.where(key_s[j] > thr, 1.0, 0.0))
    n_tied = count(lambda j: jnp.where(key_s[j] == thr, 1.0, 0.0))
    surplus = jnp.where(thr > HALF_NEG_KEY, n_tied - need, 0.0)

    def tie_search():
        def tie_body(it, j0):
            cand = j0 + lax.shift_left(jnp.int32(1), idx_bits - 1 - it)
            cnt = count(lambda j: jnp.where(key_s[j] == thr, jnp.where(krow + j * LANES < cand, 1.0, 0.0), 0.0))
            return jnp.where(cnt < need, cand, j0)
        return lax.fori_loop(0, idx_bits, tie_body, jnp.zeros((1, tq), I32))

    j0 = lax.cond(jnp.max(surplus) > 0.0, tie_search, lambda: jnp.full((1, tq), INT_MAX, I32))

    def mask_trip(js, carry):
        for j in js:
            key = key_s[j]
            sel = jnp.where(key > thr, 1.0, jnp.where(key == thr, jnp.where(krow + j * LANES <= j0, 1.0, 0.0), 0.0))
            sel = jnp.where(key > HALF_NEG_KEY, sel, 0.0)
            mask_s[j] = jnp.where(sel > 0.0, 0.0, NEG_INF)
        return carry

    _tile_loop(nt, mask_trip, 0)

    scale = ATT_HEAD_DIM ** -0.5
    heads = range(ATT_HEADS)
    qh = [qt[h * ATT_HEAD_DIM:(h + 1) * ATT_HEAD_DIM] for h in heads]

    def log_trip(js, mx):
        mx = list(mx)
        for j in js:
            band_rows = class_rows(j)
            msk = mask_s[j]
            for h in heads:
                kh = k_ref[tile_rows(j), h * ATT_HEAD_DIM:(h + 1) * ATT_HEAD_DIM]
                lg = jnp.dot(kh, qh[h], preferred_element_type=F32) * scale + band_ref[h, band_rows, :] + msk
                log_s[h, j] = lg
                mx[h] = jnp.maximum(mx[h], _fold8(lg, jnp.maximum))
        return tuple(mx)

    mx = _tile_loop(nt, log_trip, tuple(jnp.full((SUBLANES, tq), NEG_INF, F32) for _ in heads))
    m = [jnp.max(mx[h], axis=0, keepdims=True) for h in heads]
    acc_s[...] = jnp.zeros(acc_s.shape, F32)

    def pv_trip(js, ls):
        ls = list(ls)
        for h in heads:
            acc = acc_s[h]
            for j in js:
                p = jnp.exp(log_s[h, j] - m[h])
                ls[h] = ls[h] + _fold8(p)
                vth = vt_ref[j, h * ATT_HEAD_DIM:(h + 1) * ATT_HEAD_DIM, :]
                acc = acc + jnp.dot(vth, p.astype(BF16), preferred_element_type=F32)
            acc_s[h] = acc
        return tuple(ls)

    ls = _tile_loop(nt, pv_trip, tuple(jnp.zeros((SUBLANES, tq), F32) for _ in heads))
    for h in heads:
        ot = acc_s[h] / jnp.sum(ls[h], axis=0, keepdims=True)
        for u in range(qb):
            o_ref[u * LANES:(u + 1) * LANES, h * ATT_HEAD_DIM:(h + 1) * ATT_HEAD_DIM] = jnp.transpose(
                ot[:, u * LANES:(u + 1) * LANES]).astype(o_ref.dtype)


def _attention(aqt, iqt, tailt, keys_k, keys_v, keys_i, band, b, t, past_len):
    l = past_len + t
    tq = LANES
    nqb = -(-t // tq)
    tp = nqb * tq
    assert past_len % LANES == 0 and (t % tq == 0 or nqb == 1)
    jd0 = past_len // LANES
    ntiles = jd0 + nqb
    ntp = ntiles
    lp = ntp * LANES
    n_sel = min(TOPK_MAX, l // 4)
    if tp != t:
        padq = lambda a: jnp.pad(a.reshape(-1, b, t), ((0, 0), (0, 0), (0, tp - t))).reshape(-1, b * tp)
        aqt, iqt, tailt = padq(aqt), padq(iqt), padq(tailt)
    w = ATT_HEADS * ATT_HEAD_DIM
    pad = ((0, 0), (0, lp - l), (0, 0))
    kk = jnp.pad(keys_k, pad)
    vt = jnp.pad(keys_v, pad).reshape(b, ntp, LANES, w).transpose(0, 1, 3, 2)
    ki2 = jnp.pad(keys_i, ((0, 0), (0, lp - l), (0, LANES - IDX_DIM)))
    cc = jnp.arange(LANES)[:, None]
    tt = jnp.arange(LANES)[None, :]
    adm = ((cc // CHUNK <= tt // CHUNK) & (cc < l - (ntiles - 1) * LANES)).astype(F32)
    qb = 2 if nqb % 2 == 0 else 1
    wq = qb * LANES
    classes = BAND_TILES + qb - 1
    band_w, adm_w = [], []
    for r in range(classes):
        rel = [r - (classes - 1) + (qb - 1 - u) for u in range(qb)]
        band_w.append(jnp.concatenate(
            [band[:, min(max(x + BAND_TILES - 1, 0), BAND_TILES - 1) * LANES:][:, :LANES] for x in rel], axis=2))
        adm_w.append(jnp.concatenate(
            [adm if x == 0 else jnp.full((LANES, LANES), 1.0 if x < 0 else 0.0, F32) for x in rel], axis=1))
    band_w = jnp.concatenate(band_w, axis=1)
    adm_w = jnp.concatenate(adm_w, axis=0)
    nsteps = nqb // qb
    qspec = lambda width: pl.BlockSpec((wq, width), lambda bi, qi: (bi * nsteps + qi, 0))
    qtspec = lambda rows: pl.BlockSpec((rows, wq), lambda bi, qi: (0, bi * nsteps + qi))
    kspec = lambda width: pl.BlockSpec((None, lp, width), lambda bi, qi: (bi, 0, 0))
    kern = functools.partial(_attention_kernel, jd0=jd0, n_sel=n_sel, idx_bits=max((lp - 1).bit_length(), 1))
    out = pl.pallas_call(
        kern,
        grid=(b, nsteps),
        in_specs=[qtspec(w), qtspec(IDX_HEADS * IDX_DIM), qtspec(LANES),
                  kspec(w), pl.BlockSpec((None, ntp, w, LANES), lambda bi, qi: (bi, 0, 0, 0)), kspec(LANES),
                  pl.BlockSpec((ATT_HEADS, classes * LANES, wq), lambda bi, qi: (0, 0, 0)),
                  pl.BlockSpec((classes * LANES, wq), lambda bi, qi: (0, 0))],
        out_specs=qspec(w),
        out_shape=jax.ShapeDtypeStruct((b * tp, w), BF16),
        scratch_shapes=[pltpu.VMEM((LANES, IDX_HEADS * wq), BF16),
                        pltpu.VMEM((ntp, LANES, wq), I32), pltpu.VMEM((-(-ntp // 4) * 4, LANES, wq), I32),
                        pltpu.VMEM((ntp, LANES, wq), F32),
                        pltpu.VMEM((ATT_HEADS, ntp, LANES, wq), F32), pltpu.VMEM((ATT_HEADS, ATT_HEAD_DIM, wq), F32)],
        compiler_params=_cparams(("parallel", "arbitrary")),
        name="attention",
    )(aqt, iqt, tailt, kk, vt, ki2, band_w, adm_w)
    if tp != t:
        out = out.reshape(b, tp, w)[:, :t].reshape(b * t, w)
    return out


def _store_tile_major(ref, x):
    rows, width = x.shape
    nk = width // LANES
    for c in range(nk):
        ref[pl.ds(c, rows, stride=nk), :] = x[:, c * LANES:(c + 1) * LANES]


def _load_tile_major(ref, rows, nk, first=0, stride=None):
    stride = nk if stride is None else stride
    return jnp.concatenate([ref[pl.ds(first + c, rows, stride=stride), :] for c in range(nk)], axis=1)


def _outproj_kernel(x_ref, oret_ref, oatt_ref, wo_ref, g_ref, wr_ref, br_ref, x1_ref, h2_ref, e_ref, gt_ref):
    nr = oret_ref.shape[1]
    mix = jnp.dot(oret_ref[...], wo_ref[:nr, :], preferred_element_type=F32)
    mix = mix + jnp.dot(oatt_ref[...], wo_ref[nr:, :], preferred_element_type=F32)
    x1 = x_ref[...] + mix
    x1_ref[...] = x1
    h2 = (x1 * lax.rsqrt(jnp.mean(jnp.square(x1), -1, keepdims=True) + EPS)) * g_ref[...]
    _store_tile_major(h2_ref, h2)
    lg = jnp.dot(h2.astype(BF16), wr_ref[...], preferred_element_type=F32) + br_ref[...]
    tm = lg.shape[0]
    gl = lg[:, :N_GROUPS]
    gmax = jnp.max(gl, -1, keepdims=True)
    p_top = 1.0 / jnp.sum(jnp.exp(gl - gmax), -1, keepdims=True)
    gi = lax.broadcasted_iota(I32, (tm, N_GROUPS), 1).astype(F32)
    g_top = jnp.min(jnp.where(gl == gmax, gi, float(N_GROUPS)), -1, keepdims=True)
    el = jnp.zeros((tm, EXPERTS_PER_GROUP), F32)
    for g in range(N_GROUPS):
        lo = N_GROUPS + g * EXPERTS_PER_GROUP
        el = jnp.where(g_top == float(g), lg[:, lo:lo + EXPERTS_PER_GROUP], el)
    ei = lax.broadcasted_iota(I32, (tm, EXPERTS_PER_GROUP), 1).astype(F32)
    v1 = jnp.max(el, -1, keepdims=True)
    i1 = jnp.min(jnp.where(el == v1, ei, float(EXPERTS_PER_GROUP)), -1, keepdims=True)
    el2 = jnp.where(ei == i1, -jnp.inf, el)
    v2 = jnp.max(el2, -1, keepdims=True)
    i2 = jnp.min(jnp.where(el2 == v2, ei, float(EXPERTS_PER_GROUP)), -1, keepdims=True)
    e2 = jnp.exp(v2 - v1)
    den = 1.0 + e2
    two = lax.broadcasted_iota(I32, (tm, TOP_E), 1)
    e_ref[...] = (g_top * EXPERTS_PER_GROUP + jnp.where(two == 0, i1, i2)).astype(I32)
    gt_ref[...] = jnp.where(two == 0, 1.0 / den, e2 / den) * p_top


def _outproj_router(x2, o_ret, o_att, w_out, g_ffn, w_group, b_group, w_er, b_er):
    n, d = x2.shape
    tm = _row_tile(n, 512)
    nk = d // LANES
    nrt = N_GROUPS + N_EXPERTS
    wr = jnp.pad(jnp.concatenate([w_group, w_er.reshape(d, N_EXPERTS)], 1), ((0, 0), (0, LANES - nrt))).astype(BF16)
    br = jnp.pad(jnp.concatenate([b_group, b_er.reshape(N_EXPERTS)]), (0, LANES - nrt)).reshape(1, LANES)
    rspec = lambda w: pl.BlockSpec((tm, w), lambda i: (i, 0))
    cspec = lambda r, c: pl.BlockSpec((r, c), lambda i: (0, 0))
    mw = w_out.shape[0]
    return pl.pallas_call(
        _outproj_kernel,
        grid=(n // tm,),
        in_specs=[rspec(d), rspec(o_ret.shape[1]), rspec(o_att.shape[1]), cspec(mw, d), cspec(1, d),
                  cspec(d, LANES), cspec(1, LANES)],
        out_specs=[rspec(d), pl.BlockSpec((tm * nk, LANES), lambda i: (i, 0)), rspec(TOP_E), rspec(TOP_E)],
        out_shape=[jax.ShapeDtypeStruct((n, d), F32), jax.ShapeDtypeStruct((n * nk, LANES), F32),
                   jax.ShapeDtypeStruct((n, TOP_E), I32), jax.ShapeDtypeStruct((n, TOP_E), F32)],
        compiler_params=_cparams(("parallel",)),
        name="outproj_router",
    )(x2, o_ret, o_att, w_out.astype(BF16), g_ffn.reshape(1, d), wr, br)


def _expert_kernel(blk_e_ref, tok0_ref, tokn_ref, slot_ref, h_ref, wg_ref, wu_ref, wd_ref, o_ref,
                   xbuf, obuf, gsem, ssem):
    i = pl.program_id(0)
    nb = pl.num_programs(0)
    nk = wg_ref.shape[0] // LANES
    rb = xbuf.shape[0] // (2 * nk)
    blk = rb * nk
    cur = i % 2
    nxt = 1 - cur

    def lines(first, count):
        return pl.ds(pl.multiple_of(first, nk), count)

    def for_rows(start_row):
        def body(g, carry):
            for u in range(DMA_UNROLL):
                start_row(g * DMA_UNROLL + u, u % 2)
            return carry
        lax.fori_loop(0, rb // DMA_UNROLL, body, 0)

    def start_gather(tok_ref, buf):
        for_rows(lambda r, pri: pltpu.make_async_copy(
            h_ref.at[lines(tok_ref[0, 0, r], nk)], xbuf.at[lines(buf * blk + r * nk, nk)],
            gsem.at[buf]).start(priority=pri))

    def wait_gather(buf):
        pltpu.make_async_copy(h_ref.at[pl.ds(0, blk)], xbuf.at[lines(buf * blk, blk)], gsem.at[buf]).wait()

    def wait_scatter(buf):
        pltpu.make_async_copy(obuf.at[lines(buf * blk, blk)], o_ref.at[pl.ds(0, blk)], ssem.at[buf]).wait()

    @pl.when(i == 0)
    def _():
        start_gather(tok0_ref, 0)

    @pl.when(i + 1 < nb)
    def _():
        start_gather(tokn_ref, nxt)

    wait_gather(cur)
    base = cur * blk
    xb = jnp.concatenate([xbuf[pl.ds(base + c, rb, stride=nk), :] for c in range(nk)], axis=1).astype(BF16)
    hid = jax.nn.silu(jnp.dot(xb, wg_ref[...].astype(BF16), preferred_element_type=F32))
    hid = hid * jnp.dot(xb, wu_ref[...].astype(BF16), preferred_element_type=F32)
    out = jnp.dot(hid.astype(BF16), wd_ref[...].astype(BF16), preferred_element_type=F32)

    @pl.when(i >= 2)
    def _():
        wait_scatter(cur)

    for c in range(nk):
        obuf[pl.ds(base + c, rb, stride=nk), :] = out[:, c * LANES:(c + 1) * LANES]

    for_rows(lambda r, pri: pltpu.make_async_copy(
        obuf.at[lines(base + r * nk, nk)], o_ref.at[lines(slot_ref[0, 0, r], nk)], ssem.at[cur]).start(priority=pri))

    @pl.when(i == nb - 1)
    def _():
        wait_scatter(cur)

        @pl.when(nb >= 2)
        def _():
            wait_scatter(nxt)


def _experts(h2, src_tok, out_slot, blk_e, w_gate, w_up, w_down, rb):
    d, de = w_gate.shape[1:]
    nk = d // LANES
    rows = src_tok.shape[0]
    nb = rows // rb
    idx3 = lambda a: (a * nk).reshape(nb, 1, rb)
    ispec = lambda f: pl.BlockSpec((1, 1, rb), f, memory_space=pltpu.SMEM)
    wspec = lambda r, c: pl.BlockSpec((None, r, c), lambda i, be: (be[i], 0, 0))
    return pl.pallas_call(
        _expert_kernel,
        grid_spec=pltpu.PrefetchScalarGridSpec(
            num_scalar_prefetch=1,
            grid=(nb,),
            in_specs=[ispec(lambda i, be: (0, 0, 0)), ispec(lambda i, be: (jnp.minimum(i + 1, nb - 1), 0, 0)),
                      ispec(lambda i, be: (i, 0, 0)), pl.BlockSpec(memory_space=pl.ANY),
                      wspec(d, de), wspec(d, de), wspec(de, d)],
            out_specs=pl.BlockSpec(memory_space=pl.ANY),
            scratch_shapes=[pltpu.VMEM((2 * rb * nk, LANES), F32), pltpu.VMEM((2 * rb * nk, LANES), F32),
                            pltpu.SemaphoreType.DMA((2,)), pltpu.SemaphoreType.DMA((2,))],
        ),
        out_shape=jax.ShapeDtypeStruct((rows * nk, LANES), F32),
        compiler_params=_cparams(("arbitrary",)),
        name="experts",
    )(blk_e, idx3(src_tok), idx3(src_tok), idx3(out_slot), h2, w_gate, w_up, w_down)


def _route_plan(expert, rb):
    n = expert.shape[0]
    a = n * TOP_E
    flat_e = expert.reshape(a)
    counts = jnp.sum((flat_e[:, None] == jnp.arange(N_EXPERTS, dtype=I32)[None, :]).astype(I32), axis=0)
    padded = (counts + rb - 1) // rb * rb
    ends = jnp.cumsum(padded)
    n_blocks = (a + N_EXPERTS * (rb - 1) + rb - 1) // rb
    rows = n_blocks * rb
    q = jnp.arange(rb - 1, dtype=I32)[None, :]
    e = jnp.arange(N_EXPERTS, dtype=I32)[:, None]
    pad_key = jnp.where(q < (padded - counts)[:, None], 2 * e + 1, 2 * N_EXPERTS).reshape(-1)
    spare = jnp.full((rows - a - N_EXPERTS * (rb - 1),), 2 * N_EXPERTS, I32)
    perm = jnp.argsort(jnp.concatenate([2 * flat_e, pad_key, spare]), stable=True).astype(I32)
    src_tok = jnp.where(perm < a, perm // TOP_E, 0)
    out_slot = perm
    blk_e = jnp.minimum(jnp.sum((ends[None, :] <= jnp.arange(n_blocks, dtype=I32)[:, None] * rb).astype(I32), axis=1),
                        N_EXPERTS - 1)
    return src_tok, out_slot, blk_e


def _combine_kernel(x1_ref, eo_ref, gt_ref, g_ref, y_ref):
    tm, d = x1_ref.shape
    nk = d // LANES
    gt = gt_ref[...]
    moe = _load_tile_major(eo_ref, tm, nk, 0, TOP_E * nk) * gt[:, 0:1]
    moe = moe + _load_tile_major(eo_ref, tm, nk, nk, TOP_E * nk) * gt[:, 1:2]
    x = x1_ref[...] + moe
    y_ref[...] = (x * lax.rsqrt(jnp.mean(jnp.square(x), -1, keepdims=True) + EPS)) * g_ref[...]


def _combine(x1, eo2, gate, g_final):
    n, d = x1.shape
    tm = _row_tile(n, 512)
    nk = d // LANES
    rspec = lambda w: pl.BlockSpec((tm, w), lambda i: (i, 0))
    return pl.pallas_call(
        _combine_kernel,
        grid=(n // tm,),
        in_specs=[rspec(d), pl.BlockSpec((tm * TOP_E * nk, LANES), lambda i: (i, 0)), rspec(TOP_E),
                  pl.BlockSpec((1, d), lambda i: (0, 0))],
        out_specs=rspec(d),
        out_shape=jax.ShapeDtypeStruct((n, d), F32),
        compiler_params=_cparams(("parallel",)),
        name="combine",
    )(x1, eo2, gate, g_final.reshape(1, d))


def _layer(x, past_len, s_ret, past_k, past_v, past_ki, band, params, g_final):
    (rel_bias, g_mix, w_in, g_ret, w_out, g_ffn, w_group, b_group, w_er, b_er, w_gate, w_up, w_down) = params
    b, t, d = x.shape
    n = b * t
    nk = d // LANES
    x2 = x.reshape(n, d)
    rq, rk, rv, rg, ak, av, akb, avb, ik, aqt, iqt, tailt = _inproj(x2, g_mix, w_in)
    o_ret, s_new = _retention(rq, rk, rv, rg, s_ret, g_ret, b, t, past_len)
    aw = ATT_HEADS * ATT_HEAD_DIM
    keys_k = jnp.concatenate([past_k.reshape(b, past_len, aw).astype(BF16), akb.reshape(b, t, aw)], 1)
    keys_v = jnp.concatenate([past_v.reshape(b, past_len, aw).astype(BF16), avb.reshape(b, t, aw)], 1)
    keys_i = jnp.concatenate([past_ki.astype(BF16), ik.reshape(b, t, IDX_DIM).astype(BF16)], 1)
    o_att = _attention(aqt, iqt, tailt, keys_k, keys_v, keys_i, band, b, t, past_len)
    x1, h2, expert, gate = _outproj_router(x2, o_ret, o_att, w_out, g_ffn, w_group, b_group, w_er, b_er)
    rb = 256 if n >= 8192 else 128
    src_tok, out_slot, blk_e = _route_plan(expert, rb)
    eo = _experts(h2, src_tok, out_slot, blk_e, w_gate, w_up, w_down, rb)
    y = _combine(x1, eo, gate, g_final)
    return (y.reshape(b, t, d), ak.reshape(1, b, t, ATT_HEADS, ATT_HEAD_DIM),
            av.reshape(1, b, t, ATT_HEADS, ATT_HEAD_DIM), ik.reshape(1, b, t, IDX_DIM), s_new[None])


def kernel(x_prompt, x_sample, cache_attn_k, cache_attn_v, cache_idx_k, state_ret, rel_bias, g_mix, w_in, g_ret, w_out, g_ffn, w_group, b_group, w_expert_router, b_expert_router, w_gate, w_up, w_down, g_final):
    assert g_mix.shape[0] == 1, "single-layer model"
    params = (rel_bias, g_mix[0], w_in[0], g_ret[0], w_out[0], g_ffn[0], w_group[0], b_group[0],
              w_expert_router[0], b_expert_router[0], w_gate[0], w_up[0], w_down[0])
    band = _bias_band(rel_bias)
    nb = x_prompt.shape[0]
    past_len = cache_attn_k.shape[2]
    dt = x_prompt.dtype
    empty_kv = jnp.zeros((nb, 0, ATT_HEADS, ATT_HEAD_DIM), dt)
    empty_ki = jnp.zeros((nb, 0, IDX_DIM), dt)
    s0 = jnp.zeros((nb, RET_HEADS, RET_DK, RET_DV), dt)
    yp, kp, vp, kip, sp = _layer(x_prompt, 0, s0, empty_kv, empty_kv, empty_ki, band, params, g_final)
    ys, ks, vs, kis, ss = _layer(x_sample, past_len, state_ret[0], cache_attn_k[0], cache_attn_v[0],
                                 cache_idx_k[0], band, params, g_final)
    return (yp, ys, kp, vp, kip, sp, ks, vs, kis, ss)
```

```python
import functools
import math

import jax
import jax.numpy as jnp
import numpy as np
from jax import lax
from jax.experimental import pallas as pl
from jax.experimental.pallas import tpu as pltpu

F32 = jnp.float32
BF16 = jnp.bfloat16
I32 = jnp.int32

CHUNK = 64
RET_HEADS = 4
RET_DK = 128
RET_DV = 128
ROPE_BASE = 10000.0
ATT_HEADS = 4
ATT_HEAD_DIM = 128
IDX_HEADS = 8
IDX_DIM = 64
TOPK_MAX = 256
NUM_BUCKETS = 32
MAX_DISTANCE = 128
N_GROUPS = 4
EXPERTS_PER_GROUP = 8
N_EXPERTS = N_GROUPS * EXPERTS_PER_GROUP
TOP_E = 2
EPS = 1e-6
NEG_INF = -1e30

LANES = 128
SUBLANES = 8
VMEM_LIMIT = 56 * 1024 * 1024
RET_CHUNK = 256
DMA_UNROLL = 8
INT_MIN = -(2 ** 31)
INT_MAX = 2 ** 31 - 1
BAND_TILES = 3
assert (BAND_TILES - 2) * LANES + 1 >= MAX_DISTANCE
assert math.log(IDX_DIM, 4).is_integer()


def _f32_key_const(v):
    b = int(np.array(v, np.float32).view(np.int32))
    return b ^ ((b >> 31) & 0x7FFFFFFF)


HALF_NEG_KEY = _f32_key_const(0.5 * NEG_INF)


def _cparams(sem):
    return pltpu.CompilerParams(dimension_semantics=sem, vmem_limit_bytes=VMEM_LIMIT)


def _row_tile(n, want=256):
    return want if n % want == 0 else n


def _inproj_kernel(x_ref, g_ref, wm_ref, wt_ref, wqt_ref, wiqt_ref, wtt_ref,
                   rq, rk, rv, rg, ak, av, akb, avb, ik, aqt, iqt, tailt):
    x = x_ref[...]
    h = (x * lax.rsqrt(jnp.mean(jnp.square(x), -1, keepdims=True) + EPS)) * g_ref[...]
    hb = h.astype(BF16)

    def proj(i):
        return jnp.dot(hb, wm_ref[:, i * 512:(i + 1) * 512], preferred_element_type=F32)

    def proj_t(wt_rows):
        return lax.dot_general(wt_rows, hb, (((1,), (1,)), ((), ())), preferred_element_type=F32)

    rq[...] = proj(0)
    rk[...] = proj(1)
    rv[...] = proj(2).astype(BF16)
    rg[...] = proj(3)
    aqt[...] = proj_t(wqt_ref[...]).astype(BF16)
    k = proj(5)
    _store_tile_major(ak, k)
    akb[...] = k.astype(BF16)
    v = proj(6)
    _store_tile_major(av, v)
    avb[...] = v.astype(BF16)
    iqt[...] = proj_t(wiqt_ref[...]).astype(BF16)
    ik[...] = jnp.dot(hb, wt_ref[...], preferred_element_type=F32)[:, :IDX_DIM]
    tailt[...] = proj_t(wtt_ref[...])


def _inproj(x2, g_mix, w_in):
    n, d = x2.shape
    tm = _row_tile(n)
    wm = w_in[:, :4096].astype(BF16)
    wt = jnp.pad(w_in[:, 4096:], ((0, 0), (0, LANES - (w_in.shape[1] - 4096)))).astype(BF16)
    rspec = lambda w, lines=1: pl.BlockSpec((tm * lines, w), lambda i: (i, 0))
    outs = [(512, F32, 1), (512, F32, 1), (512, BF16, 1), (512, F32, 1),
            (LANES, F32, ATT_HEADS), (LANES, F32, ATT_HEADS),
            (512, BF16, 1), (512, BF16, 1), (IDX_DIM, F32, 1)]
    outs_t = [(512, BF16), (512, BF16), (LANES, F32)]
    whole = lambda a: pl.BlockSpec(a.shape, lambda i: (0, 0))
    wqt, wiqt, wtt = wm[:, 4 * 512:5 * 512].T, wm[:, 7 * 512:8 * 512].T, wt.T
    return pl.pallas_call(
        _inproj_kernel,
        grid=(n // tm,),
        in_specs=[rspec(d), pl.BlockSpec((1, d), lambda i: (0, 0)), whole(wm), whole(wt),
                  whole(wqt), whole(wiqt), whole(wtt)],
        out_specs=[rspec(w, ln) for w, _, ln in outs] + [pl.BlockSpec((w, tm), lambda i: (0, i)) for w, _ in outs_t],
        out_shape=[jax.ShapeDtypeStruct((n * ln, w), dt) for w, dt, ln in outs]
        + [jax.ShapeDtypeStruct((w, n), dt) for w, dt in outs_t],
        compiler_params=_cparams(("parallel",)),
        name="inproj",
    )(x2, g_mix.reshape(1, d), wm, wt, wqt, wiqt, wtt)


def _retention_kernel(cd_ref, rq_ref, rk_ref, rv_ref, rg_ref, cos_ref, sin_ref, dmat_ref, qd_ref, kd_ref,
                      gret_ref, s0_ref, o_ref, s_ref):
    c = pl.program_id(1)

    @pl.when(c == 0)
    def _():
        s_ref[...] = s0_ref[...]

    cosf = cos_ref[...]
    sinf = sin_ref[...]
    half = RET_DK // 2

    def rot(x):
        return x * cosf + pltpu.roll(x, half, 1) * sinf

    for h in range(RET_HEADS):
        sl = slice(h * RET_DK, (h + 1) * RET_DK)
        q = rot(rq_ref[:, sl])
        k = rot(rk_ref[:, sl]) * (RET_DK ** -0.5)
        v = rv_ref[:, sl]
        s = s_ref[0, h]
        sc = lax.dot_general(q.astype(BF16), k.astype(BF16), (((1,), (1,)), ((), ())),
                             preferred_element_type=F32) * dmat_ref[h]
        o = jnp.dot(sc.astype(BF16), v, preferred_element_type=F32)
        o = o + jnp.dot((q * qd_ref[:, sl]).astype(BF16), s.astype(BF16), preferred_element_type=F32)
        kdt = jnp.transpose(k * kd_ref[:, sl]).astype(BF16)
        s_ref[0, h] = cd_ref[h] * s + jnp.dot(kdt, v, preferred_element_type=F32)
        mu = jnp.mean(o, -1, keepdims=True)
        var = jnp.mean(jnp.square(o - mu), -1, keepdims=True)
        on = (o - mu) * lax.rsqrt(var + EPS) * gret_ref[:, sl]
        o_ref[:, sl] = (jax.nn.silu(rg_ref[:, sl]) * on).astype(o_ref.dtype)


def _retention(rq, rk, rv, rg, s0, g_ret, b, t, past_len):
    cl = min(RET_CHUNK, t)
    nc = t // cl
    half = RET_DK // 2
    pos = (past_len + jnp.arange(t)).astype(F32)
    inv = ROPE_BASE ** (-jnp.arange(half, dtype=F32) / half)
    ang = pos[:, None] * inv[None, :]
    cosf = jnp.concatenate([jnp.cos(ang), jnp.cos(ang)], -1)
    sinf = jnp.concatenate([-jnp.sin(ang), jnp.sin(ang)], -1)
    log_g = jnp.log1p(-jnp.exp2(-5.0 - jnp.arange(RET_HEADS, dtype=F32)))
    i = jnp.arange(cl, dtype=F32)
    diff = i[:, None] - i[None, :]
    dmat = jnp.where(diff[None] >= 0, jnp.exp(jnp.maximum(diff, 0.0)[None] * log_g[:, None, None]), 0.0)
    kd = jnp.repeat(jnp.exp((cl - 1.0 - i)[:, None] * log_g[None, :]), RET_DK, axis=1)
    qd = jnp.repeat(jnp.exp((i + 1.0)[:, None] * log_g[None, :]), RET_DK, axis=1)
    cd = jnp.exp(cl * log_g)
    w = RET_HEADS * RET_DK
    rspec = pl.BlockSpec((cl, w), lambda bi, ci: (bi * nc + ci, 0))
    cspec = lambda shape: pl.BlockSpec(shape, lambda bi, ci: (0,) * len(shape))
    sspec = pl.BlockSpec((1, RET_HEADS, RET_DK, RET_DV), lambda bi, ci: (bi, 0, 0, 0))
    return pl.pallas_call(
        _retention_kernel,
        grid=(b, nc),
        in_specs=[pl.BlockSpec(memory_space=pltpu.SMEM), rspec, rspec, rspec, rspec,
                  pl.BlockSpec((cl, RET_DK), lambda bi, ci: (ci, 0)),
                  pl.BlockSpec((cl, RET_DK), lambda bi, ci: (ci, 0)),
                  cspec((RET_HEADS, cl, cl)), cspec((cl, w)), cspec((cl, w)), cspec((1, w)), sspec],
        out_specs=[rspec, sspec],
        out_shape=[jax.ShapeDtypeStruct((b * t, w), BF16),
                   jax.ShapeDtypeStruct((b, RET_HEADS, RET_DK, RET_DV), F32)],
        compiler_params=_cparams(("parallel", "arbitrary")),
        name="retention",
    )(cd, rq, rk, rv, rg, cosf, sinf, dmat, qd, kd, g_ret.reshape(1, w), s0)


def _t5_bucket(rel):
    nb = NUM_BUCKETS // 2
    max_exact = nb // 2
    base = jnp.where(rel > 0, nb, 0)
    n = jnp.abs(rel)
    nf = jnp.maximum(n, 1).astype(F32)
    large = max_exact + (jnp.log(nf / max_exact) / math.log(MAX_DISTANCE / max_exact) * (nb - max_exact)).astype(I32)
    large = jnp.minimum(large, nb - 1)
    return base + jnp.where(n < max_exact, n, large)


def _band_kernel(rb_ref, bucket_ref, band_ref):
    bucket = bucket_ref[...]
    for h in range(ATT_HEADS):
        acc = jnp.zeros(bucket.shape, F32)
        for j in range(NUM_BUCKETS):
            acc = jnp.where(bucket == j, rb_ref[j, h], acc)
        band_ref[h] = acc


def _bias_band(rel_bias):
    c = jnp.arange(BAND_TILES * LANES, dtype=I32)[:, None]
    t = jnp.arange(LANES, dtype=I32)[None, :]
    bucket = _t5_bucket(c - (BAND_TILES - 1) * LANES - t)
    return pl.pallas_call(
        _band_kernel,
        in_specs=[pl.BlockSpec(memory_space=pltpu.SMEM), pl.BlockSpec(memory_space=pltpu.VMEM)],
        out_specs=pl.BlockSpec(memory_space=pltpu.VMEM),
        out_shape=jax.ShapeDtypeStruct((ATT_HEADS, BAND_TILES * LANES, LANES), F32),
        name="bias_band",
    )(rel_bias, bucket)


def _order_key(s):
    bits = lax.bitcast_convert_type(s, I32)
    return bits ^ ((bits >> 31) & 0x7FFFFFFF)


def _fold8(x, op=jnp.add):
    parts = [x[i * SUBLANES:(i + 1) * SUBLANES] for i in range(x.shape[0] // SUBLANES)]
    while len(parts) > 1:
        parts = [op(parts[i], parts[i + 1]) for i in range(0, len(parts), 2)]
    return parts[0]


GROUP = LANES // SUBLANES


def _sort_network(n):
    pairs = []
    p = 1
    while p < n:
        k = p
        while k >= 1:
            for j in range(k % p, n - k, 2 * k):
                for i in range(min(k, n - j - k)):
                    if (i + j) // (2 * p) == (i + j + k) // (2 * p):
                        pairs.append((i + j, i + j + k))
            k //= 2
        p *= 2
    return pairs


def _sort_desc(vals):
    vals = list(vals)
    for a, b in _sort_network(len(vals)):
        vals[a], vals[b] = jnp.maximum(vals[a], vals[b]), jnp.minimum(vals[a], vals[b])
    return vals


def _tile_loop(nt, trip, carry):
    def run(first, trips, width, carry):
        return lax.fori_loop(0, trips, lambda i, c: trip([first + i * width + u for u in range(width)], c), carry)
    carry = run(0, nt // 4, 4, carry)
    carry = run(nt // 4 * 4, (nt // 2) % 2, 2, carry)
    return run(nt // 2 * 2, nt % 2, 1, carry)


def _attention_kernel(q_ref, iq_ref, tail_ref, k_ref, vt_ref, ki_ref, band_ref, adm_ref, o_ref,
                      iqs_s, key_s, ks_s, mask_s, log_s, acc_s, *, jd0, n_sel, idx_bits):
    tq = q_ref.shape[1]
    qb = tq // LANES
    jd = jd0 + qb * pl.program_id(1) + qb - 1
    nt = jd + 1
    classes = band_ref.shape[1] // LANES
    krow = lax.broadcasted_iota(I32, (LANES, tq), 0)

    qt = q_ref[...]
    iqt = iq_ref[...]
    wt = tail_ref[IDX_DIM:IDX_DIM + IDX_HEADS, :] * (IDX_HEADS ** -0.5) * (IDX_DIM ** -0.5)
    zpad = jnp.zeros((LANES - IDX_DIM, tq), BF16)
    for h in range(IDX_HEADS):
        iqs_s[:, h * tq:(h + 1) * tq] = jnp.concatenate([iqt[h * IDX_DIM:(h + 1) * IDX_DIM], zpad], axis=0)

    def tile_rows(j):
        return pl.ds(pl.multiple_of(j * LANES, LANES), LANES)

    def class_rows(j):
        return tile_rows(jnp.clip(j - jd + classes - 1, 0, classes - 1))

    def score_tile(j):
        d = jnp.dot(ki_ref[tile_rows(j), :], iqs_s[...], preferred_element_type=F32)
        acc = jnp.zeros((LANES, tq), F32)
        for h in range(IDX_HEADS):
            acc = acc + wt[h:h + 1] * jnp.maximum(d[:, h * tq:(h + 1) * tq], 0.0)
        return acc

    def score_trip(js, carry):
        for j in js:
            key = jnp.where(adm_ref[class_rows(j), :] > 0.0, _order_key(score_tile(j)), _f32_key_const(NEG_INF))
            key_s[j] = key
            srt = _sort_desc([key[g * SUBLANES:(g + 1) * SUBLANES] for g in range(GROUP)])
            for g in range(GROUP):
                ks_s[j, g * SUBLANES:(g + 1) * SUBLANES] = srt[g]
        return carry

    _tile_loop(nt, score_trip, 0)
    nquad = (nt + 3) // 4

    def fill_body(j, carry):
        ks_s[j] = jnp.full((LANES, tq), INT_MIN, I32)
        return carry

    lax.fori_loop(nt, nquad * 4, fill_body, 0)

    def count(pred_tile):
        def trip(js, acc):
            for j in js:
                acc = acc + _fold8(pred_tile(j))
            return acc
        return jnp.sum(_tile_loop(nt, trip, jnp.zeros((SUBLANES, tq), F32)), axis=0, keepdims=True)

    def count_ge(cand):
        def one(m):
            return jnp.where(m, 1.0, 0.0)

        def trip(q, accs):
            a8, a4, a2, a1 = accs
            for u in range(4):
                v = [ks_s[4 * q + u, g * SUBLANES:(g + 1) * SUBLANES] for g in range(GROUP)]
                t1 = v[7] >= cand
                t2 = jnp.where(t1, v[11], v[3]) >= cand
                t3 = jnp.where(t1, jnp.where(t2, v[13], v[9]), jnp.where(t2, v[5], v[1])) >= cand
                t4 = jnp.where(t1, jnp.where(t2, jnp.where(t3, v[14], v[12]), jnp.where(t3, v[10], v[8])),
                               jnp.where(t2, jnp.where(t3, v[6], v[4]), jnp.where(t3, v[2], v[0]))) >= cand
                a8, a4, a2 = a8 + one(t1), a4 + one(t2), a2 + one(t3)
                a1 = a1 + one(t4) + one(v[15] >= cand)
            return a8, a4, a2, a1

        zero = jnp.zeros((SUBLANES, tq), F32)
        a8, a4, a2, a1 = lax.fori_loop(0, nquad, trip, (zero, zero, zero, zero))
        return jnp.sum(8.0 * a8 + 4.0 * a4 + 2.0 * a2 + a1, axis=0, keepdims=True)

    kf = float(n_sel)

    def bit_body(it, thr):
        cand = thr + lax.shift_left(jnp.int32(1), 31 - it)
        return jnp.where(count_ge(cand) >= kf, cand, thr)

    thr = lax.fori_loop(0, 32, bit_body, jnp.full((1, tq), INT_MIN, I32))

    need = kf - count(lambda j: jnp.where(key_s[j] > thr, 1.0, 0.0))
    n_tied = count(lambda j: jnp.where(key_s[j] == thr, 1.0, 0.0))
    surplus = jnp.where(thr > HALF_NEG_KEY, n_tied - need, 0.0)

    def tie_search():
        def tie_body(it, j0):
            cand = j0 + lax.shift_left(jnp.int32(1), idx_bits - 1 - it)
            cnt = count(lambda j: jnp.where(key_s[j] == thr, jnp.where(krow + j * LANES < cand, 1.0, 0.0), 0.0))
            return jnp.where(cnt < need, cand, j0)
        return lax.fori_loop(0, idx_bits, tie_body, jnp.zeros((1, tq), I32))

    j0 = lax.cond(jnp.max(surplus) > 0.0, tie_search, lambda: jnp.full((1, tq), INT_MAX, I32))

    def mask_trip(js, carry):
        for j in js:
            key = key_s[j]
            sel = jnp.where(key > thr, 1.0, jnp.where(key == thr, jnp.where(krow + j * LANES <= j0, 1.0, 0.0), 0.0))
            sel = jnp.where(key > HALF_NEG_KEY, sel, 0.0)
            mask_s[j] = jnp.where(sel > 0.0, 0.0, NEG_INF)
        return carry

    _tile_loop(nt, mask_trip, 0)

    scale = ATT_HEAD_DIM ** -0.5
    heads = range(ATT_HEADS)
    qh = [qt[h * ATT_HEAD_DIM:(h + 1) * ATT_HEAD_DIM] for h in heads]

    def log_trip(js, mx):
        mx = list(mx)
        for j in js:
            band_rows = class_rows(j)
            msk = mask_s[j]
            for h in heads:
                kh = k_ref[tile_rows(j), h * ATT_HEAD_DIM:(h + 1) * ATT_HEAD_DIM]
                lg = jnp.dot(kh, qh[h], preferred_element_type=F32) * scale + band_ref[h, band_rows, :] + msk
                log_s[h, j] = lg
                mx[h] = jnp.maximum(mx[h], _fold8(lg, jnp.maximum))
        return tuple(mx)

    mx = _tile_loop(nt, log_trip, tuple(jnp.full((SUBLANES, tq), NEG_INF, F32) for _ in heads))
    m = [jnp.max(mx[h], axis=0, keepdims=True) for h in heads]
    acc_s[...] = jnp.zeros(acc_s.shape, F32)

    def pv_trip(js, ls):
        ls = list(ls)
        for h in heads:
            acc = acc_s[h]
            for j in js:
                p = jnp.exp(log_s[h, j] - m[h])
                ls[h] = ls[h] + _fold8(p)
                vth = vt_ref[j, h * ATT_HEAD_DIM:(h + 1) * ATT_HEAD_DIM, :]
                acc = acc + jnp.dot(vth, p.astype(BF16), preferred_element_type=F32)
            acc_s[h] = acc
        return tuple(ls)

    ls = _tile_loop(nt, pv_trip, tuple(jnp.zeros((SUBLANES, tq), F32) for _ in heads))
    for h in heads:
        ot = acc_s[h] / jnp.sum(ls[h], axis=0, keepdims=True)
        for u in range(qb):
            o_ref[u * LANES:(u + 1) * LANES, h * ATT_HEAD_DIM:(h + 1) * ATT_HEAD_DIM] = jnp.transpose(
                ot[:, u * LANES:(u + 1) * LANES]).astype(o_ref.dtype)


def _attention(aqt, iqt, tailt, keys_k, keys_v, keys_i, band, b, t, past_len):
    l = past_len + t
    tq = LANES
    nqb = -(-t // tq)
    tp = nqb * tq
    assert past_len % LANES == 0 and (t % tq == 0 or nqb == 1)
    jd0 = past_len // LANES
    ntiles = jd0 + nqb
    ntp = ntiles
    lp = ntp * LANES
    n_sel = min(TOPK_MAX, l // 4)
    if tp != t:
        padq = lambda a: jnp.pad(a.reshape(-1, b, t), ((0, 0), (0, 0), (0, tp - t))).reshape(-1, b * tp)
        aqt, iqt, tailt = padq(aqt), padq(iqt), padq(tailt)
    w = ATT_HEADS * ATT_HEAD_DIM
    pad = ((0, 0), (0, lp - l), (0, 0))
    kk = jnp.pad(keys_k, pad)
    vt = jnp.pad(keys_v, pad).reshape(b, ntp, LANES, w).transpose(0, 1, 3, 2)
    ki2 = jnp.pad(keys_i, ((0, 0), (0, lp - l), (0, LANES - IDX_DIM)))
    cc = jnp.arange(LANES)[:, None]
    tt = jnp.arange(LANES)[None, :]
    adm = ((cc // CHUNK <= tt // CHUNK) & (cc < l - (ntiles - 1) * LANES)).astype(F32)
    qb = 2 if nqb % 2 == 0 else 1
    wq = qb * LANES
    classes = BAND_TILES + qb - 1
    band_w, adm_w = [], []
    for r in range(classes):
        rel = [r - (classes - 1) + (qb - 1 - u) for u in range(qb)]
        band_w.append(jnp.concatenate(
            [band[:, min(max(x + BAND_TILES - 1, 0), BAND_TILES - 1) * LANES:][:, :LANES] for x in rel], axis=2))
        adm_w.append(jnp.concatenate(
            [adm if x == 0 else jnp.full((LANES, LANES), 1.0 if x < 0 else 0.0, F32) for x in rel], axis=1))
    band_w = jnp.concatenate(band_w, axis=1)
    adm_w = jnp.concatenate(adm_w, axis=0)
    nsteps = nqb // qb
    qspec = lambda width: pl.BlockSpec((wq, width), lambda bi, qi: (bi * nsteps + qi, 0))
    qtspec = lambda rows: pl.BlockSpec((rows, wq), lambda bi, qi: (0, bi * nsteps + qi))
    kspec = lambda width: pl.BlockSpec((None, lp, width), lambda bi, qi: (bi, 0, 0))
    kern = functools.partial(_attention_kernel, jd0=jd0, n_sel=n_sel, idx_bits=max((lp - 1).bit_length(), 1))
    out = pl.pallas_call(
        kern,
        grid=(b, nsteps),
        in_specs=[qtspec(w), qtspec(IDX_HEADS * IDX_DIM), qtspec(LANES),
                  kspec(w), pl.BlockSpec((None, ntp, w, LANES), lambda bi, qi: (bi, 0, 0, 0)), kspec(LANES),
                  pl.BlockSpec((ATT_HEADS, classes * LANES, wq), lambda bi, qi: (0, 0, 0)),
                  pl.BlockSpec((classes * LANES, wq), lambda bi, qi: (0, 0))],
        out_specs=qspec(w),
        out_shape=jax.ShapeDtypeStruct((b * tp, w), BF16),
        scratch_shapes=[pltpu.VMEM((LANES, IDX_HEADS * wq), BF16),
                        pltpu.VMEM((ntp, LANES, wq), I32), pltpu.VMEM((-(-ntp // 4) * 4, LANES, wq), I32),
                        pltpu.VMEM((ntp, LANES, wq), F32),
                        pltpu.VMEM((ATT_HEADS, ntp, LANES, wq), F32), pltpu.VMEM((ATT_HEADS, ATT_HEAD_DIM, wq), F32)],
        compiler_params=_cparams(("parallel", "arbitrary")),
        name="attention",
    )(aqt, iqt, tailt, kk, vt, ki2, band_w, adm_w)
    if tp != t:
        out = out.reshape(b, tp, w)[:, :t].reshape(b * t, w)
    return out


def _store_tile_major(ref, x):
    rows, width = x.shape
    nk = width // LANES
    for c in range(nk):
        ref[pl.ds(c, rows, stride=nk), :] = x[:, c * LANES:(c + 1) * LANES]


def _load_tile_major(ref, rows, nk, first=0, stride=None):
    stride = nk if stride is None else stride
    return jnp.concatenate([ref[pl.ds(first + c, rows, stride=stride), :] for c in range(nk)], axis=1)


def _outproj_kernel(x_ref, oret_ref, oatt_ref, wo_ref, g_ref, wr_ref, br_ref, *rest, own_steps):
    outs = rest[-4:]
    step = pl.program_id(0)

    @pl.when(step < own_steps)
    def _():
        _outproj_body(x_ref, oret_ref, oatt_ref, wo_ref, g_ref, wr_ref, br_ref, *outs)

    @pl.when(step >= own_steps)
    def _():
        outs[1][...] = jnp.zeros(outs[1].shape, F32)


def _outproj_body(x_ref, oret_ref, oatt_ref, wo_ref, g_ref, wr_ref, br_ref, x1_ref, h2_ref, e_ref, gt_ref):
    mixed = jnp.concatenate([oret_ref[...], oatt_ref[...]], axis=1)
    x1 = x_ref[...] + jnp.dot(mixed, wo_ref[...], preferred_element_type=F32)
    x1_ref[...] = x1
    h2 = (x1 * lax.rsqrt(jnp.mean(jnp.square(x1), -1, keepdims=True) + EPS)) * g_ref[...]
    _store_tile_major(h2_ref, h2)
    lg = jnp.dot(h2.astype(BF16), wr_ref[...], preferred_element_type=F32) + br_ref[...]
    tm = lg.shape[0]
    gl = lg[:, :N_GROUPS]
    gmax = jnp.max(gl, -1, keepdims=True)
    p_top = 1.0 / jnp.sum(jnp.exp(gl - gmax), -1, keepdims=True)
    gi = lax.broadcasted_iota(I32, (tm, N_GROUPS), 1).astype(F32)
    g_top = jnp.min(jnp.where(gl == gmax, gi, float(N_GROUPS)), -1, keepdims=True)
    el = jnp.zeros((tm, EXPERTS_PER_GROUP), F32)
    for g in range(N_GROUPS):
        lo = N_GROUPS + g * EXPERTS_PER_GROUP
        el = jnp.where(g_top == float(g), lg[:, lo:lo + EXPERTS_PER_GROUP], el)
    ei = lax.broadcasted_iota(I32, (tm, EXPERTS_PER_GROUP), 1).astype(F32)
    v1 = jnp.max(el, -1, keepdims=True)
    i1 = jnp.min(jnp.where(el == v1, ei, float(EXPERTS_PER_GROUP)), -1, keepdims=True)
    el2 = jnp.where(ei == i1, -jnp.inf, el)
    v2 = jnp.max(el2, -1, keepdims=True)
    i2 = jnp.min(jnp.where(el2 == v2, ei, float(EXPERTS_PER_GROUP)), -1, keepdims=True)
    e2 = jnp.exp(v2 - v1)
    den = 1.0 + e2
    two = lax.broadcasted_iota(I32, (tm, TOP_E), 1)
    e_ref[...] = (g_top * EXPERTS_PER_GROUP + jnp.where(two == 0, i1, i2)).astype(I32)
    gt_ref[...] = jnp.where(two == 0, 1.0 / den, e2 / den) * p_top


def _outproj_router(x2, o_ret, o_att, w_out, g_ffn, w_group, b_group, w_er, b_er, h2_all, row0, rows_all):
    n, d = x2.shape
    shared = () if h2_all is None else (h2_all,)
    tm = _row_tile(n, 512)
    other = 0 if shared else rows_all - n
    if other:
        assert row0 == 0
        tm = math.gcd(tm, other)
    assert row0 % tm == 0
    blk0 = row0 // tm
    own_steps = n // tm
    nk = d // LANES
    nrt = N_GROUPS + N_EXPERTS
    wr = jnp.pad(jnp.concatenate([w_group, w_er.reshape(d, N_EXPERTS)], 1), ((0, 0), (0, LANES - nrt))).astype(BF16)
    br = jnp.pad(jnp.concatenate([b_group, b_er.reshape(N_EXPERTS)]), (0, LANES - nrt)).reshape(1, LANES)
    rspec = lambda w: pl.BlockSpec((tm, w), lambda i: (jnp.minimum(i, own_steps - 1), 0))
    cspec = lambda r, c: pl.BlockSpec((r, c), lambda i: (0, 0))
    mw = w_out.shape[0]
    return pl.pallas_call(
        functools.partial(_outproj_kernel, own_steps=own_steps),
        grid=(own_steps + other // tm,),
        in_specs=[rspec(d), rspec(o_ret.shape[1]), rspec(o_att.shape[1]), cspec(mw, d), cspec(1, d),
                  cspec(d, LANES), cspec(1, LANES)] + [pl.BlockSpec(memory_space=pl.ANY) for _ in shared],
        out_specs=[rspec(d), pl.BlockSpec((tm * nk, LANES), lambda i: (blk0 + i, 0)), rspec(TOP_E), rspec(TOP_E)],
        out_shape=[jax.ShapeDtypeStruct((n, d), F32), jax.ShapeDtypeStruct((rows_all * nk, LANES), F32),
                   jax.ShapeDtypeStruct((n, TOP_E), I32), jax.ShapeDtypeStruct((n, TOP_E), F32)],
        input_output_aliases={7: 1} if shared else {},
        compiler_params=_cparams(("arbitrary",)),
        name="outproj_router",
    )(x2, o_ret, o_att, w_out.astype(BF16), g_ffn.reshape(1, d), wr, br, *shared)


def _expert_kernel(blk_e_ref, tok0_ref, tokn_ref, slot_ref, h_ref, wg_ref, wu_ref, wd_ref, o_ref,
                   xbuf, obuf, gsem, ssem):
    i = pl.program_id(0)
    nb = pl.num_programs(0)
    nk = wg_ref.shape[0] // LANES
    rb = xbuf.shape[0] // (2 * nk)
    blk = rb * nk
    cur = i % 2
    nxt = 1 - cur

    def lines(first, count):
        return pl.ds(pl.multiple_of(first, nk), count)

    def for_rows(start_row):
        def body(g, carry):
            for u in range(DMA_UNROLL):
                start_row(g * DMA_UNROLL + u, u % 2)
            return carry
        lax.fori_loop(0, rb // DMA_UNROLL, body, 0)

    def start_gather(tok_ref, buf):
        for_rows(lambda r, pri: pltpu.make_async_copy(
            h_ref.at[lines(tok_ref[0, 0, r], nk)], xbuf.at[lines(buf * blk + r * nk, nk)],
            gsem.at[buf]).start(priority=pri))

    def wait_gather(buf):
        pltpu.make_async_copy(h_ref.at[pl.ds(0, blk)], xbuf.at[lines(buf * blk, blk)], gsem.at[buf]).wait()

    def wait_scatter(buf):
        pltpu.make_async_copy(obuf.at[lines(buf * blk, blk)], o_ref.at[pl.ds(0, blk)], ssem.at[buf]).wait()

    @pl.when(i == 0)
    def _():
        start_gather(tok0_ref, 0)

    @pl.when(i + 1 < nb)
    def _():
        start_gather(tokn_ref, nxt)

    wait_gather(cur)
    base = cur * blk
    xb = jnp.concatenate([xbuf[pl.ds(base + c, rb, stride=nk), :] for c in range(nk)], axis=1).astype(BF16)
    hid = jax.nn.silu(jnp.dot(xb, wg_ref[...].astype(BF16), preferred_element_type=F32))
    hid = hid * jnp.dot(xb, wu_ref[...].astype(BF16), preferred_element_type=F32)
    out = jnp.dot(hid.astype(BF16), wd_ref[...].astype(BF16), preferred_element_type=F32)

    @pl.when(i >= 2)
    def _():
        wait_scatter(cur)

    for c in range(nk):
        obuf[pl.ds(base + c, rb, stride=nk), :] = out[:, c * LANES:(c + 1) * LANES]

    for_rows(lambda r, pri: pltpu.make_async_copy(
        obuf.at[lines(base + r * nk, nk)], o_ref.at[lines(slot_ref[0, 0, r], nk)], ssem.at[cur]).start(priority=pri))

    @pl.when(i == nb - 1)
    def _():
        wait_scatter(cur)

        @pl.when(nb >= 2)
        def _():
            wait_scatter(nxt)


def _experts(h2, src_tok, out_slot, blk_e, w_gate, w_up, w_down, rb):
    d, de = w_gate.shape[1:]
    nk = d // LANES
    rows = src_tok.shape[0]
    nb = rows // rb
    idx3 = lambda a: (a * nk).reshape(nb, 1, rb)
    ispec = lambda f: pl.BlockSpec((1, 1, rb), f, memory_space=pltpu.SMEM)
    wspec = lambda r, c: pl.BlockSpec((None, r, c), lambda i, be: (be[i], 0, 0))
    return pl.pallas_call(
        _expert_kernel,
        grid_spec=pltpu.PrefetchScalarGridSpec(
            num_scalar_prefetch=1,
            grid=(nb,),
            in_specs=[ispec(lambda i, be: (0, 0, 0)), ispec(lambda i, be: (jnp.minimum(i + 1, nb - 1), 0, 0)),
                      ispec(lambda i, be: (i, 0, 0)), pl.BlockSpec(memory_space=pl.ANY),
                      wspec(d, de), wspec(d, de), wspec(de, d)],
            out_specs=pl.BlockSpec(memory_space=pl.ANY),
            scratch_shapes=[pltpu.VMEM((2 * rb * nk, LANES), F32), pltpu.VMEM((2 * rb * nk, LANES), F32),
                            pltpu.SemaphoreType.DMA((2,)), pltpu.SemaphoreType.DMA((2,))],
        ),
        out_shape=jax.ShapeDtypeStruct((rows * nk, LANES), F32),
        compiler_params=_cparams(("arbitrary",)),
        name="experts",
    )(blk_e, idx3(src_tok), idx3(src_tok), idx3(out_slot), h2, w_gate, w_up, w_down)


def _route_plan(expert, rb):
    n = expert.shape[0]
    a = n * TOP_E
    flat_e = expert.reshape(a)
    counts = jnp.sum((flat_e[:, None] == jnp.arange(N_EXPERTS, dtype=I32)[None, :]).astype(I32), axis=0)
    padded = (counts + rb - 1) // rb * rb
    ends = jnp.cumsum(padded)
    n_blocks = (a + N_EXPERTS * (rb - 1) + rb - 1) // rb
    rows = n_blocks * rb
    q = jnp.arange(rb - 1, dtype=I32)[None, :]
    e = jnp.arange(N_EXPERTS, dtype=I32)[:, None]
    pad_key = jnp.where(q < (padded - counts)[:, None], 2 * e + 1, 2 * N_EXPERTS).reshape(-1)
    spare = jnp.full((rows - a - N_EXPERTS * (rb - 1),), 2 * N_EXPERTS, I32)
    perm = jnp.argsort(jnp.concatenate([2 * flat_e, pad_key, spare]), stable=True).astype(I32)
    src_tok = jnp.where(perm < a, perm // TOP_E, 0)
    out_slot = perm
    blk_e = jnp.minimum(jnp.sum((ends[None, :] <= jnp.arange(n_blocks, dtype=I32)[:, None] * rb).astype(I32), axis=1),
                        N_EXPERTS - 1)
    return src_tok, out_slot, blk_e


def _combine_kernel(x1_ref, eo_ref, gt_ref, g_ref, y_ref):
    tm, d = x1_ref.shape
    nk = d // LANES
    gt = gt_ref[...]
    moe = _load_tile_major(eo_ref, tm, nk, 0, TOP_E * nk) * gt[:, 0:1]
    moe = moe + _load_tile_major(eo_ref, tm, nk, nk, TOP_E * nk) * gt[:, 1:2]
    x = x1_ref[...] + moe
    y_ref[...] = (x * lax.rsqrt(jnp.mean(jnp.square(x), -1, keepdims=True) + EPS)) * g_ref[...]


def _combine(x1, eo2, gate, g_final, row0):
    n, d = x1.shape
    tm = _row_tile(n, 512)
    assert row0 % tm == 0
    blk0 = row0 // tm
    nk = d // LANES
    rspec = lambda w: pl.BlockSpec((tm, w), lambda i: (i, 0))
    return pl.pallas_call(
        _combine_kernel,
        grid=(n // tm,),
        in_specs=[rspec(d), pl.BlockSpec((tm * TOP_E * nk, LANES), lambda i: (blk0 + i, 0)), rspec(TOP_E),
                  pl.BlockSpec((1, d), lambda i: (0, 0))],
        out_specs=rspec(d),
        out_shape=jax.ShapeDtypeStruct((n, d), F32),
        compiler_params=_cparams(("parallel",)),
        name="combine",
    )(x1, eo2, gate, g_final.reshape(1, d))


def _mixer(x, past_len, s_ret, past_k, past_v, past_ki, band, params, h2_all, row0, rows_all):
    (rel_bias, g_mix, w_in, g_ret, w_out, g_ffn, w_group, b_group, w_er, b_er, w_gate, w_up, w_down) = params
    b, t, d = x.shape
    n = b * t
    x2 = x.reshape(n, d)
    rq, rk, rv, rg, ak, av, akb, avb, ik, aqt, iqt, tailt = _inproj(x2, g_mix, w_in)
    o_ret, s_new = _retention(rq, rk, rv, rg, s_ret, g_ret, b, t, past_len)
    aw = ATT_HEADS * ATT_HEAD_DIM
    keys_k = jnp.concatenate([past_k.reshape(b, past_len, aw).astype(BF16), akb.reshape(b, t, aw)], 1)
    keys_v = jnp.concatenate([past_v.reshape(b, past_len, aw).astype(BF16), avb.reshape(b, t, aw)], 1)
    keys_i = jnp.concatenate([past_ki.astype(BF16), ik.reshape(b, t, IDX_DIM).astype(BF16)], 1)
    o_att = _attention(aqt, iqt, tailt, keys_k, keys_v, keys_i, band, b, t, past_len)
    x1, h2_all, expert, gate = _outproj_router(x2, o_ret, o_att, w_out, g_ffn, w_group, b_group, w_er, b_er,
                                               h2_all, row0, rows_all)
    caches = (ak.reshape(1, b, t, ATT_HEADS, ATT_HEAD_DIM), av.reshape(1, b, t, ATT_HEADS, ATT_HEAD_DIM),
              ik.reshape(1, b, t, IDX_DIM), s_new[None])
    return x1, h2_all, expert, gate, caches


def kernel(x_prompt, x_sample, cache_attn_k, cache_attn_v, cache_idx_k, state_ret, rel_bias, g_mix, w_in, g_ret, w_out, g_ffn, w_group, b_group, w_expert_router, b_expert_router, w_gate, w_up, w_down, g_final):
    assert g_mix.shape[0] == 1, "single-layer model"
    params = (rel_bias, g_mix[0], w_in[0], g_ret[0], w_out[0], g_ffn[0], w_group[0], b_group[0],
              w_expert_router[0], b_expert_router[0], w_gate[0], w_up[0], w_down[0])
    band = _bias_band(rel_bias)
    nb = x_prompt.shape[0]
    past_len = cache_attn_k.shape[2]
    dt = x_prompt.dtype
    empty_kv = jnp.zeros((nb, 0, ATT_HEADS, ATT_HEAD_DIM), dt)
    empty_ki = jnp.zeros((nb, 0, IDX_DIM), dt)
    s0 = jnp.zeros((nb, RET_HEADS, RET_DK, RET_DV), dt)
    n_p = x_prompt.shape[0] * x_prompt.shape[1]
    n_s = x_sample.shape[0] * x_sample.shape[1]
    x1p, h2_all, ep, gp, caches_p = _mixer(x_prompt, 0, s0, empty_kv, empty_kv, empty_ki, band, params,
                                           None, 0, n_p + n_s)
    x1s, h2_all, es, gs, caches_s = _mixer(x_sample, past_len, state_ret[0], cache_attn_k[0], cache_attn_v[0],
                                           cache_idx_k[0], band, params, h2_all, n_p, n_p + n_s)
    rb = 256 if n_p + n_s >= 8192 else 128
    src_tok, out_slot, blk_e = _route_plan(jnp.concatenate([ep, es], 0), rb)
    eo = _experts(h2_all, src_tok, out_slot, blk_e, w_gate[0], w_up[0], w_down[0], rb)
    yp = _combine(x1p, eo, gp, g_final, 0).reshape(x_prompt.shape)
    ys = _combine(x1s, eo, gs, g_final, n_p).reshape(x_sample.shape)
    return (yp, ys) + caches_p + caches_s
```

```python
import functools
import math

import jax
import jax.numpy as jnp
import numpy as np
from jax import lax
from jax.experimental import pallas as pl
from jax.experimental.pallas import tpu as pltpu

F32 = jnp.float32
BF16 = jnp.bfloat16
I32 = jnp.int32

CHUNK = 64
RET_HEADS = 4
RET_DK = 128
RET_DV = 128
ROPE_BASE = 10000.0
ATT_HEADS = 4
ATT_HEAD_DIM = 128
IDX_HEADS = 8
IDX_DIM = 64
TOPK_MAX = 256
NUM_BUCKETS = 32
MAX_DISTANCE = 128
N_GROUPS = 4
EXPERTS_PER_GROUP = 8
N_EXPERTS = N_GROUPS * EXPERTS_PER_GROUP
TOP_E = 2
EPS = 1e-6
NEG_INF = -1e30

LANES = 128
SUBLANES = 8
VMEM_LIMIT = 56 * 1024 * 1024
RET_CHUNK = 256
DMA_UNROLL = 8
INT_MIN = -(2 ** 31)
INT_MAX = 2 ** 31 - 1
BAND_TILES = 3
assert (BAND_TILES - 2) * LANES + 1 >= MAX_DISTANCE
assert math.log(IDX_DIM, 4).is_integer()


def _f32_key_const(v):
    b = int(np.array(v, np.float32).view(np.int32))
    return b ^ ((b >> 31) & 0x7FFFFFFF)


HALF_NEG_KEY = _f32_key_const(0.5 * NEG_INF)


def _cparams(sem):
    return pltpu.CompilerParams(dimension_semantics=sem, vmem_limit_bytes=VMEM_LIMIT)


def _row_tile(n, want=256):
    return want if n % want == 0 else n


def _inproj_kernel(x_ref, g_ref, wm_ref, wt_ref, wqt_ref, wiqt_ref, wtt_ref,
                   rq, rk, rv, rg, ak, av, akb, avb, ik, aqt, iqt, tailt):
    x = x_ref[...]
    h = (x * lax.rsqrt(jnp.mean(jnp.square(x), -1, keepdims=True) + EPS)) * g_ref[...]
    hb = h.astype(BF16)

    def proj(i):
        return jnp.dot(hb, wm_ref[:, i * 512:(i + 1) * 512], preferred_element_type=F32)

    def proj_t(wt_rows):
        return lax.dot_general(wt_rows, hb, (((1,), (1,)), ((), ())), preferred_element_type=F32)

    rq[...] = proj(0)
    rk[...] = proj(1)
    rv[...] = proj(2).astype(BF16)
    rg[...] = proj(3)
    aqt[...] = proj_t(wqt_ref[...]).astype(BF16)
    k = proj(5)
    _store_tile_major(ak, k)
    akb[...] = k.astype(BF16)
    v = proj(6)
    _store_tile_major(av, v)
    avb[...] = v.astype(BF16)
    iqt[...] = proj_t(wiqt_ref[...]).astype(BF16)
    ik[...] = jnp.dot(hb, wt_ref[...], preferred_element_type=F32)[:, :IDX_DIM]
    tailt[...] = proj_t(wtt_ref[...])


def _inproj(x2, g_mix, w_in):
    n, d = x2.shape
    tm = _row_tile(n)
    wm = w_in[:, :4096].astype(BF16)
    wt = jnp.pad(w_in[:, 4096:], ((0, 0), (0, LANES - (w_in.shape[1] - 4096)))).astype(BF16)
    rspec = lambda w, lines=1: pl.BlockSpec((tm * lines, w), lambda i: (i, 0))
    outs = [(512, F32, 1), (512, F32, 1), (512, BF16, 1), (512, F32, 1),
            (LANES, F32, ATT_HEADS), (LANES, F32, ATT_HEADS),
            (512, BF16, 1), (512, BF16, 1), (IDX_DIM, F32, 1)]
    outs_t = [(512, BF16), (512, BF16), (LANES, F32)]
    whole = lambda a: pl.BlockSpec(a.shape, lambda i: (0, 0))
    wqt, wiqt, wtt = wm[:, 4 * 512:5 * 512].T, wm[:, 7 * 512:8 * 512].T, wt.T
    return pl.pallas_call(
        _inproj_kernel,
        grid=(n // tm,),
        in_specs=[rspec(d), pl.BlockSpec((1, d), lambda i: (0, 0)), whole(wm), whole(wt),
                  whole(wqt), whole(wiqt), whole(wtt)],
        out_specs=[rspec(w, ln) for w, _, ln in outs] + [pl.BlockSpec((w, tm), lambda i: (0, i)) for w, _ in outs_t],
        out_shape=[jax.ShapeDtypeStruct((n * ln, w), dt) for w, dt, ln in outs]
        + [jax.ShapeDtypeStruct((w, n), dt) for w, dt in outs_t],
        compiler_params=_cparams(("parallel",)),
        name="inproj",
    )(x2, g_mix.reshape(1, d), wm, wt, wqt, wiqt, wtt)


def _retention_kernel(cd_ref, rq_ref, rk_ref, rv_ref, rg_ref, cos_ref, sin_ref, dmat_ref, qd_ref, kd_ref,
                      gret_ref, s0_ref, o_ref, s_ref):
    c = pl.program_id(1)

    @pl.when(c == 0)
    def _():
        s_ref[...] = s0_ref[...]

    cosf = cos_ref[...]
    sinf = sin_ref[...]
    half = RET_DK // 2

    def rot(x):
        return x * cosf + pltpu.roll(x, half, 1) * sinf

    for h in range(RET_HEADS):
        sl = slice(h * RET_DK, (h + 1) * RET_DK)
        q = rot(rq_ref[:, sl])
        k = rot(rk_ref[:, sl]) * (RET_DK ** -0.5)
        v = rv_ref[:, sl]
        s = s_ref[0, h]
        sc = lax.dot_general(q.astype(BF16), k.astype(BF16), (((1,), (1,)), ((), ())),
                             preferred_element_type=F32) * dmat_ref[h]
        o = jnp.dot(sc.astype(BF16), v, preferred_element_type=F32)
        o = o + jnp.dot((q * qd_ref[:, sl]).astype(BF16), s.astype(BF16), preferred_element_type=F32)
        kdt = jnp.transpose(k * kd_ref[:, sl]).astype(BF16)
        s_ref[0, h] = cd_ref[h] * s + jnp.dot(kdt, v, preferred_element_type=F32)
        mu = jnp.mean(o, -1, keepdims=True)
        var = jnp.mean(jnp.square(o - mu), -1, keepdims=True)
        on = (o - mu) * lax.rsqrt(var + EPS) * gret_ref[:, sl]
        o_ref[:, sl] = (jax.nn.silu(rg_ref[:, sl]) * on).astype(o_ref.dtype)


def _retention(rq, rk, rv, rg, s0, g_ret, b, t, past_len):
    cl = min(RET_CHUNK, t)
    nc = t // cl
    half = RET_DK // 2
    pos = (past_len + jnp.arange(t)).astype(F32)
    inv = ROPE_BASE ** (-jnp.arange(half, dtype=F32) / half)
    ang = pos[:, None] * inv[None, :]
    cosf = jnp.concatenate([jnp.cos(ang), jnp.cos(ang)], -1)
    sinf = jnp.concatenate([-jnp.sin(ang), jnp.sin(ang)], -1)
    log_g = jnp.log1p(-jnp.exp2(-5.0 - jnp.arange(RET_HEADS, dtype=F32)))
    i = jnp.arange(cl, dtype=F32)
    diff = i[:, None] - i[None, :]
    dmat = jnp.where(diff[None] >= 0, jnp.exp(jnp.maximum(diff, 0.0)[None] * log_g[:, None, None]), 0.0)
    kd = jnp.repeat(jnp.exp((cl - 1.0 - i)[:, None] * log_g[None, :]), RET_DK, axis=1)
    qd = jnp.repeat(jnp.exp((i + 1.0)[:, None] * log_g[None, :]), RET_DK, axis=1)
    cd = jnp.exp(cl * log_g)
    w = RET_HEADS * RET_DK
    rspec = pl.BlockSpec((cl, w), lambda bi, ci: (bi * nc + ci, 0))
    cspec = lambda shape: pl.BlockSpec(shape, lambda bi, ci: (0,) * len(shape))
    sspec = pl.BlockSpec((1, RET_HEADS, RET_DK, RET_DV), lambda bi, ci: (bi, 0, 0, 0))
    return pl.pallas_call(
        _retention_kernel,
        grid=(b, nc),
        in_specs=[pl.BlockSpec(memory_space=pltpu.SMEM), rspec, rspec, rspec, rspec,
                  pl.BlockSpec((cl, RET_DK), lambda bi, ci: (ci, 0)),
                  pl.BlockSpec((cl, RET_DK), lambda bi, ci: (ci, 0)),
                  cspec((RET_HEADS, cl, cl)), cspec((cl, w)), cspec((cl, w)), cspec((1, w)), sspec],
        out_specs=[rspec, sspec],
        out_shape=[jax.ShapeDtypeStruct((b * t, w), BF16),
                   jax.ShapeDtypeStruct((b, RET_HEADS, RET_DK, RET_DV), F32)],
        compiler_params=_cparams(("parallel", "arbitrary")),
        name="retention",
    )(cd, rq, rk, rv, rg, cosf, sinf, dmat, qd, kd, g_ret.reshape(1, w), s0)


def _t5_bucket(rel):
    nb = NUM_BUCKETS // 2
    max_exact = nb // 2
    base = jnp.where(rel > 0, nb, 0)
    n = jnp.abs(rel)
    nf = jnp.maximum(n, 1).astype(F32)
    large = max_exact + (jnp.log(nf / max_exact) / math.log(MAX_DISTANCE / max_exact) * (nb - max_exact)).astype(I32)
    large = jnp.minimum(large, nb - 1)
    return base + jnp.where(n < max_exact, n, large)


def _band_kernel(rb_ref, bucket_ref, band_ref):
    bucket = bucket_ref[...]
    for h in range(ATT_HEADS):
        acc = jnp.zeros(bucket.shape, F32)
        for j in range(NUM_BUCKETS):
            acc = jnp.where(bucket == j, rb_ref[j, h], acc)
        band_ref[h] = acc


def _bias_band(rel_bias):
    c = jnp.arange(BAND_TILES * LANES, dtype=I32)[:, None]
    t = jnp.arange(LANES, dtype=I32)[None, :]
    bucket = _t5_bucket(c - (BAND_TILES - 1) * LANES - t)
    return pl.pallas_call(
        _band_kernel,
        in_specs=[pl.BlockSpec(memory_space=pltpu.SMEM), pl.BlockSpec(memory_space=pltpu.VMEM)],
        out_specs=pl.BlockSpec(memory_space=pltpu.VMEM),
        out_shape=jax.ShapeDtypeStruct((ATT_HEADS, BAND_TILES * LANES, LANES), F32),
        name="bias_band",
    )(rel_bias, bucket)


def _order_key(s):
    bits = lax.bitcast_convert_type(s, I32)
    return bits ^ ((bits >> 31) & 0x7FFFFFFF)


def _fold8(x, op=jnp.add):
    parts = [x[i * SUBLANES:(i + 1) * SUBLANES] for i in range(x.shape[0] // SUBLANES)]
    while len(parts) > 1:
        parts = [op(parts[i], parts[i + 1]) for i in range(0, len(parts), 2)]
    return parts[0]


GROUP = LANES // SUBLANES


def _sort_network(n):
    pairs = []
    p = 1
    while p < n:
        k = p
        while k >= 1:
            for j in range(k % p, n - k, 2 * k):
                for i in range(min(k, n - j - k)):
                    if (i + j) // (2 * p) == (i + j + k) // (2 * p):
                        pairs.append((i + j, i + j + k))
            k //= 2
        p *= 2
    return pairs


def _sort_desc(vals):
    vals = list(vals)
    for a, b in _sort_network(len(vals)):
        vals[a], vals[b] = jnp.maximum(vals[a], vals[b]), jnp.minimum(vals[a], vals[b])
    return vals


def _tile_loop(nt, trip, carry):
    def run(first, trips, width, carry):
        return lax.fori_loop(0, trips, lambda i, c: trip([first + i * width + u for u in range(width)], c), carry)
    carry = run(0, nt // 4, 4, carry)
    carry = run(nt // 4 * 4, (nt // 2) % 2, 2, carry)
    return run(nt // 2 * 2, nt % 2, 1, carry)


def _attention_kernel(q_ref, iq_ref, tail_ref, k_ref, vt_ref, ki_ref, band_ref, adm_ref, o_ref,
                      iqs_s, key_s, ks_s, mask_s, log_s, acc_s, *, jd0, n_sel, idx_bits):
    tq = q_ref.shape[1]
    qb = tq // LANES
    jd = jd0 + qb * pl.program_id(1) + qb - 1
    nt = jd + 1
    classes = band_ref.shape[1] // LANES
    krow = lax.broadcasted_iota(I32, (LANES, tq), 0)

    qt = q_ref[...]
    iqt = iq_ref[...]
    wt = tail_ref[IDX_DIM:IDX_DIM + IDX_HEADS, :] * (IDX_HEADS ** -0.5) * (IDX_DIM ** -0.5)
    zpad = jnp.zeros((LANES - IDX_DIM, tq), BF16)
    for h in range(IDX_HEADS):
        iqs_s[:, h * tq:(h + 1) * tq] = jnp.concatenate([iqt[h * IDX_DIM:(h + 1) * IDX_DIM], zpad], axis=0)

    def tile_rows(j):
        return pl.ds(pl.multiple_of(j * LANES, LANES), LANES)

    def class_rows(j):
        return tile_rows(jnp.clip(j - jd + classes - 1, 0, classes - 1))

    def score_tile(j):
        d = jnp.dot(ki_ref[tile_rows(j), :], iqs_s[...], preferred_element_type=F32)
        acc = jnp.zeros((LANES, tq), F32)
        for h in range(IDX_HEADS):
            acc = acc + wt[h:h + 1] * jnp.maximum(d[:, h * tq:(h + 1) * tq], 0.0)
        return acc

    def score_trip(js, carry):
        for j in js:
            key = jnp.where(adm_ref[class_rows(j), :] > 0.0, _order_key(score_tile(j)), _f32_key_const(NEG_INF))
            key_s[j] = key
            srt = _sort_desc([key[g * SUBLANES:(g + 1) * SUBLANES] for g in range(GROUP)])
            for g in range(GROUP):
                ks_s[j, g * SUBLANES:(g + 1) * SUBLANES] = srt[g]
        return carry

    _tile_loop(nt, score_trip, 0)
    nquad = (nt + 3) // 4

    def fill_body(j, carry):
        ks_s[j] = jnp.full((LANES, tq), INT_MIN, I32)
        return carry

    lax.fori_loop(nt, nquad * 4, fill_body, 0)

    def count(pred_tile):
        def trip(js, acc):
            for j in js:
                acc = acc + _fold8(pred_tile(j))
            return acc
        return jnp.sum(_tile_loop(nt, trip, jnp.zeros((SUBLANES, tq), F32)), axis=0, keepdims=True)

    def count_ge(cand):
        def one(m):
            return jnp.where(m, 1.0, 0.0)

        def trip(q, accs):
            a8, a4, a2, a1 = accs
            for u in range(4):
                v = [ks_s[4 * q + u, g * SUBLANES:(g + 1) * SUBLANES] for g in range(GROUP)]
                t1 = v[7] >= cand
                t2 = jnp.where(t1, v[11], v[3]) >= cand
                t3 = jnp.where(t1, jnp.where(t2, v[13], v[9]), jnp.where(t2, v[5], v[1])) >= cand
                t4 = jnp.where(t1, jnp.where(t2, jnp.where(t3, v[14], v[12]), jnp.where(t3, v[10], v[8])),
                               jnp.where(t2, jnp.where(t3, v[6], v[4]), jnp.where(t3, v[2], v[0]))) >= cand
                a8, a4, a2 = a8 + one(t1), a4 + one(t2), a2 + one(t3)
                a1 = a1 + one(t4) + one(v[15] >= cand)
            return a8, a4, a2, a1

        zero = jnp.zeros((SUBLANES, tq), F32)
        a8, a4, a2, a1 = lax.fori_loop(0, nquad, trip, (zero, zero, zero, zero))
        return jnp.sum(8.0 * a8 + 4.0 * a4 + 2.0 * a2 + a1, axis=0, keepdims=True)

    kf = float(n_sel)

    def bit_body(it, thr):
        cand = thr + lax.shift_left(jnp.int32(1), 31 - it)
        return jnp.where(count_ge(cand) >= kf, cand, thr)

    thr = lax.fori_loop(0, 32, bit_body, jnp.full((1, tq), INT_MIN, I32))

    need = kf - count(lambda j: jnp.where(key_s[j] > thr, 1.0, 0.0))
    n_tied = count(lambda j: jnp.where(key_s[j] == thr, 1.0, 0.0))
    surplus = jnp.where(thr > HALF_NEG_KEY, n_tied - need, 0.0)

    def tie_search():
        def tie_body(it, j0):
            cand = j0 + lax.shift_left(jnp.int32(1), idx_bits - 1 - it)
            cnt = count(lambda j: jnp.where(key_s[j] == thr, jnp.where(krow + j * LANES < cand, 1.0, 0.0), 0.0))
            return jnp.where(cnt < need, cand, j0)
        return lax.fori_loop(0, idx_bits, tie_body, jnp.zeros((1, tq), I32))

    j0 = lax.cond(jnp.max(surplus) > 0.0, tie_search, lambda: jnp.full((1, tq), INT_MAX, I32))

    def mask_trip(js, carry):
        for j in js:
            key = key_s[j]
            sel = jnp.where(key > thr, 1.0, jnp.where(key == thr, jnp.where(krow + j * LANES <= j0, 1.0, 0.0), 0.0))
            sel = jnp.where(key > HALF_NEG_KEY, sel, 0.0)
            mask_s[j] = jnp.where(sel > 0.0, 0.0, NEG_INF)
        return carry

    _tile_loop(nt, mask_trip, 0)

    scale = ATT_HEAD_DIM ** -0.5
    heads = range(ATT_HEADS)
    qh = [qt[h * ATT_HEAD_DIM:(h + 1) * ATT_HEAD_DIM] for h in heads]

    def log_trip(js, mx):
        mx = list(mx)
        for j in js:
            band_rows = class_rows(j)
            msk = mask_s[j]
            for h in heads:
                kh = k_ref[tile_rows(j), h * ATT_HEAD_DIM:(h + 1) * ATT_HEAD_DIM]
                lg = jnp.dot(kh, qh[h], preferred_element_type=F32) * scale + band_ref[h, band_rows, :] + msk
                log_s[h, j] = lg
                mx[h] = jnp.maximum(mx[h], _fold8(lg, jnp.maximum))
        return tuple(mx)

    mx = _tile_loop(nt, log_trip, tuple(jnp.full((SUBLANES, tq), NEG_INF, F32) for _ in heads))
    m = [jnp.max(mx[h], axis=0, keepdims=True) for h in heads]
    acc_s[...] = jnp.zeros(acc_s.shape, F32)

    def pv_trip(js, ls):
        ls = list(ls)
        for h in heads:
            acc = acc_s[h]
            for j in js:
                p = jnp.exp(log_s[h, j] - m[h])
                ls[h] = ls[h] + _fold8(p)
                vth = vt_ref[j, h * ATT_HEAD_DIM:(h + 1) * ATT_HEAD_DIM, :]
                acc = acc + jnp.dot(vth, p.astype(BF16), preferred_element_type=F32)
            acc_s[h] = acc
        return tuple(ls)

    ls = _tile_loop(nt, pv_trip, tuple(jnp.zeros((SUBLANES, tq), F32) for _ in heads))
    for h in heads:
        ot = acc_s[h] / jnp.sum(ls[h], axis=0, keepdims=True)
        for u in range(qb):
            o_ref[u * LANES:(u + 1) * LANES, h * ATT_HEAD_DIM:(h + 1) * ATT_HEAD_DIM] = jnp.transpose(
                ot[:, u * LANES:(u + 1) * LANES]).astype(o_ref.dtype)


def _attention(aqt, iqt, tailt, keys_k, keys_v, keys_i, band, b, t, past_len):
    l = past_len + t
    tq = LANES
    nqb = -(-t // tq)
    tp = nqb * tq
    assert past_len % LANES == 0 and (t % tq == 0 or nqb == 1)
    jd0 = past_len // LANES
    ntiles = jd0 + nqb
    ntp = ntiles
    lp = ntp * LANES
    n_sel = min(TOPK_MAX, l // 4)
    if tp != t:
        padq = lambda a: jnp.pad(a.reshape(-1, b, t), ((0, 0), (0, 0), (0, tp - t))).reshape(-1, b * tp)
        aqt, iqt, tailt = padq(aqt), padq(iqt), padq(tailt)
    w = ATT_HEADS * ATT_HEAD_DIM
    pad = ((0, 0), (0, lp - l), (0, 0))
    kk = jnp.pad(keys_k, pad)
    vt = jnp.pad(keys_v, pad).reshape(b, ntp, LANES, w).transpose(0, 1, 3, 2)
    ki2 = jnp.pad(keys_i, ((0, 0), (0, lp - l), (0, LANES - IDX_DIM)))
    cc = jnp.arange(LANES)[:, None]
    tt = jnp.arange(LANES)[None, :]
    adm = ((cc // CHUNK <= tt // CHUNK) & (cc < l - (ntiles - 1) * LANES)).astype(F32)
    qb = 2 if nqb % 2 == 0 else 1
    wq = qb * LANES
    classes = BAND_TILES + qb - 1
    band_w, adm_w = [], []
    for r in range(classes):
        rel = [r - (classes - 1) + (qb - 1 - u) for u in range(qb)]
        band_w.append(jnp.concatenate(
            [band[:, min(max(x + BAND_TILES - 1, 0), BAND_TILES - 1) * LANES:][:, :LANES] for x in rel], axis=2))
        adm_w.append(jnp.concatenate(
            [adm if x == 0 else jnp.full((LANES, LANES), 1.0 if x < 0 else 0.0, F32) for x in rel], axis=1))
    band_w = jnp.concatenate(band_w, axis=1)
    adm_w = jnp.concatenate(adm_w, axis=0)
    nsteps = nqb // qb
    qspec = lambda width: pl.BlockSpec((wq, width), lambda bi, qi: (bi * nsteps + qi, 0))
    qtspec = lambda rows: pl.BlockSpec((rows, wq), lambda bi, qi: (0, bi * nsteps + qi))
    kspec = lambda width: pl.BlockSpec((None, lp, width), lambda bi, qi: (bi, 0, 0))
    kern = functools.partial(_attention_kernel, jd0=jd0, n_sel=n_sel, idx_bits=max((lp - 1).bit_length(), 1))
    out = pl.pallas_call(
        kern,
        grid=(b, nsteps),
        in_specs=[qtspec(w), qtspec(IDX_HEADS * IDX_DIM), qtspec(LANES),
                  kspec(w), pl.BlockSpec((None, ntp, w, LANES), lambda bi, qi: (bi, 0, 0, 0)), kspec(LANES),
                  pl.BlockSpec((ATT_HEADS, classes * LANES, wq), lambda bi, qi: (0, 0, 0)),
                  pl.BlockSpec((classes * LANES, wq), lambda bi, qi: (0, 0))],
        out_specs=qspec(w),
        out_shape=jax.ShapeDtypeStruct((b * tp, w), BF16),
        scratch_shapes=[pltpu.VMEM((LANES, IDX_HEADS * wq), BF16),
                        pltpu.VMEM((ntp, LANES, wq), I32), pltpu.VMEM((-(-ntp // 4) * 4, LANES, wq), I32),
                        pltpu.VMEM((ntp, LANES, wq), F32),
                        pltpu.VMEM((ATT_HEADS, ntp, LANES, wq), F32), pltpu.VMEM((ATT_HEADS, ATT_HEAD_DIM, wq), F32)],
        compiler_params=_cparams(("parallel", "arbitrary")),
        name="attention",
    )(aqt, iqt, tailt, kk, vt, ki2, band_w, adm_w)
    if tp != t:
        out = out.reshape(b, tp, w)[:, :t].reshape(b * t, w)
    return out


def _store_tile_major(ref, x):
    rows, width = x.shape
    nk = width // LANES
    for c in range(nk):
        ref[pl.ds(c, rows, stride=nk), :] = x[:, c * LANES:(c + 1) * LANES]


def _load_tile_major(ref, rows, nk, first=0, stride=None):
    stride = nk if stride is None else stride
    return jnp.concatenate([ref[pl.ds(first + c, rows, stride=stride), :] for c in range(nk)], axis=1)


def _outproj_kernel(x_ref, oret_ref, oatt_ref, wo_ref, g_ref, wr_ref, br_ref, *rest, own_steps):
    outs = rest[-4:]
    step = pl.program_id(0)

    @pl.when(step < own_steps)
    def _():
        _outproj_body(x_ref, oret_ref, oatt_ref, wo_ref, g_ref, wr_ref, br_ref, *outs)

    @pl.when(step >= own_steps)
    def _():
        outs[1][...] = jnp.zeros(outs[1].shape, F32)


def _outproj_body(x_ref, oret_ref, oatt_ref, wo_ref, g_ref, wr_ref, br_ref, x1_ref, h2_ref, e_ref, gt_ref):
    mixed = jnp.concatenate([oret_ref[...], oatt_ref[...]], axis=1)
    x1 = x_ref[...] + jnp.dot(mixed, wo_ref[...], preferred_element_type=F32)
    x1_ref[...] = x1
    h2 = (x1 * lax.rsqrt(jnp.mean(jnp.square(x1), -1, keepdims=True) + EPS)) * g_ref[...]
    _store_tile_major(h2_ref, h2)
    lg = lax.dot_general(wr_ref[...], h2.astype(BF16), (((1,), (1,)), ((), ())),
                         preferred_element_type=F32) + br_ref[...]
    tm = lg.shape[1]
    ridx = lax.broadcasted_iota(I32, (SUBLANES, tm), 0).astype(F32)
    first = lambda hit: jnp.min(jnp.where(hit, ridx, float(SUBLANES)), 0, keepdims=True)
    gl = lg[:SUBLANES]
    gmax = jnp.max(gl, 0, keepdims=True)
    p_top = 1.0 / jnp.sum(jnp.exp(gl - gmax), 0, keepdims=True)
    g_top = first(gl == gmax)
    el = jnp.zeros((EXPERTS_PER_GROUP, tm), F32)
    for g in range(N_GROUPS):
        el = jnp.where(g_top == float(g), lg[(1 + g) * SUBLANES:(2 + g) * SUBLANES], el)
    v1 = jnp.max(el, 0, keepdims=True)
    i1 = first(el == v1)
    el2 = jnp.where(ridx == i1, -jnp.inf, el)
    v2 = jnp.max(el2, 0, keepdims=True)
    i2 = first(el2 == v2)
    e2 = jnp.exp(v2 - v1)
    den = 1.0 + e2
    e_ref[...] = jnp.concatenate([g_top * EXPERTS_PER_GROUP + i1, g_top * EXPERTS_PER_GROUP + i2], 0).astype(I32)
    gt_ref[...] = jnp.concatenate([1.0 / den, e2 / den], 0) * p_top


def _outproj_router(x2, o_ret, o_att, w_out, g_ffn, w_group, b_group, w_er, b_er, h2_all, row0, rows_all):
    n, d = x2.shape
    shared = () if h2_all is None else (h2_all,)
    tm = _row_tile(n, 512)
    other = 0 if shared else rows_all - n
    if other:
        assert row0 == 0
        tm = math.gcd(tm, other)
    assert row0 % tm == 0
    blk0 = row0 // tm
    own_steps = n // tm
    nk = d // LANES
    assert N_GROUPS <= SUBLANES and EXPERTS_PER_GROUP == SUBLANES
    gpad = SUBLANES - N_GROUPS
    nrow = SUBLANES + N_EXPERTS
    rpad = -nrow % 16
    wr = jnp.concatenate([w_group.T, jnp.zeros((gpad, d), F32), w_er.reshape(d, N_EXPERTS).T,
                          jnp.zeros((rpad, d), F32)], 0).astype(BF16)
    br = jnp.concatenate([b_group, jnp.full((gpad,), NEG_INF, F32), b_er.reshape(N_EXPERTS),
                          jnp.zeros((rpad,), F32)]).reshape(nrow + rpad, 1)
    last = own_steps - 1
    rspec = lambda w: pl.BlockSpec((tm, w), lambda i: (jnp.minimum(i, last), 0))
    tspec = pl.BlockSpec((TOP_E, tm), lambda i: (0, jnp.minimum(i, last)))
    cspec = lambda r, c: pl.BlockSpec((r, c), lambda i: (0, 0))
    mw = w_out.shape[0]
    x1, h2_all, expert_t, gate_t = pl.pallas_call(
        functools.partial(_outproj_kernel, own_steps=own_steps),
        grid=(own_steps + other // tm,),
        in_specs=[rspec(d), rspec(o_ret.shape[1]), rspec(o_att.shape[1]), cspec(mw, d), cspec(1, d),
                  cspec(nrow + rpad, d), cspec(nrow + rpad, 1)] + [pl.BlockSpec(memory_space=pl.ANY) for _ in shared],
        out_specs=[rspec(d), pl.BlockSpec((tm * nk, LANES), lambda i: (blk0 + i, 0)), tspec, tspec],
        out_shape=[jax.ShapeDtypeStruct((n, d), F32), jax.ShapeDtypeStruct((rows_all * nk, LANES), F32),
                   jax.ShapeDtypeStruct((TOP_E, n), I32), jax.ShapeDtypeStruct((TOP_E, n), F32)],
        input_output_aliases={7: 1} if shared else {},
        compiler_params=_cparams(("arbitrary",)),
        name="outproj_router",
    )(x2, o_ret, o_att, w_out.astype(BF16), g_ffn.reshape(1, d), wr, br, *shared)
    return x1, h2_all, expert_t.T, gate_t.T


def _expert_kernel(blk_e_ref, tok0_ref, tokn_ref, slot_ref, h_ref, wg_ref, wu_ref, wd_ref, o_ref,
                   xbuf, obuf, gsem, ssem):
    i = pl.program_id(0)
    nb = pl.num_programs(0)
    nk = wg_ref.shape[0] // LANES
    rb = xbuf.shape[0] // (2 * nk)
    blk = rb * nk
    cur = i % 2
    nxt = 1 - cur

    def lines(first, count):
        return pl.ds(pl.multiple_of(first, nk), count)

    def for_rows(start_row):
        def body(g, carry):
            for u in range(DMA_UNROLL):
                start_row(g * DMA_UNROLL + u, u % 2)
            return carry
        lax.fori_loop(0, rb // DMA_UNROLL, body, 0)

    def start_gather(tok_ref, buf):
        for_rows(lambda r, pri: pltpu.make_async_copy(
            h_ref.at[lines(tok_ref[0, 0, r], nk)], xbuf.at[lines(buf * blk + r * nk, nk)],
            gsem.at[buf]).start(priority=pri))

    def wait_gather(buf):
        pltpu.make_async_copy(h_ref.at[pl.ds(0, blk)], xbuf.at[lines(buf * blk, blk)], gsem.at[buf]).wait()

    def wait_scatter(buf):
        pltpu.make_async_copy(obuf.at[lines(buf * blk, blk)], o_ref.at[pl.ds(0, blk)], ssem.at[buf]).wait()

    @pl.when(i == 0)
    def _():
        start_gather(tok0_ref, 0)

    @pl.when(i + 1 < nb)
    def _():
        start_gather(tokn_ref, nxt)

    wait_gather(cur)
    base = cur * blk
    xb = jnp.concatenate([xbuf[pl.ds(base + c, rb, stride=nk), :] for c in range(nk)], axis=1).astype(BF16)
    hid = jax.nn.silu(jnp.dot(xb, wg_ref[...].astype(BF16), preferred_element_type=F32))
    hid = hid * jnp.dot(xb, wu_ref[...].astype(BF16), preferred_element_type=F32)
    out = jnp.dot(hid.astype(BF16), wd_ref[...].astype(BF16), preferred_element_type=F32)

    @pl.when(i >= 2)
    def _():
        wait_scatter(cur)

    for c in range(nk):
        obuf[pl.ds(base + c, rb, stride=nk), :] = out[:, c * LANES:(c + 1) * LANES]

    for_rows(lambda r, pri: pltpu.make_async_copy(
        obuf.at[lines(base + r * nk, nk)], o_ref.at[lines(slot_ref[0, 0, r], nk)], ssem.at[cur]).start(priority=pri))

    @pl.when(i == nb - 1)
    def _():
        wait_scatter(cur)

        @pl.when(nb >= 2)
        def _():
            wait_scatter(nxt)


def _experts(h2, src_tok, out_slot, blk_e, w_gate, w_up, w_down, rb):
    d, de = w_gate.shape[1:]
    nk = d // LANES
    rows = src_tok.shape[0]
    nb = rows // rb
    idx3 = lambda a: (a * nk).reshape(nb, 1, rb)
    ispec = lambda f: pl.BlockSpec((1, 1, rb), f, memory_space=pltpu.SMEM)
    wspec = lambda r, c: pl.BlockSpec((None, r, c), lambda i, be: (be[i], 0, 0))
    return pl.pallas_call(
        _expert_kernel,
        grid_spec=pltpu.PrefetchScalarGridSpec(
            num_scalar_prefetch=1,
            grid=(nb,),
            in_specs=[ispec(lambda i, be: (0, 0, 0)), ispec(lambda i, be: (jnp.minimum(i + 1, nb - 1), 0, 0)),
                      ispec(lambda i, be: (i, 0, 0)), pl.BlockSpec(memory_space=pl.ANY),
                      wspec(d, de), wspec(d, de), wspec(de, d)],
            out_specs=pl.BlockSpec(memory_space=pl.ANY),
            scratch_shapes=[pltpu.VMEM((2 * rb * nk, LANES), F32), pltpu.VMEM((2 * rb * nk, LANES), F32),
                            pltpu.SemaphoreType.DMA((2,)), pltpu.SemaphoreType.DMA((2,))],
        ),
        out_shape=jax.ShapeDtypeStruct((rows * nk, LANES), F32),
        compiler_params=_cparams(("arbitrary",)),
        name="experts",
    )(blk_e, idx3(src_tok), idx3(src_tok), idx3(out_slot), h2, w_gate, w_up, w_down)


def _route_plan(expert, rb):
    n = expert.shape[0]
    a = n * TOP_E
    flat_e = expert.reshape(a)
    counts = jnp.sum((flat_e[:, None] == jnp.arange(N_EXPERTS, dtype=I32)[None, :]).astype(I32), axis=0)
    padded = (counts + rb - 1) // rb * rb
    ends = jnp.cumsum(padded)
    n_blocks = (a + N_EXPERTS * (rb - 1) + rb - 1) // rb
    rows = n_blocks * rb
    q = jnp.arange(rb - 1, dtype=I32)[None, :]
    e = jnp.arange(N_EXPERTS, dtype=I32)[:, None]
    pad_key = jnp.where(q < (padded - counts)[:, None], 2 * e + 1, 2 * N_EXPERTS).reshape(-1)
    spare = jnp.full((rows - a - N_EXPERTS * (rb - 1),), 2 * N_EXPERTS, I32)
    perm = jnp.argsort(jnp.concatenate([2 * flat_e, pad_key, spare]), stable=True).astype(I32)
    src_tok = jnp.where(perm < a, perm // TOP_E, 0)
    out_slot = perm
    blk_e = jnp.minimum(jnp.sum((ends[None, :] <= jnp.arange(n_blocks, dtype=I32)[:, None] * rb).astype(I32), axis=1),
                        N_EXPERTS - 1)
    return src_tok, out_slot, blk_e


def _combine_kernel(x1_ref, eo_ref, gt_ref, g_ref, y_ref):
    tm, d = x1_ref.shape
    nk = d // LANES
    gt = gt_ref[...]
    moe = _load_tile_major(eo_ref, tm, nk, 0, TOP_E * nk) * gt[:, 0:1]
    moe = moe + _load_tile_major(eo_ref, tm, nk, nk, TOP_E * nk) * gt[:, 1:2]
    x = x1_ref[...] + moe
    y_ref[...] = (x * lax.rsqrt(jnp.mean(jnp.square(x), -1, keepdims=True) + EPS)) * g_ref[...]


def _combine(x1, eo2, gate, g_final, row0):
    n, d = x1.shape
    tm = _row_tile(n, 512)
    assert row0 % tm == 0
    blk0 = row0 // tm
    nk = d // LANES
    rspec = lambda w: pl.BlockSpec((tm, w), lambda i: (i, 0))
    return pl.pallas_call(
        _combine_kernel,
        grid=(n // tm,),
        in_specs=[rspec(d), pl.BlockSpec((tm * TOP_E * nk, LANES), lambda i: (blk0 + i, 0)), rspec(TOP_E),
                  pl.BlockSpec((1, d), lambda i: (0, 0))],
        out_specs=rspec(d),
        out_shape=jax.ShapeDtypeStruct((n, d), F32),
        compiler_params=_cparams(("parallel",)),
        name="combine",
    )(x1, eo2, gate, g_final.reshape(1, d))


def _mixer(x, past_len, s_ret, past_k, past_v, past_ki, band, params, h2_all, row0, rows_all):
    (rel_bias, g_mix, w_in, g_ret, w_out, g_ffn, w_group, b_group, w_er, b_er, w_gate, w_up, w_down) = params
    b, t, d = x.shape
    n = b * t
    x2 = x.reshape(n, d)
    rq, rk, rv, rg, ak, av, akb, avb, ik, aqt, iqt, tailt = _inproj(x2, g_mix, w_in)
    o_ret, s_new = _retention(rq, rk, rv, rg, s_ret, g_ret, b, t, past_len)
    aw = ATT_HEADS * ATT_HEAD_DIM
    keys_k = jnp.concatenate([past_k.reshape(b, past_len, aw).astype(BF16), akb.reshape(b, t, aw)], 1)
    keys_v = jnp.concatenate([past_v.reshape(b, past_len, aw).astype(BF16), avb.reshape(b, t, aw)], 1)
    keys_i = jnp.concatenate([past_ki.astype(BF16), ik.reshape(b, t, IDX_DIM).astype(BF16)], 1)
    o_att = _attention(aqt, iqt, tailt, keys_k, keys_v, keys_i, band, b, t, past_len)
    x1, h2_all, expert, gate = _outproj_router(x2, o_ret, o_att, w_out, g_ffn, w_group, b_group, w_er, b_er,
                                               h2_all, row0, rows_all)
    caches = (ak.reshape(1, b, t, ATT_HEADS, ATT_HEAD_DIM), av.reshape(1, b, t, ATT_HEADS, ATT_HEAD_DIM),
              ik.reshape(1, b, t, IDX_DIM), s_new[None])
    return x1, h2_all, expert, gate, caches


def kernel(x_prompt, x_sample, cache_attn_k, cache_attn_v, cache_idx_k, state_ret, rel_bias, g_mix, w_in, g_ret, w_out, g_ffn, w_group, b_group, w_expert_router, b_expert_router, w_gate, w_up, w_down, g_final):
    assert g_mix.shape[0] == 1, "single-layer model"
    params = (rel_bias, g_mix[0], w_in[0], g_ret[0], w_out[0], g_ffn[0], w_group[0], b_group[0],
              w_expert_router[0], b_expert_router[0], w_gate[0], w_up[0], w_down[0])
    band = _bias_band(rel_bias)
    nb = x_prompt.shape[0]
    past_len = cache_attn_k.shape[2]
    dt = x_prompt.dtype
    empty_kv = jnp.zeros((nb, 0, ATT_HEADS, ATT_HEAD_DIM), dt)
    empty_ki = jnp.zeros((nb, 0, IDX_DIM), dt)
    s0 = jnp.zeros((nb, RET_HEADS, RET_DK, RET_DV), dt)
    n_p = x_prompt.shape[0] * x_prompt.shape[1]
    n_s = x_sample.shape[0] * x_sample.shape[1]
    x1p, h2_all, ep, gp, caches_p = _mixer(x_prompt, 0, s0, empty_kv, empty_kv, empty_ki, band, params,
                                           None, 0, n_p + n_s)
    x1s, h2_all, es, gs, caches_s = _mixer(x_sample, past_len, state_ret[0], cache_attn_k[0], cache_attn_v[0],
                                           cache_idx_k[0], band, params, h2_all, n_p, n_p + n_s)
    rb = 256 if n_p + n_s >= 8192 else 128
    src_tok, out_slot, blk_e = _route_plan(jnp.concatenate([ep, es], 0), rb)
    eo = _experts(h2_all, src_tok, out_slot, blk_e, w_gate[0], w_up[0], w_down[0], rb)
    yp = _combine(x1p, eo, gp, g_final, 0).reshape(x_prompt.shape)
    ys = _combine(x1s, eo, gs, g_final, n_p).reshape(x_sample.shape)
    return (yp, ys) + caches_p + caches_s
```

```python
import functools
import math

import jax
import jax.numpy as jnp
import numpy as np
from jax import lax
from jax.experimental import pallas as pl
from jax.experimental.pallas import tpu as pltpu

F32 = jnp.float32
BF16 = jnp.bfloat16
I32 = jnp.int32

CHUNK = 64
RET_HEADS = 4
RET_DK = 128
RET_DV = 128
ROPE_BASE = 10000.0
ATT_HEADS = 4
ATT_HEAD_DIM = 128
IDX_HEADS = 8
IDX_DIM = 64
TOPK_MAX = 256
NUM_BUCKETS = 32
MAX_DISTANCE = 128
N_GROUPS = 4
EXPERTS_PER_GROUP = 8
N_EXPERTS = N_GROUPS * EXPERTS_PER_GROUP
TOP_E = 2
EPS = 1e-6
NEG_INF = -1e30

LANES = 128
SUBLANES = 8
VMEM_LIMIT = 56 * 1024 * 1024
RET_CHUNK = 256
DMA_UNROLL = 8
INT_MIN = -(2 ** 31)
INT_MAX = 2 ** 31 - 1
BAND_TILES = 3
assert (BAND_TILES - 2) * LANES + 1 >= MAX_DISTANCE
assert math.log(IDX_DIM, 4).is_integer()


def _f32_key_const(v):
    b = int(np.array(v, np.float32).view(np.int32))
    return b ^ ((b >> 31) & 0x7FFFFFFF)


HALF_NEG_KEY = _f32_key_const(0.5 * NEG_INF)


def _cparams(sem):
    return pltpu.CompilerParams(dimension_semantics=sem, vmem_limit_bytes=VMEM_LIMIT)


def _row_tile(n, want=256):
    return want if n % want == 0 else n


def _inproj_kernel(x_ref, g_ref, wm_ref, wt_ref, wqt_ref, wiqt_ref, wtt_ref,
                   rq, rk, rv, rg, ak, av, akb, avb, ik, aqt, iqt, tailt):
    x = x_ref[...]
    h = (x * lax.rsqrt(jnp.mean(jnp.square(x), -1, keepdims=True) + EPS)) * g_ref[...]
    hb = h.astype(BF16)

    def proj(i):
        return jnp.dot(hb, wm_ref[:, i * 512:(i + 1) * 512], preferred_element_type=F32)

    def proj_t(wt_rows):
        return lax.dot_general(wt_rows, hb, (((1,), (1,)), ((), ())), preferred_element_type=F32)

    rq[...] = proj(0)
    rk[...] = proj(1)
    rv[...] = proj(2).astype(BF16)
    rg[...] = proj(3)
    aqt[...] = proj_t(wqt_ref[...]).astype(BF16)
    k = proj(5)
    _store_tile_major(ak, k)
    akb[...] = k.astype(BF16)
    v = proj(6)
    _store_tile_major(av, v)
    avb[...] = v.astype(BF16)
    iqt[...] = proj_t(wiqt_ref[...]).astype(BF16)
    ik[...] = jnp.dot(hb, wt_ref[...], preferred_element_type=F32)[:, :IDX_DIM]
    tailt[...] = proj_t(wtt_ref[...])


def _inproj(x2, g_mix, w_in):
    n, d = x2.shape
    tm = _row_tile(n, 512)
    wm = w_in[:, :4096].astype(BF16)
    wt = jnp.pad(w_in[:, 4096:], ((0, 0), (0, LANES - (w_in.shape[1] - 4096)))).astype(BF16)
    rspec = lambda w, lines=1: pl.BlockSpec((tm * lines, w), lambda i: (i, 0))
    outs = [(512, F32, 1), (512, F32, 1), (512, BF16, 1), (512, F32, 1),
            (LANES, F32, ATT_HEADS), (LANES, F32, ATT_HEADS),
            (512, BF16, 1), (512, BF16, 1), (IDX_DIM, F32, 1)]
    outs_t = [(512, BF16), (512, BF16), (LANES, F32)]
    whole = lambda a: pl.BlockSpec(a.shape, lambda i: (0, 0), pipeline_mode=pl.Buffered(1))
    wqt, wiqt, wtt = wm[:, 4 * 512:5 * 512].T, wm[:, 7 * 512:8 * 512].T, wt.T
    return pl.pallas_call(
        _inproj_kernel,
        grid=(n // tm,),
        in_specs=[rspec(d), pl.BlockSpec((1, d), lambda i: (0, 0)), whole(wm), whole(wt),
                  whole(wqt), whole(wiqt), whole(wtt)],
        out_specs=[rspec(w, ln) for w, _, ln in outs] + [pl.BlockSpec((w, tm), lambda i: (0, i)) for w, _ in outs_t],
        out_shape=[jax.ShapeDtypeStruct((n * ln, w), dt) for w, dt, ln in outs]
        + [jax.ShapeDtypeStruct((w, n), dt) for w, dt in outs_t],
        compiler_params=_cparams(("parallel",)),
        name="inproj",
    )(x2, g_mix.reshape(1, d), wm, wt, wqt, wiqt, wtt)


def _retention_kernel(cd_ref, rq_ref, rk_ref, rv_ref, rg_ref, cos_ref, sin_ref, dmat_ref, qd_ref, kd_ref,
                      gret_ref, s0_ref, o_ref, s_ref):
    c = pl.program_id(1)

    @pl.when(c == 0)
    def _():
        s_ref[...] = s0_ref[...]

    cosf = cos_ref[...]
    sinf = sin_ref[...]
    half = RET_DK // 2

    def rot(x):
        return x * cosf + pltpu.roll(x, half, 1) * sinf

    for h in range(RET_HEADS):
        sl = slice(h * RET_DK, (h + 1) * RET_DK)
        q = rot(rq_ref[:, sl])
        k = rot(rk_ref[:, sl]) * (RET_DK ** -0.5)
        v = rv_ref[:, sl]
        s = s_ref[0, h]
        sc = lax.dot_general(q.astype(BF16), k.astype(BF16), (((1,), (1,)), ((), ())),
                             preferred_element_type=F32) * dmat_ref[h]
        o = jnp.dot(sc.astype(BF16), v, preferred_element_type=F32)
        o = o + jnp.dot((q * qd_ref[:, sl]).astype(BF16), s.astype(BF16), preferred_element_type=F32)
        kdt = jnp.transpose(k * kd_ref[:, sl]).astype(BF16)
        s_ref[0, h] = cd_ref[h] * s + jnp.dot(kdt, v, preferred_element_type=F32)
        mu = jnp.mean(o, -1, keepdims=True)
        var = jnp.mean(jnp.square(o - mu), -1, keepdims=True)
        on = (o - mu) * lax.rsqrt(var + EPS) * gret_ref[:, sl]
        o_ref[:, sl] = (jax.nn.silu(rg_ref[:, sl]) * on).astype(o_ref.dtype)


def _retention(rq, rk, rv, rg, s0, g_ret, b, t, past_len):
    cl = min(RET_CHUNK, t)
    nc = t // cl
    half = RET_DK // 2
    pos = (past_len + jnp.arange(t)).astype(F32)
    inv = ROPE_BASE ** (-jnp.arange(half, dtype=F32) / half)
    ang = pos[:, None] * inv[None, :]
    cosf = jnp.concatenate([jnp.cos(ang), jnp.cos(ang)], -1)
    sinf = jnp.concatenate([-jnp.sin(ang), jnp.sin(ang)], -1)
    log_g = jnp.log1p(-jnp.exp2(-5.0 - jnp.arange(RET_HEADS, dtype=F32)))
    i = jnp.arange(cl, dtype=F32)
    diff = i[:, None] - i[None, :]
    dmat = jnp.where(diff[None] >= 0, jnp.exp(jnp.maximum(diff, 0.0)[None] * log_g[:, None, None]), 0.0)
    kd = jnp.repeat(jnp.exp((cl - 1.0 - i)[:, None] * log_g[None, :]), RET_DK, axis=1)
    qd = jnp.repeat(jnp.exp((i + 1.0)[:, None] * log_g[None, :]), RET_DK, axis=1)
    cd = jnp.exp(cl * log_g)
    w = RET_HEADS * RET_DK
    rspec = pl.BlockSpec((cl, w), lambda bi, ci: (bi * nc + ci, 0))
    cspec = lambda shape: pl.BlockSpec(shape, lambda bi, ci: (0,) * len(shape))
    sspec = pl.BlockSpec((1, RET_HEADS, RET_DK, RET_DV), lambda bi, ci: (bi, 0, 0, 0))
    return pl.pallas_call(
        _retention_kernel,
        grid=(b, nc),
        in_specs=[pl.BlockSpec(memory_space=pltpu.SMEM), rspec, rspec, rspec, rspec,
                  pl.BlockSpec((cl, RET_DK), lambda bi, ci: (ci, 0)),
                  pl.BlockSpec((cl, RET_DK), lambda bi, ci: (ci, 0)),
                  cspec((RET_HEADS, cl, cl)), cspec((cl, w)), cspec((cl, w)), cspec((1, w)), sspec],
        out_specs=[rspec, sspec],
        out_shape=[jax.ShapeDtypeStruct((b * t, w), BF16),
                   jax.ShapeDtypeStruct((b, RET_HEADS, RET_DK, RET_DV), F32)],
        compiler_params=_cparams(("parallel", "arbitrary")),
        name="retention",
    )(cd, rq, rk, rv, rg, cosf, sinf, dmat, qd, kd, g_ret.reshape(1, w), s0)


def _t5_bucket(rel):
    nb = NUM_BUCKETS // 2
    max_exact = nb // 2
    base = jnp.where(rel > 0, nb, 0)
    n = jnp.abs(rel)
    nf = jnp.maximum(n, 1).astype(F32)
    large = max_exact + (jnp.log(nf / max_exact) / math.log(MAX_DISTANCE / max_exact) * (nb - max_exact)).astype(I32)
    large = jnp.minimum(large, nb - 1)
    return base + jnp.where(n < max_exact, n, large)


def _band_kernel(rb_ref, bucket_ref, band_ref):
    bucket = bucket_ref[...]
    for h in range(ATT_HEADS):
        acc = jnp.zeros(bucket.shape, F32)
        for j in range(NUM_BUCKETS):
            acc = jnp.where(bucket == j, rb_ref[j, h], acc)
        band_ref[h] = acc


def _bias_band(rel_bias):
    c = jnp.arange(BAND_TILES * LANES, dtype=I32)[:, None]
    t = jnp.arange(LANES, dtype=I32)[None, :]
    bucket = _t5_bucket(c - (BAND_TILES - 1) * LANES - t)
    return pl.pallas_call(
        _band_kernel,
        in_specs=[pl.BlockSpec(memory_space=pltpu.SMEM), pl.BlockSpec(memory_space=pltpu.VMEM)],
        out_specs=pl.BlockSpec(memory_space=pltpu.VMEM),
        out_shape=jax.ShapeDtypeStruct((ATT_HEADS, BAND_TILES * LANES, LANES), F32),
        name="bias_band",
    )(rel_bias, bucket)


def _order_key(s):
    bits = lax.bitcast_convert_type(s, I32)
    return bits ^ ((bits >> 31) & 0x7FFFFFFF)


def _fold8(x, op=jnp.add):
    parts = [x[i * SUBLANES:(i + 1) * SUBLANES] for i in range(x.shape[0] // SUBLANES)]
    while len(parts) > 1:
        parts = [op(parts[i], parts[i + 1]) for i in range(0, len(parts), 2)]
    return parts[0]


GROUP = LANES // SUBLANES


def _sort_network(n):
    pairs = []
    p = 1
    while p < n:
        k = p
        while k >= 1:
            for j in range(k % p, n - k, 2 * k):
                for i in range(min(k, n - j - k)):
                    if (i + j) // (2 * p) == (i + j + k) // (2 * p):
                        pairs.append((i + j, i + j + k))
            k //= 2
        p *= 2
    return pairs


def _sort_desc(vals):
    vals = list(vals)
    for a, b in _sort_network(len(vals)):
        vals[a], vals[b] = jnp.maximum(vals[a], vals[b]), jnp.minimum(vals[a], vals[b])
    return vals


def _tile_loop(nt, trip, carry):
    def run(first, trips, width, carry):
        return lax.fori_loop(0, trips, lambda i, c: trip([first + i * width + u for u in range(width)], c), carry)
    carry = run(0, nt // 4, 4, carry)
    carry = run(nt // 4 * 4, (nt // 2) % 2, 2, carry)
    return run(nt // 2 * 2, nt % 2, 1, carry)


def _attention_kernel(q_ref, iq_ref, tail_ref, k_ref, vt_ref, ki_ref, band_ref, adm_ref, o_ref,
                      iqs_s, key_s, ks_s, mask_s, log_s, acc_s, *, jd0, n_sel, idx_bits):
    tq = q_ref.shape[1]
    qb = tq // LANES
    jd = jd0 + qb * pl.program_id(1) + qb - 1
    nt = jd + 1
    classes = band_ref.shape[1] // LANES
    krow = lax.broadcasted_iota(I32, (LANES, tq), 0)

    qt = q_ref[...]
    iqt = iq_ref[...]
    wt = tail_ref[IDX_DIM:IDX_DIM + IDX_HEADS, :] * (IDX_HEADS ** -0.5) * (IDX_DIM ** -0.5)
    zpad = jnp.zeros((LANES - IDX_DIM, tq), BF16)
    for h in range(IDX_HEADS):
        iqs_s[:, h * tq:(h + 1) * tq] = jnp.concatenate([iqt[h * IDX_DIM:(h + 1) * IDX_DIM], zpad], axis=0)

    def tile_rows(j):
        return pl.ds(pl.multiple_of(j * LANES, LANES), LANES)

    def class_rows(j):
        return tile_rows(jnp.clip(j - jd + classes - 1, 0, classes - 1))

    def score_tile(j):
        d = jnp.dot(ki_ref[tile_rows(j), :], iqs_s[...], preferred_element_type=F32)
        acc = jnp.zeros((LANES, tq), F32)
        for h in range(IDX_HEADS):
            acc = acc + wt[h:h + 1] * jnp.maximum(d[:, h * tq:(h + 1) * tq], 0.0)
        return acc

    def score_trip(js, carry):
        for j in js:
            key = jnp.where(adm_ref[class_rows(j), :] > 0.0, _order_key(score_tile(j)), _f32_key_const(NEG_INF))
            key_s[j] = key
            srt = _sort_desc([key[g * SUBLANES:(g + 1) * SUBLANES] for g in range(GROUP)])
            for g in range(GROUP):
                ks_s[j, g * SUBLANES:(g + 1) * SUBLANES] = srt[g]
        return carry

    _tile_loop(nt, score_trip, 0)
    nquad = (nt + 3) // 4

    def fill_body(j, carry):
        ks_s[j] = jnp.full((LANES, tq), INT_MIN, I32)
        return carry

    lax.fori_loop(nt, nquad * 4, fill_body, 0)

    def count(pred_tile):
        def trip(js, acc):
            for j in js:
                acc = acc + _fold8(pred_tile(j))
            return acc
        return jnp.sum(_tile_loop(nt, trip, jnp.zeros((SUBLANES, tq), F32)), axis=0, keepdims=True)

    def count_ge(cand):
        def one(m):
            return jnp.where(m, 1.0, 0.0)

        def trip(q, accs):
            a8, a4, a2, a1 = accs
            for u in range(4):
                v = [ks_s[4 * q + u, g * SUBLANES:(g + 1) * SUBLANES] for g in range(GROUP)]
                t1 = v[7] >= cand
                t2 = jnp.where(t1, v[11], v[3]) >= cand
                t3 = jnp.where(t1, jnp.where(t2, v[13], v[9]), jnp.where(t2, v[5], v[1])) >= cand
                t4 = jnp.where(t1, jnp.where(t2, jnp.where(t3, v[14], v[12]), jnp.where(t3, v[10], v[8])),
                               jnp.where(t2, jnp.where(t3, v[6], v[4]), jnp.where(t3, v[2], v[0]))) >= cand
                a8, a4, a2 = a8 + one(t1), a4 + one(t2), a2 + one(t3)
                a1 = a1 + one(t4) + one(v[15] >= cand)
            return a8, a4, a2, a1

        zero = jnp.zeros((SUBLANES, tq), F32)
        a8, a4, a2, a1 = lax.fori_loop(0, nquad, trip, (zero, zero, zero, zero))
        return jnp.sum(8.0 * a8 + 4.0 * a4 + 2.0 * a2 + a1, axis=0, keepdims=True)

    kf = float(n_sel)

    def bit_body(it, thr):
        cand = thr + lax.shift_left(jnp.int32(1), 31 - it)
        return jnp.where(count_ge(cand) >= kf, cand, thr)

    thr = lax.fori_loop(0, 32, bit_body, jnp.full((1, tq), INT_MIN, I32))

    need = kf - count(lambda j: jnp.where(key_s[j] > thr, 1.0, 0.0))
    n_tied = count(lambda j: jnp.where(key_s[j] == thr, 1.0, 0.0))
    surplus = jnp.where(thr > HALF_NEG_KEY, n_tied - need, 0.0)

    def tie_search():
        def tie_body(it, j0):
            cand = j0 + lax.shift_left(jnp.int32(1), idx_bits - 1 - it)
            cnt = count(lambda j: jnp.where(key_s[j] == thr, jnp.where(krow + j * LANES < cand, 1.0, 0.0), 0.0))
            return jnp.where(cnt < need, cand, j0)
        return lax.fori_loop(0, idx_bits, tie_body, jnp.zeros((1, tq), I32))

    j0 = lax.cond(jnp.max(surplus) > 0.0, tie_search, lambda: jnp.full((1, tq), INT_MAX, I32))

    def mask_trip(js, carry):
        for j in js:
            key = key_s[j]
            sel = jnp.where(key > thr, 1.0, jnp.where(key == thr, jnp.where(krow + j * LANES <= j0, 1.0, 0.0), 0.0))
            sel = jnp.where(key > HALF_NEG_KEY, sel, 0.0)
            mask_s[j] = jnp.where(sel > 0.0, 0.0, NEG_INF)
        return carry

    _tile_loop(nt, mask_trip, 0)

    scale = ATT_HEAD_DIM ** -0.5
    heads = range(ATT_HEADS)
    qh = [qt[h * ATT_HEAD_DIM:(h + 1) * ATT_HEAD_DIM] for h in heads]

    def log_trip(js, mx):
        mx = list(mx)
        for j in js:
            band_rows = class_rows(j)
            msk = mask_s[j]
            for h in heads:
                kh = k_ref[tile_rows(j), h * ATT_HEAD_DIM:(h + 1) * ATT_HEAD_DIM]
                lg = jnp.dot(kh, qh[h], preferred_element_type=F32) * scale + band_ref[h, band_rows, :] + msk
                log_s[h, j] = lg
                mx[h] = jnp.maximum(mx[h], _fold8(lg, jnp.maximum))
        return tuple(mx)

    mx = _tile_loop(nt, log_trip, tuple(jnp.full((SUBLANES, tq), NEG_INF, F32) for _ in heads))
    m = [jnp.max(mx[h], axis=0, keepdims=True) for h in heads]
    acc_s[...] = jnp.zeros(acc_s.shape, F32)

    def pv_trip(js, ls):
        ls = list(ls)
        for h in heads:
            acc = acc_s[h]
            for j in js:
                p = jnp.exp(log_s[h, j] - m[h])
                ls[h] = ls[h] + _fold8(p)
                vth = vt_ref[j, h * ATT_HEAD_DIM:(h + 1) * ATT_HEAD_DIM, :]
                acc = acc + jnp.dot(vth, p.astype(BF16), preferred_element_type=F32)
            acc_s[h] = acc
        return tuple(ls)

    ls = _tile_loop(nt, pv_trip, tuple(jnp.zeros((SUBLANES, tq), F32) for _ in heads))
    for h in heads:
        ot = acc_s[h] / jnp.sum(ls[h], axis=0, keepdims=True)
        for u in range(qb):
            o_ref[u * LANES:(u + 1) * LANES, h * ATT_HEAD_DIM:(h + 1) * ATT_HEAD_DIM] = jnp.transpose(
                ot[:, u * LANES:(u + 1) * LANES]).astype(o_ref.dtype)


def _attention(aqt, iqt, tailt, keys_k, keys_v, keys_i, band, b, t, past_len):
    l = past_len + t
    tq = LANES
    nqb = -(-t // tq)
    tp = nqb * tq
    assert past_len % LANES == 0 and (t % tq == 0 or nqb == 1)
    jd0 = past_len // LANES
    ntiles = jd0 + nqb
    ntp = ntiles
    lp = ntp * LANES
    n_sel = min(TOPK_MAX, l // 4)
    if tp != t:
        padq = lambda a: jnp.pad(a.reshape(-1, b, t), ((0, 0), (0, 0), (0, tp - t))).reshape(-1, b * tp)
        aqt, iqt, tailt = padq(aqt), padq(iqt), padq(tailt)
    w = ATT_HEADS * ATT_HEAD_DIM
    pad = ((0, 0), (0, lp - l), (0, 0))
    kk = jnp.pad(keys_k, pad)
    vt = jnp.pad(keys_v, pad).reshape(b, ntp, LANES, w).transpose(0, 1, 3, 2)
    ki2 = jnp.pad(keys_i, ((0, 0), (0, lp - l), (0, LANES - IDX_DIM)))
    cc = jnp.arange(LANES)[:, None]
    tt = jnp.arange(LANES)[None, :]
    adm = ((cc // CHUNK <= tt // CHUNK) & (cc < l - (ntiles - 1) * LANES)).astype(F32)
    qb = 2 if nqb % 2 == 0 else 1
    wq = qb * LANES
    classes = BAND_TILES + qb - 1
    band_w, adm_w = [], []
    for r in range(classes):
        rel = [r - (classes - 1) + (qb - 1 - u) for u in range(qb)]
        band_w.append(jnp.concatenate(
            [band[:, min(max(x + BAND_TILES - 1, 0), BAND_TILES - 1) * LANES:][:, :LANES] for x in rel], axis=2))
        adm_w.append(jnp.concatenate(
            [adm if x == 0 else jnp.full((LANES, LANES), 1.0 if x < 0 else 0.0, F32) for x in rel], axis=1))
    band_w = jnp.concatenate(band_w, axis=1)
    adm_w = jnp.concatenate(adm_w, axis=0)
    nsteps = nqb // qb
    qspec = lambda width: pl.BlockSpec((wq, width), lambda bi, qi: (bi * nsteps + qi, 0))
    qtspec = lambda rows: pl.BlockSpec((rows, wq), lambda bi, qi: (0, bi * nsteps + qi))
    kspec = lambda width: pl.BlockSpec((None, lp, width), lambda bi, qi: (bi, 0, 0))
    kern = functools.partial(_attention_kernel, jd0=jd0, n_sel=n_sel, idx_bits=max((lp - 1).bit_length(), 1))
    out = pl.pallas_call(
        kern,
        grid=(b, nsteps),
        in_specs=[qtspec(w), qtspec(IDX_HEADS * IDX_DIM), qtspec(LANES),
                  kspec(w), pl.BlockSpec((None, ntp, w, LANES), lambda bi, qi: (bi, 0, 0, 0)), kspec(LANES),
                  pl.BlockSpec((ATT_HEADS, classes * LANES, wq), lambda bi, qi: (0, 0, 0)),
                  pl.BlockSpec((classes * LANES, wq), lambda bi, qi: (0, 0))],
        out_specs=qspec(w),
        out_shape=jax.ShapeDtypeStruct((b * tp, w), BF16),
        scratch_shapes=[pltpu.VMEM((LANES, IDX_HEADS * wq), BF16),
                        pltpu.VMEM((ntp, LANES, wq), I32), pltpu.VMEM((-(-ntp // 4) * 4, LANES, wq), I32),
                        pltpu.VMEM((ntp, LANES, wq), F32),
                        pltpu.VMEM((ATT_HEADS, ntp, LANES, wq), F32), pltpu.VMEM((ATT_HEADS, ATT_HEAD_DIM, wq), F32)],
        compiler_params=_cparams(("parallel", "arbitrary")),
        name="attention",
    )(aqt, iqt, tailt, kk, vt, ki2, band_w, adm_w)
    if tp != t:
        out = out.reshape(b, tp, w)[:, :t].reshape(b * t, w)
    return out


def _store_tile_major(ref, x):
    rows, width = x.shape
    nk = width // LANES
    for c in range(nk):
        ref[pl.ds(c, rows, stride=nk), :] = x[:, c * LANES:(c + 1) * LANES]


def _load_tile_major(ref, rows, nk, first=0, stride=None):
    stride = nk if stride is None else stride
    return jnp.concatenate([ref[pl.ds(first + c, rows, stride=stride), :] for c in range(nk)], axis=1)


def _outproj_kernel(x_ref, oret_ref, oatt_ref, wo_ref, g_ref, wr_ref, br_ref, *rest, own_steps):
    outs = rest[-4:]
    step = pl.program_id(0)

    @pl.when(step < own_steps)
    def _():
        _outproj_body(x_ref, oret_ref, oatt_ref, wo_ref, g_ref, wr_ref, br_ref, *outs)

    @pl.when(step >= own_steps)
    def _():
        outs[1][...] = jnp.zeros(outs[1].shape, F32)


def _outproj_body(x_ref, oret_ref, oatt_ref, wo_ref, g_ref, wr_ref, br_ref, x1_ref, h2_ref, e_ref, gt_ref):
    mixed = jnp.concatenate([oret_ref[...], oatt_ref[...]], axis=1)
    x1 = x_ref[...] + jnp.dot(mixed, wo_ref[...], preferred_element_type=F32)
    x1_ref[...] = x1
    h2 = (x1 * lax.rsqrt(jnp.mean(jnp.square(x1), -1, keepdims=True) + EPS)) * g_ref[...]
    _store_tile_major(h2_ref, h2)
    lg = lax.dot_general(wr_ref[...], h2.astype(BF16), (((1,), (1,)), ((), ())),
                         preferred_element_type=F32) + br_ref[...]
    tm = lg.shape[1]
    ridx = lax.broadcasted_iota(I32, (SUBLANES, tm), 0).astype(F32)
    first = lambda hit: jnp.min(jnp.where(hit, ridx, float(SUBLANES)), 0, keepdims=True)
    gl = lg[:SUBLANES]
    gmax = jnp.max(gl, 0, keepdims=True)
    p_top = 1.0 / jnp.sum(jnp.exp(gl - gmax), 0, keepdims=True)
    g_top = first(gl == gmax)
    el = jnp.zeros((EXPERTS_PER_GROUP, tm), F32)
    for g in range(N_GROUPS):
        el = jnp.where(g_top == float(g), lg[(1 + g) * SUBLANES:(2 + g) * SUBLANES], el)
    v1 = jnp.max(el, 0, keepdims=True)
    i1 = first(el == v1)
    el2 = jnp.where(ridx == i1, -jnp.inf, el)
    v2 = jnp.max(el2, 0, keepdims=True)
    i2 = first(el2 == v2)
    e2 = jnp.exp(v2 - v1)
    den = 1.0 + e2
    e_ref[...] = jnp.concatenate([g_top * EXPERTS_PER_GROUP + i1, g_top * EXPERTS_PER_GROUP + i2], 0).astype(I32)
    gt_ref[...] = jnp.concatenate([1.0 / den, e2 / den], 0) * p_top


def _outproj_router(x2, o_ret, o_att, w_out, g_ffn, w_group, b_group, w_er, b_er, h2_all, row0, rows_all):
    n, d = x2.shape
    shared = () if h2_all is None else (h2_all,)
    tm = _row_tile(n, 512)
    other = 0 if shared else rows_all - n
    if other:
        assert row0 == 0
        tm = math.gcd(tm, other)
    assert row0 % tm == 0
    blk0 = row0 // tm
    own_steps = n // tm
    nk = d // LANES
    assert N_GROUPS <= SUBLANES and EXPERTS_PER_GROUP == SUBLANES
    gpad = SUBLANES - N_GROUPS
    nrow = SUBLANES + N_EXPERTS
    rpad = -nrow % 16
    wr = jnp.concatenate([w_group.T, jnp.zeros((gpad, d), F32), w_er.reshape(d, N_EXPERTS).T,
                          jnp.zeros((rpad, d), F32)], 0).astype(BF16)
    br = jnp.concatenate([b_group, jnp.full((gpad,), NEG_INF, F32), b_er.reshape(N_EXPERTS),
                          jnp.zeros((rpad,), F32)]).reshape(nrow + rpad, 1)
    last = own_steps - 1
    rspec = lambda w: pl.BlockSpec((tm, w), lambda i: (jnp.minimum(i, last), 0))
    tspec = pl.BlockSpec((TOP_E, tm), lambda i: (0, jnp.minimum(i, last)))
    cspec = lambda r, c: pl.BlockSpec((r, c), lambda i: (0, 0))
    mw = w_out.shape[0]
    x1, h2_all, expert_t, gate_t = pl.pallas_call(
        functools.partial(_outproj_kernel, own_steps=own_steps),
        grid=(own_steps + other // tm,),
        in_specs=[rspec(d), rspec(o_ret.shape[1]), rspec(o_att.shape[1]), cspec(mw, d), cspec(1, d),
                  cspec(nrow + rpad, d), cspec(nrow + rpad, 1)] + [pl.BlockSpec(memory_space=pl.ANY) for _ in shared],
        out_specs=[rspec(d), pl.BlockSpec((tm * nk, LANES), lambda i: (blk0 + i, 0)), tspec, tspec],
        out_shape=[jax.ShapeDtypeStruct((n, d), F32), jax.ShapeDtypeStruct((rows_all * nk, LANES), F32),
                   jax.ShapeDtypeStruct((TOP_E, n), I32), jax.ShapeDtypeStruct((TOP_E, n), F32)],
        input_output_aliases={7: 1} if shared else {},
        compiler_params=_cparams(("arbitrary",)),
        name="outproj_router",
    )(x2, o_ret, o_att, w_out.astype(BF16), g_ffn.reshape(1, d), wr, br, *shared)
    return x1, h2_all, expert_t.T, gate_t.T


def _expert_kernel(blk_e_ref, tok0_ref, tokn_ref, slot_ref, h_ref, wg_ref, wu_ref, wd_ref, o_ref,
                   xbuf, obuf, gsem, ssem):
    i = pl.program_id(0)
    nb = pl.num_programs(0)
    nk = wg_ref.shape[0] // LANES
    rb = xbuf.shape[0] // (2 * nk)
    blk = rb * nk
    cur = i % 2
    nxt = 1 - cur

    def lines(first, count):
        return pl.ds(pl.multiple_of(first, nk), count)

    def for_rows(start_row):
        def body(g, carry):
            for u in range(DMA_UNROLL):
                start_row(g * DMA_UNROLL + u, u % 2)
            return carry
        lax.fori_loop(0, rb // DMA_UNROLL, body, 0)

    def start_gather(tok_ref, buf):
        for_rows(lambda r, pri: pltpu.make_async_copy(
            h_ref.at[lines(tok_ref[0, 0, r], nk)], xbuf.at[lines(buf * blk + r * nk, nk)],
            gsem.at[buf]).start(priority=pri))

    def wait_gather(buf):
        pltpu.make_async_copy(h_ref.at[pl.ds(0, blk)], xbuf.at[lines(buf * blk, blk)], gsem.at[buf]).wait()

    def wait_scatter(buf):
        pltpu.make_async_copy(obuf.at[lines(buf * blk, blk)], o_ref.at[pl.ds(0, blk)], ssem.at[buf]).wait()

    @pl.when(i == 0)
    def _():
        start_gather(tok0_ref, 0)

    @pl.when(i + 1 < nb)
    def _():
        start_gather(tokn_ref, nxt)

    wait_gather(cur)
    base = cur * blk
    xb = jnp.concatenate([xbuf[pl.ds(base + c, rb, stride=nk), :] for c in range(nk)], axis=1).astype(BF16)
    hid = jax.nn.silu(jnp.dot(xb, wg_ref[...].astype(BF16), preferred_element_type=F32))
    hid = hid * jnp.dot(xb, wu_ref[...].astype(BF16), preferred_element_type=F32)
    out = jnp.dot(hid.astype(BF16), wd_ref[...].astype(BF16), preferred_element_type=F32)

    @pl.when(i >= 2)
    def _():
        wait_scatter(cur)

    for c in range(nk):
        obuf[pl.ds(base + c, rb, stride=nk), :] = out[:, c * LANES:(c + 1) * LANES]

    for_rows(lambda r, pri: pltpu.make_async_copy(
        obuf.at[lines(base + r * nk, nk)], o_ref.at[lines(slot_ref[0, 0, r], nk)], ssem.at[cur]).start(priority=pri))

    @pl.when(i == nb - 1)
    def _():
        wait_scatter(cur)

        @pl.when(nb >= 2)
        def _():
            wait_scatter(nxt)


def _experts(h2, src_tok, out_slot, blk_e, w_gate, w_up, w_down, rb):
    d, de = w_gate.shape[1:]
    nk = d // LANES
    rows = src_tok.shape[0]
    nb = rows // rb
    idx3 = lambda a: (a * nk).reshape(nb, 1, rb)
    ispec = lambda f: pl.BlockSpec((1, 1, rb), f, memory_space=pltpu.SMEM)
    wspec = lambda r, c: pl.BlockSpec((None, r, c), lambda i, be: (be[i], 0, 0))
    return pl.pallas_call(
        _expert_kernel,
        grid_spec=pltpu.PrefetchScalarGridSpec(
            num_scalar_prefetch=1,
            grid=(nb,),
            in_specs=[ispec(lambda i, be: (0, 0, 0)), ispec(lambda i, be: (jnp.minimum(i + 1, nb - 1), 0, 0)),
                      ispec(lambda i, be: (i, 0, 0)), pl.BlockSpec(memory_space=pl.ANY),
                      wspec(d, de), wspec(d, de), wspec(de, d)],
            out_specs=pl.BlockSpec(memory_space=pl.ANY),
            scratch_shapes=[pltpu.VMEM((2 * rb * nk, LANES), F32), pltpu.VMEM((2 * rb * nk, LANES), F32),
                            pltpu.SemaphoreType.DMA((2,)), pltpu.SemaphoreType.DMA((2,))],
        ),
        out_shape=jax.ShapeDtypeStruct((rows * nk, LANES), F32),
        compiler_params=_cparams(("arbitrary",)),
        name="experts",
    )(blk_e, idx3(src_tok), idx3(src_tok), idx3(out_slot), h2, w_gate, w_up, w_down)


def _route_plan(expert, rb):
    n = expert.shape[0]
    a = n * TOP_E
    flat_e = expert.reshape(a)
    counts = jnp.sum((flat_e[:, None] == jnp.arange(N_EXPERTS, dtype=I32)[None, :]).astype(I32), axis=0)
    padded = (counts + rb - 1) // rb * rb
    ends = jnp.cumsum(padded)
    n_blocks = (a + N_EXPERTS * (rb - 1) + rb - 1) // rb
    rows = n_blocks * rb
    q = jnp.arange(rb - 1, dtype=I32)[None, :]
    e = jnp.arange(N_EXPERTS, dtype=I32)[:, None]
    pad_key = jnp.where(q < (padded - counts)[:, None], 2 * e + 1, 2 * N_EXPERTS).reshape(-1)
    spare = jnp.full((rows - a - N_EXPERTS * (rb - 1),), 2 * N_EXPERTS, I32)
    pos_bits = max((rows - 1).bit_length(), 1)
    assert (2 * N_EXPERTS + 1) << pos_bits < 2 ** 31
    keyed = (jnp.concatenate([2 * flat_e, pad_key, spare]) << pos_bits) | jnp.arange(rows, dtype=I32)
    perm = jnp.sort(keyed) & ((1 << pos_bits) - 1)
    src_tok = jnp.where(perm < a, perm // TOP_E, 0)
    out_slot = perm
    blk_e = jnp.minimum(jnp.sum((ends[None, :] <= jnp.arange(n_blocks, dtype=I32)[:, None] * rb).astype(I32), axis=1),
                        N_EXPERTS - 1)
    return src_tok, out_slot, blk_e


def _combine_kernel(x1_ref, eo_ref, gt_ref, g_ref, y_ref):
    tm, d = x1_ref.shape
    nk = d // LANES
    gt = gt_ref[...]
    moe = _load_tile_major(eo_ref, tm, nk, 0, TOP_E * nk) * gt[:, 0:1]
    moe = moe + _load_tile_major(eo_ref, tm, nk, nk, TOP_E * nk) * gt[:, 1:2]
    x = x1_ref[...] + moe
    y_ref[...] = (x * lax.rsqrt(jnp.mean(jnp.square(x), -1, keepdims=True) + EPS)) * g_ref[...]


def _combine(x1, eo2, gate, g_final, row0):
    n, d = x1.shape
    tm = _row_tile(n, 512)
    assert row0 % tm == 0
    blk0 = row0 // tm
    nk = d // LANES
    rspec = lambda w: pl.BlockSpec((tm, w), lambda i: (i, 0))
    return pl.pallas_call(
        _combine_kernel,
        grid=(n // tm,),
        in_specs=[rspec(d), pl.BlockSpec((tm * TOP_E * nk, LANES), lambda i: (blk0 + i, 0)), rspec(TOP_E),
                  pl.BlockSpec((1, d), lambda i: (0, 0))],
        out_specs=rspec(d),
        out_shape=jax.ShapeDtypeStruct((n, d), F32),
        compiler_params=_cparams(("parallel",)),
        name="combine",
    )(x1, eo2, gate, g_final.reshape(1, d))


def _mixer(x, past_len, s_ret, past_k, past_v, past_ki, band, params, h2_all, row0, rows_all):
    (rel_bias, g_mix, w_in, g_ret, w_out, g_ffn, w_group, b_group, w_er, b_er, w_gate, w_up, w_down) = params
    b, t, d = x.shape
    n = b * t
    x2 = x.reshape(n, d)
    rq, rk, rv, rg, ak, av, akb, avb, ik, aqt, iqt, tailt = _inproj(x2, g_mix, w_in)
    o_ret, s_new = _retention(rq, rk, rv, rg, s_ret, g_ret, b, t, past_len)
    aw = ATT_HEADS * ATT_HEAD_DIM
    keys_k = jnp.concatenate([past_k.reshape(b, past_len, aw).astype(BF16), akb.reshape(b, t, aw)], 1)
    keys_v = jnp.concatenate([past_v.reshape(b, past_len, aw).astype(BF16), avb.reshape(b, t, aw)], 1)
    keys_i = jnp.concatenate([past_ki.astype(BF16), ik.reshape(b, t, IDX_DIM).astype(BF16)], 1)
    o_att = _attention(aqt, iqt, tailt, keys_k, keys_v, keys_i, band, b, t, past_len)
    x1, h2_all, expert, gate = _outproj_router(x2, o_ret, o_att, w_out, g_ffn, w_group, b_group, w_er, b_er,
                                               h2_all, row0, rows_all)
    caches = (ak.reshape(1, b, t, ATT_HEADS, ATT_HEAD_DIM), av.reshape(1, b, t, ATT_HEADS, ATT_HEAD_DIM),
              ik.reshape(1, b, t, IDX_DIM), s_new[None])
    return x1, h2_all, expert, gate, caches


def kernel(x_prompt, x_sample, cache_attn_k, cache_attn_v, cache_idx_k, state_ret, rel_bias, g_mix, w_in, g_ret, w_out, g_ffn, w_group, b_group, w_expert_router, b_expert_router, w_gate, w_up, w_down, g_final):
    assert g_mix.shape[0] == 1, "single-layer model"
    params = (rel_bias, g_mix[0], w_in[0], g_ret[0], w_out[0], g_ffn[0], w_group[0], b_group[0],
              w_expert_router[0], b_expert_router[0], w_gate[0], w_up[0], w_down[0])
    band = _bias_band(rel_bias)
    nb = x_prompt.shape[0]
    past_len = cache_attn_k.shape[2]
    dt = x_prompt.dtype
    empty_kv = jnp.zeros((nb, 0, ATT_HEADS, ATT_HEAD_DIM), dt)
    empty_ki = jnp.zeros((nb, 0, IDX_DIM), dt)
    s0 = jnp.zeros((nb, RET_HEADS, RET_DK, RET_DV), dt)
    n_p = x_prompt.shape[0] * x_prompt.shape[1]
    n_s = x_sample.shape[0] * x_sample.shape[1]
    x1p, h2_all, ep, gp, caches_p = _mixer(x_prompt, 0, s0, empty_kv, empty_kv, empty_ki, band, params,
                                           None, 0, n_p + n_s)
    x1s, h2_all, es, gs, caches_s = _mixer(x_sample, past_len, state_ret[0], cache_attn_k[0], cache_attn_v[0],
                                           cache_idx_k[0], band, params, h2_all, n_p, n_p + n_s)
    rb = 256 if n_p + n_s >= 8192 else 128
    src_tok, out_slot, blk_e = _route_plan(jnp.concatenate([ep, es], 0), rb)
    eo = _experts(h2_all, src_tok, out_slot, blk_e, w_gate[0], w_up[0], w_down[0], rb)
    yp = _combine(x1p, eo, gp, g_final, 0).reshape(x_prompt.shape)
    ys = _combine(x1s, eo, gs, g_final, n_p).reshape(x_sample.shape)
    return (yp, ys) + caches_p + caches_s
```

```python
import functools
import math

import jax
import jax.numpy as jnp
import numpy as np
from jax import lax
from jax.experimental import pallas as pl
from jax.experimental.pallas import tpu as pltpu

F32 = jnp.float32
BF16 = jnp.bfloat16
I32 = jnp.int32

CHUNK = 64
RET_HEADS = 4
RET_DK = 128
RET_DV = 128
ROPE_BASE = 10000.0
ATT_HEADS = 4
ATT_HEAD_DIM = 128
IDX_HEADS = 8
IDX_DIM = 64
TOPK_MAX = 256
NUM_BUCKETS = 32
MAX_DISTANCE = 128
N_GROUPS = 4
EXPERTS_PER_GROUP = 8
N_EXPERTS = N_GROUPS * EXPERTS_PER_GROUP
TOP_E = 2
EPS = 1e-6
NEG_INF = -1e30

LANES = 128
SUBLANES = 8
VMEM_LIMIT = 56 * 1024 * 1024
RET_CHUNK = 256
DMA_UNROLL = 8
INT_MIN = -(2 ** 31)
INT_MAX = 2 ** 31 - 1
BAND_TILES = 3
assert (BAND_TILES - 2) * LANES + 1 >= MAX_DISTANCE
assert math.log(IDX_DIM, 4).is_integer()


def _f32_key_const(v):
    b = int(np.array(v, np.float32).view(np.int32))
    return b ^ ((b >> 31) & 0x7FFFFFFF)


HALF_NEG_KEY = _f32_key_const(0.5 * NEG_INF)


def _cparams(sem):
    return pltpu.CompilerParams(dimension_semantics=sem, vmem_limit_bytes=VMEM_LIMIT)


def _row_tile(n, want=256):
    return want if n % want == 0 else n


def _inproj_kernel(x_ref, g_ref, wm_ref, wt_ref, wqt_ref, wiqt_ref, wtt_ref,
                   rq, rk, rv, rg, ak, av, akb, avb, ik, aqt, iqt, tailt):
    x = x_ref[...]
    h = (x * lax.rsqrt(jnp.mean(jnp.square(x), -1, keepdims=True) + EPS)) * g_ref[...]
    hb = h.astype(BF16)

    def proj(i):
        return jnp.dot(hb, wm_ref[:, i * 512:(i + 1) * 512], preferred_element_type=F32)

    def proj_t(wt_rows):
        return lax.dot_general(wt_rows, hb, (((1,), (1,)), ((), ())), preferred_element_type=F32)

    rq[...] = proj(0)
    rk[...] = proj(1)
    rv[...] = proj(2).astype(BF16)
    rg[...] = proj(3)
    aqt[...] = proj_t(wqt_ref[...]).astype(BF16)
    k = proj(5)
    _store_tile_major(ak, k)
    akb[...] = k.astype(BF16)
    v = proj(6)
    _store_tile_major(av, v)
    avb[...] = v.astype(BF16)
    iqt[...] = proj_t(wiqt_ref[...]).astype(BF16)
    ik[...] = jnp.dot(hb, wt_ref[...], preferred_element_type=F32)[:, :IDX_DIM]
    tailt[...] = proj_t(wtt_ref[...])


def _inproj(x2, g_mix, w_in):
    n, d = x2.shape
    tm = _row_tile(n, 512)
    wm = w_in[:, :4096].astype(BF16)
    wt = jnp.pad(w_in[:, 4096:], ((0, 0), (0, LANES - (w_in.shape[1] - 4096)))).astype(BF16)
    rspec = lambda w, lines=1: pl.BlockSpec((tm * lines, w), lambda i: (i, 0))
    outs = [(512, F32, 1), (512, F32, 1), (512, BF16, 1), (512, F32, 1),
            (LANES, F32, ATT_HEADS), (LANES, F32, ATT_HEADS),
            (512, BF16, 1), (512, BF16, 1), (IDX_DIM, F32, 1)]
    outs_t = [(512, BF16), (512, BF16), (LANES, F32)]
    whole = lambda a: pl.BlockSpec(a.shape, lambda i: (0, 0), pipeline_mode=pl.Buffered(1))
    wqt, wiqt, wtt = wm[:, 4 * 512:5 * 512].T, wm[:, 7 * 512:8 * 512].T, wt.T
    return pl.pallas_call(
        _inproj_kernel,
        grid=(n // tm,),
        in_specs=[rspec(d), pl.BlockSpec((1, d), lambda i: (0, 0)), whole(wm), whole(wt),
                  whole(wqt), whole(wiqt), whole(wtt)],
        out_specs=[rspec(w, ln) for w, _, ln in outs] + [pl.BlockSpec((w, tm), lambda i: (0, i)) for w, _ in outs_t],
        out_shape=[jax.ShapeDtypeStruct((n * ln, w), dt) for w, dt, ln in outs]
        + [jax.ShapeDtypeStruct((w, n), dt) for w, dt in outs_t],
        compiler_params=_cparams(("parallel",)),
        name="inproj",
    )(x2, g_mix.reshape(1, d), wm, wt, wqt, wiqt, wtt)


def _retention_kernel(cd_ref, rq_ref, rk_ref, rv_ref, rg_ref, cos_ref, sin_ref, dmat_ref, qd_ref, kd_ref,
                      gret_ref, s0_ref, o_ref, s_ref):
    c = pl.program_id(1)

    @pl.when(c == 0)
    def _():
        s_ref[...] = s0_ref[...]

    cosf = cos_ref[...]
    sinf = sin_ref[...]
    half = RET_DK // 2

    def rot(x):
        return x * cosf + pltpu.roll(x, half, 1) * sinf

    for h in range(RET_HEADS):
        sl = slice(h * RET_DK, (h + 1) * RET_DK)
        q = rot(rq_ref[:, sl])
        k = rot(rk_ref[:, sl]) * (RET_DK ** -0.5)
        v = rv_ref[:, sl]
        s = s_ref[0, h]
        sc = lax.dot_general(q.astype(BF16), k.astype(BF16), (((1,), (1,)), ((), ())),
                             preferred_element_type=F32) * dmat_ref[h]
        o = jnp.dot(sc.astype(BF16), v, preferred_element_type=F32)
        o = o + jnp.dot((q * qd_ref[:, sl]).astype(BF16), s.astype(BF16), preferred_element_type=F32)
        kdt = jnp.transpose(k * kd_ref[:, sl]).astype(BF16)
        s_ref[0, h] = cd_ref[h] * s + jnp.dot(kdt, v, preferred_element_type=F32)
        mu = jnp.mean(o, -1, keepdims=True)
        var = jnp.mean(jnp.square(o - mu), -1, keepdims=True)
        on = (o - mu) * lax.rsqrt(var + EPS) * gret_ref[:, sl]
        o_ref[:, sl] = (jax.nn.silu(rg_ref[:, sl]) * on).astype(o_ref.dtype)


def _retention(rq, rk, rv, rg, s0, g_ret, b, t, past_len):
    cl = min(RET_CHUNK, t)
    nc = t // cl
    half = RET_DK // 2
    pos = (past_len + jnp.arange(t)).astype(F32)
    inv = ROPE_BASE ** (-jnp.arange(half, dtype=F32) / half)
    ang = pos[:, None] * inv[None, :]
    cosf = jnp.concatenate([jnp.cos(ang), jnp.cos(ang)], -1)
    sinf = jnp.concatenate([-jnp.sin(ang), jnp.sin(ang)], -1)
    log_g = jnp.log1p(-jnp.exp2(-5.0 - jnp.arange(RET_HEADS, dtype=F32)))
    i = jnp.arange(cl, dtype=F32)
    diff = i[:, None] - i[None, :]
    dmat = jnp.where(diff[None] >= 0, jnp.exp(jnp.maximum(diff, 0.0)[None] * log_g[:, None, None]), 0.0)
    kd = jnp.repeat(jnp.exp((cl - 1.0 - i)[:, None] * log_g[None, :]), RET_DK, axis=1)
    qd = jnp.repeat(jnp.exp((i + 1.0)[:, None] * log_g[None, :]), RET_DK, axis=1)
    cd = jnp.exp(cl * log_g)
    w = RET_HEADS * RET_DK
    rspec = pl.BlockSpec((cl, w), lambda bi, ci: (bi * nc + ci, 0))
    cspec = lambda shape: pl.BlockSpec(shape, lambda bi, ci: (0,) * len(shape))
    sspec = pl.BlockSpec((1, RET_HEADS, RET_DK, RET_DV), lambda bi, ci: (bi, 0, 0, 0))
    return pl.pallas_call(
        _retention_kernel,
        grid=(b, nc),
        in_specs=[pl.BlockSpec(memory_space=pltpu.SMEM), rspec, rspec, rspec, rspec,
                  pl.BlockSpec((cl, RET_DK), lambda bi, ci: (ci, 0)),
                  pl.BlockSpec((cl, RET_DK), lambda bi, ci: (ci, 0)),
                  cspec((RET_HEADS, cl, cl)), cspec((cl, w)), cspec((cl, w)), cspec((1, w)), sspec],
        out_specs=[rspec, sspec],
        out_shape=[jax.ShapeDtypeStruct((b * t, w), BF16),
                   jax.ShapeDtypeStruct((b, RET_HEADS, RET_DK, RET_DV), F32)],
        compiler_params=_cparams(("parallel", "arbitrary")),
        name="retention",
    )(cd, rq, rk, rv, rg, cosf, sinf, dmat, qd, kd, g_ret.reshape(1, w), s0)


def _t5_bucket(rel):
    nb = NUM_BUCKETS // 2
    max_exact = nb // 2
    base = jnp.where(rel > 0, nb, 0)
    n = jnp.abs(rel)
    nf = jnp.maximum(n, 1).astype(F32)
    large = max_exact + (jnp.log(nf / max_exact) / math.log(MAX_DISTANCE / max_exact) * (nb - max_exact)).astype(I32)
    large = jnp.minimum(large, nb - 1)
    return base + jnp.where(n < max_exact, n, large)


def _band_kernel(rb_ref, bucket_ref, band_ref):
    bucket = bucket_ref[...]
    for h in range(ATT_HEADS):
        acc = jnp.zeros(bucket.shape, F32)
        for j in range(NUM_BUCKETS):
            acc = jnp.where(bucket == j, rb_ref[j, h], acc)
        band_ref[h] = acc


def _bias_band(rel_bias):
    c = jnp.arange(BAND_TILES * LANES, dtype=I32)[:, None]
    t = jnp.arange(LANES, dtype=I32)[None, :]
    bucket = _t5_bucket(c - (BAND_TILES - 1) * LANES - t)
    return pl.pallas_call(
        _band_kernel,
        in_specs=[pl.BlockSpec(memory_space=pltpu.SMEM), pl.BlockSpec(memory_space=pltpu.VMEM)],
        out_specs=pl.BlockSpec(memory_space=pltpu.VMEM),
        out_shape=jax.ShapeDtypeStruct((ATT_HEADS, BAND_TILES * LANES, LANES), F32),
        name="bias_band",
    )(rel_bias, bucket)


def _order_key(s):
    bits = lax.bitcast_convert_type(s, I32)
    return bits ^ ((bits >> 31) & 0x7FFFFFFF)


def _fold8(x, op=jnp.add):
    parts = [x[i * SUBLANES:(i + 1) * SUBLANES] for i in range(x.shape[0] // SUBLANES)]
    while len(parts) > 1:
        parts = [op(parts[i], parts[i + 1]) for i in range(0, len(parts), 2)]
    return parts[0]


GROUP = LANES // SUBLANES


def _sort_network(n):
    pairs = []
    p = 1
    while p < n:
        k = p
        while k >= 1:
            for j in range(k % p, n - k, 2 * k):
                for i in range(min(k, n - j - k)):
                    if (i + j) // (2 * p) == (i + j + k) // (2 * p):
                        pairs.append((i + j, i + j + k))
            k //= 2
        p *= 2
    return pairs


def _sort_desc(vals):
    vals = list(vals)
    for a, b in _sort_network(len(vals)):
        vals[a], vals[b] = jnp.maximum(vals[a], vals[b]), jnp.minimum(vals[a], vals[b])
    return vals


def _tile_loop(nt, trip, carry):
    def run(first, trips, width, carry):
        return lax.fori_loop(0, trips, lambda i, c: trip([first + i * width + u for u in range(width)], c), carry)
    carry = run(0, nt // 4, 4, carry)
    carry = run(nt // 4 * 4, (nt // 2) % 2, 2, carry)
    return run(nt // 2 * 2, nt % 2, 1, carry)


def _attention_kernel(q_ref, iq_ref, tail_ref, k_ref, vt_ref, ki_ref, band_ref, adm_ref, o_ref,
                      iqs_s, key_s, ks_s, mask_s, log_s, acc_s, *, jd0, n_sel, idx_bits):
    tq = q_ref.shape[1]
    qb = tq // LANES
    jd = jd0 + qb * pl.program_id(1) + qb - 1
    nt = jd + 1
    classes = band_ref.shape[1] // LANES
    krow = lax.broadcasted_iota(I32, (LANES, tq), 0)

    qt = q_ref[...]
    iqt = iq_ref[...]
    wt = tail_ref[IDX_DIM:IDX_DIM + IDX_HEADS, :] * (IDX_HEADS ** -0.5) * (IDX_DIM ** -0.5)
    zpad = jnp.zeros((LANES - IDX_DIM, tq), BF16)
    for h in range(IDX_HEADS):
        iqs_s[:, h * tq:(h + 1) * tq] = jnp.concatenate([iqt[h * IDX_DIM:(h + 1) * IDX_DIM], zpad], axis=0)

    def tile_rows(j):
        return pl.ds(pl.multiple_of(j * LANES, LANES), LANES)

    def class_rows(j):
        return tile_rows(jnp.clip(j - jd + classes - 1, 0, classes - 1))

    def score_tile(j):
        d = jnp.dot(ki_ref[tile_rows(j), :], iqs_s[...], preferred_element_type=F32)
        acc = jnp.zeros((LANES, tq), F32)
        for h in range(IDX_HEADS):
            acc = acc + wt[h:h + 1] * jnp.maximum(d[:, h * tq:(h + 1) * tq], 0.0)
        return acc

    def score_trip(js, carry):
        for j in js:
            key = jnp.where(adm_ref[class_rows(j), :] > 0.0, _order_key(score_tile(j)), _f32_key_const(NEG_INF))
            key_s[j] = key
            srt = _sort_desc([key[g * SUBLANES:(g + 1) * SUBLANES] for g in range(GROUP)])
            for g in range(GROUP):
                ks_s[j, g * SUBLANES:(g + 1) * SUBLANES] = srt[g]
        return carry

    _tile_loop(nt, score_trip, 0)
    nquad = (nt + 3) // 4

    def fill_body(j, carry):
        ks_s[j] = jnp.full((LANES, tq), INT_MIN, I32)
        return carry

    lax.fori_loop(nt, nquad * 4, fill_body, 0)

    def count(pred_tile):
        def trip(js, acc):
            for j in js:
                acc = acc + _fold8(pred_tile(j))
            return acc
        return jnp.sum(_tile_loop(nt, trip, jnp.zeros((SUBLANES, tq), F32)), axis=0, keepdims=True)

    def count_ge(cand):
        def one(m):
            return jnp.where(m, 1.0, 0.0)

        def trip(q, accs):
            a8, a4, a2, a1 = accs
            for u in range(4):
                v = [ks_s[4 * q + u, g * SUBLANES:(g + 1) * SUBLANES] for g in range(GROUP)]
                t1 = v[7] >= cand
                t2 = jnp.where(t1, v[11], v[3]) >= cand
                t3 = jnp.where(t1, jnp.where(t2, v[13], v[9]), jnp.where(t2, v[5], v[1])) >= cand
                t4 = jnp.where(t1, jnp.where(t2, jnp.where(t3, v[14], v[12]), jnp.where(t3, v[10], v[8])),
                               jnp.where(t2, jnp.where(t3, v[6], v[4]), jnp.where(t3, v[2], v[0]))) >= cand
                a8, a4, a2 = a8 + one(t1), a4 + one(t2), a2 + one(t3)
                a1 = a1 + one(t4) + one(v[15] >= cand)
            return a8, a4, a2, a1

        zero = jnp.zeros((SUBLANES, tq), F32)
        a8, a4, a2, a1 = lax.fori_loop(0, nquad, trip, (zero, zero, zero, zero))
        return jnp.sum(8.0 * a8 + 4.0 * a4 + 2.0 * a2 + a1, axis=0, keepdims=True)

    kf = float(n_sel)

    def bit_body(it, thr):
        cand = thr + lax.shift_left(jnp.int32(1), 31 - it)
        return jnp.where(count_ge(cand) >= kf, cand, thr)

    thr = lax.fori_loop(0, 32, bit_body, jnp.full((1, tq), INT_MIN, I32))

    need = kf - count(lambda j: jnp.where(key_s[j] > thr, 1.0, 0.0))
    n_tied = count(lambda j: jnp.where(key_s[j] == thr, 1.0, 0.0))
    surplus = jnp.where(thr > HALF_NEG_KEY, n_tied - need, 0.0)

    def tie_search():
        def tie_body(it, j0):
            cand = j0 + lax.shift_left(jnp.int32(1), idx_bits - 1 - it)
            cnt = count(lambda j: jnp.where(key_s[j] == thr, jnp.where(krow + j * LANES < cand, 1.0, 0.0), 0.0))
            return jnp.where(cnt < need, cand, j0)
        return lax.fori_loop(0, idx_bits, tie_body, jnp.zeros((1, tq), I32))

    j0 = lax.cond(jnp.max(surplus) > 0.0, tie_search, lambda: jnp.full((1, tq), INT_MAX, I32))

    def mask_trip(js, carry):
        for j in js:
            key = key_s[j]
            sel = jnp.where(key > thr, 1.0, jnp.where(key == thr, jnp.where(krow + j * LANES <= j0, 1.0, 0.0), 0.0))
            sel = jnp.where(key > HALF_NEG_KEY, sel, 0.0)
            mask_s[j] = jnp.where(sel > 0.0, 0.0, NEG_INF)
        return carry

    _tile_loop(nt, mask_trip, 0)

    scale = ATT_HEAD_DIM ** -0.5
    heads = range(ATT_HEADS)
    qh = [qt[h * ATT_HEAD_DIM:(h + 1) * ATT_HEAD_DIM] for h in heads]

    def log_trip(js, mx):
        mx = list(mx)
        for j in js:
            band_rows = class_rows(j)
            msk = mask_s[j]
            for h in heads:
                kh = k_ref[tile_rows(j), h * ATT_HEAD_DIM:(h + 1) * ATT_HEAD_DIM]
                lg = jnp.dot(kh, qh[h], preferred_element_type=F32) * scale + band_ref[h, band_rows, :] + msk
                log_s[h, j] = lg
                mx[h] = jnp.maximum(mx[h], _fold8(lg, jnp.maximum))
        return tuple(mx)

    mx = _tile_loop(nt, log_trip, tuple(jnp.full((SUBLANES, tq), NEG_INF, F32) for _ in heads))
    m = [jnp.max(mx[h], axis=0, keepdims=True) for h in heads]
    acc_s[...] = jnp.zeros(acc_s.shape, F32)

    def pv_trip(js, ls):
        ls = list(ls)
        for h in heads:
            acc = acc_s[h]
            for j in js:
                p = jnp.exp(log_s[h, j] - m[h])
                ls[h] = ls[h] + _fold8(p)
                vth = vt_ref[j, h * ATT_HEAD_DIM:(h + 1) * ATT_HEAD_DIM, :]
                acc = acc + jnp.dot(vth, p.astype(BF16), preferred_element_type=F32)
            acc_s[h] = acc
        return tuple(ls)

    ls = _tile_loop(nt, pv_trip, tuple(jnp.zeros((SUBLANES, tq), F32) for _ in heads))
    for h in heads:
        ot = acc_s[h] / jnp.sum(ls[h], axis=0, keepdims=True)
        for u in range(qb):
            o_ref[u * LANES:(u + 1) * LANES, h * ATT_HEAD_DIM:(h + 1) * ATT_HEAD_DIM] = jnp.transpose(
                ot[:, u * LANES:(u + 1) * LANES]).astype(o_ref.dtype)


def _attention(aqt, iqt, tailt, keys_k, keys_v, keys_i, band, b, t, past_len):
    l = past_len + t
    tq = LANES
    nqb = -(-t // tq)
    tp = nqb * tq
    assert past_len % LANES == 0 and (t % tq == 0 or nqb == 1)
    jd0 = past_len // LANES
    ntiles = jd0 + nqb
    ntp = ntiles
    lp = ntp * LANES
    n_sel = min(TOPK_MAX, l // 4)
    if tp != t:
        padq = lambda a: jnp.pad(a.reshape(-1, b, t), ((0, 0), (0, 0), (0, tp - t))).reshape(-1, b * tp)
        aqt, iqt, tailt = padq(aqt), padq(iqt), padq(tailt)
    w = ATT_HEADS * ATT_HEAD_DIM
    pad = ((0, 0), (0, lp - l), (0, 0))
    kk = jnp.pad(keys_k, pad)
    vt = jnp.pad(keys_v, pad).reshape(b, ntp, LANES, w).transpose(0, 1, 3, 2)
    ki2 = jnp.pad(keys_i, ((0, 0), (0, lp - l), (0, LANES - IDX_DIM)))
    cc = jnp.arange(LANES)[:, None]
    tt = jnp.arange(LANES)[None, :]
    adm = ((cc // CHUNK <= tt // CHUNK) & (cc < l - (ntiles - 1) * LANES)).astype(F32)
    qb = 2 if nqb % 2 == 0 else 1
    wq = qb * LANES
    classes = BAND_TILES + qb - 1
    band_w, adm_w = [], []
    for r in range(classes):
        rel = [r - (classes - 1) + (qb - 1 - u) for u in range(qb)]
        band_w.append(jnp.concatenate(
            [band[:, min(max(x + BAND_TILES - 1, 0), BAND_TILES - 1) * LANES:][:, :LANES] for x in rel], axis=2))
        adm_w.append(jnp.concatenate(
            [adm if x == 0 else jnp.full((LANES, LANES), 1.0 if x < 0 else 0.0, F32) for x in rel], axis=1))
    band_w = jnp.concatenate(band_w, axis=1)
    adm_w = jnp.concatenate(adm_w, axis=0)
    nsteps = nqb // qb
    qspec = lambda width: pl.BlockSpec((wq, width), lambda bi, qi: (bi * nsteps + qi, 0))
    qtspec = lambda rows: pl.BlockSpec((rows, wq), lambda bi, qi: (0, bi * nsteps + qi))
    kspec = lambda width: pl.BlockSpec((None, lp, width), lambda bi, qi: (bi, 0, 0))
    kern = functools.partial(_attention_kernel, jd0=jd0, n_sel=n_sel, idx_bits=max((lp - 1).bit_length(), 1))
    out = pl.pallas_call(
        kern,
        grid=(b, nsteps),
        in_specs=[qtspec(w), qtspec(IDX_HEADS * IDX_DIM), qtspec(LANES),
                  kspec(w), pl.BlockSpec((None, ntp, w, LANES), lambda bi, qi: (bi, 0, 0, 0)), kspec(LANES),
                  pl.BlockSpec((ATT_HEADS, classes * LANES, wq), lambda bi, qi: (0, 0, 0)),
                  pl.BlockSpec((classes * LANES, wq), lambda bi, qi: (0, 0))],
        out_specs=qspec(w),
        out_shape=jax.ShapeDtypeStruct((b * tp, w), BF16),
        scratch_shapes=[pltpu.VMEM((LANES, IDX_HEADS * wq), BF16),
                        pltpu.VMEM((ntp, LANES, wq), I32), pltpu.VMEM((-(-ntp // 4) * 4, LANES, wq), I32),
                        pltpu.VMEM((ntp, LANES, wq), F32),
                        pltpu.VMEM((ATT_HEADS, ntp, LANES, wq), F32), pltpu.VMEM((ATT_HEADS, ATT_HEAD_DIM, wq), F32)],
        compiler_params=_cparams(("parallel", "arbitrary")),
        name="attention",
    )(aqt, iqt, tailt, kk, vt, ki2, band_w, adm_w)
    if tp != t:
        out = out.reshape(b, tp, w)[:, :t].reshape(b * t, w)
    return out


def _store_tile_major(ref, x):
    rows, width = x.shape
    nk = width // LANES
    for c in range(nk):
        ref[pl.ds(c, rows, stride=nk), :] = x[:, c * LANES:(c + 1) * LANES]


def _load_tile_major(ref, rows, nk, first=0, stride=None):
    stride = nk if stride is None else stride
    return jnp.concatenate([ref[pl.ds(first + c, rows, stride=stride), :] for c in range(nk)], axis=1)


def _outproj_kernel(x_ref, oret_ref, oatt_ref, wo_ref, g_ref, wr_ref, br_ref, *rest, own_steps):
    outs = rest[-4:]
    step = pl.program_id(0)

    @pl.when(step < own_steps)
    def _():
        _outproj_body(x_ref, oret_ref, oatt_ref, wo_ref, g_ref, wr_ref, br_ref, *outs)

    @pl.when(step >= own_steps)
    def _():
        outs[1][...] = jnp.zeros(outs[1].shape, F32)


def _outproj_body(x_ref, oret_ref, oatt_ref, wo_ref, g_ref, wr_ref, br_ref, x1_ref, h2_ref, e_ref, gt_ref):
    mixed = jnp.concatenate([oret_ref[...], oatt_ref[...]], axis=1)
    x1 = x_ref[...] + jnp.dot(mixed, wo_ref[...], preferred_element_type=F32)
    x1_ref[...] = x1
    h2 = (x1 * lax.rsqrt(jnp.mean(jnp.square(x1), -1, keepdims=True) + EPS)) * g_ref[...]
    _store_tile_major(h2_ref, h2)
    lg = lax.dot_general(wr_ref[...], h2.astype(BF16), (((1,), (1,)), ((), ())),
                         preferred_element_type=F32) + br_ref[...]
    tm = lg.shape[1]
    ridx = lax.broadcasted_iota(I32, (SUBLANES, tm), 0).astype(F32)
    first = lambda hit: jnp.min(jnp.where(hit, ridx, float(SUBLANES)), 0, keepdims=True)
    gl = lg[:SUBLANES]
    gmax = jnp.max(gl, 0, keepdims=True)
    p_top = 1.0 / jnp.sum(jnp.exp(gl - gmax), 0, keepdims=True)
    g_top = first(gl == gmax)
    el = jnp.zeros((EXPERTS_PER_GROUP, tm), F32)
    for g in range(N_GROUPS):
        el = jnp.where(g_top == float(g), lg[(1 + g) * SUBLANES:(2 + g) * SUBLANES], el)
    v1 = jnp.max(el, 0, keepdims=True)
    i1 = first(el == v1)
    el2 = jnp.where(ridx == i1, -jnp.inf, el)
    v2 = jnp.max(el2, 0, keepdims=True)
    i2 = first(el2 == v2)
    e2 = jnp.exp(v2 - v1)
    den = 1.0 + e2
    e_ref[...] = jnp.concatenate([g_top * EXPERTS_PER_GROUP + i1, g_top * EXPERTS_PER_GROUP + i2], 0).astype(I32)
    gt_ref[...] = jnp.concatenate([1.0 / den, e2 / den], 0) * p_top


def _outproj_router(x2, o_ret, o_att, w_out, g_ffn, w_group, b_group, w_er, b_er, h2_all, row0, rows_all):
    n, d = x2.shape
    shared = () if h2_all is None else (h2_all,)
    tm = _row_tile(n, 512)
    other = 0 if shared else rows_all - n
    if other:
        assert row0 == 0
        tm = math.gcd(tm, other)
    assert row0 % tm == 0
    blk0 = row0 // tm
    own_steps = n // tm
    nk = d // LANES
    assert N_GROUPS <= SUBLANES and EXPERTS_PER_GROUP == SUBLANES
    gpad = SUBLANES - N_GROUPS
    nrow = SUBLANES + N_EXPERTS
    rpad = -nrow % 16
    wr = jnp.concatenate([w_group.T, jnp.zeros((gpad, d), F32), w_er.reshape(d, N_EXPERTS).T,
                          jnp.zeros((rpad, d), F32)], 0).astype(BF16)
    br = jnp.concatenate([b_group, jnp.full((gpad,), NEG_INF, F32), b_er.reshape(N_EXPERTS),
                          jnp.zeros((rpad,), F32)]).reshape(nrow + rpad, 1)
    last = own_steps - 1
    rspec = lambda w: pl.BlockSpec((tm, w), lambda i: (jnp.minimum(i, last), 0))
    tspec = pl.BlockSpec((TOP_E, tm), lambda i: (0, jnp.minimum(i, last)))
    cspec = lambda r, c: pl.BlockSpec((r, c), lambda i: (0, 0))
    mw = w_out.shape[0]
    x1, h2_all, expert_t, gate_t = pl.pallas_call(
        functools.partial(_outproj_kernel, own_steps=own_steps),
        grid=(own_steps + other // tm,),
        in_specs=[rspec(d), rspec(o_ret.shape[1]), rspec(o_att.shape[1]), cspec(mw, d), cspec(1, d),
                  cspec(nrow + rpad, d), cspec(nrow + rpad, 1)] + [pl.BlockSpec(memory_space=pl.ANY) for _ in shared],
        out_specs=[rspec(d), pl.BlockSpec((tm * nk, LANES), lambda i: (blk0 + i, 0)), tspec, tspec],
        out_shape=[jax.ShapeDtypeStruct((n, d), F32), jax.ShapeDtypeStruct((rows_all * nk, LANES), F32),
                   jax.ShapeDtypeStruct((TOP_E, n), I32), jax.ShapeDtypeStruct((TOP_E, n), F32)],
        input_output_aliases={7: 1} if shared else {},
        compiler_params=_cparams(("arbitrary",)),
        name="outproj_router",
    )(x2, o_ret, o_att, w_out.astype(BF16), g_ffn.reshape(1, d), wr, br, *shared)
    return x1, h2_all, expert_t.T, gate_t.T


def _expert_kernel(blk_e_ref, tok0_ref, tokn_ref, slot_ref, h_ref, wg_ref, wu_ref, wd_ref, o_ref,
                   xbuf, obuf, gsem, ssem):
    i = pl.program_id(0)
    nb = pl.num_programs(0)
    nk = wg_ref.shape[0] // LANES
    rb = xbuf.shape[0] // (2 * nk)
    blk = rb * nk
    cur = i % 2
    nxt = 1 - cur

    def lines(first, count):
        return pl.ds(pl.multiple_of(first, nk), count)

    def for_rows(start_row):
        def body(g, carry):
            for u in range(DMA_UNROLL):
                start_row(g * DMA_UNROLL + u, u % 2)
            return carry
        lax.fori_loop(0, rb // DMA_UNROLL, body, 0)

    def start_gather(tok_ref, buf):
        for_rows(lambda r, pri: pltpu.make_async_copy(
            h_ref.at[lines(tok_ref[0, 0, r], nk)], xbuf.at[lines(buf * blk + r * nk, nk)],
            gsem.at[buf]).start(priority=pri))

    def wait_gather(buf):
        pltpu.make_async_copy(h_ref.at[pl.ds(0, blk)], xbuf.at[lines(buf * blk, blk)], gsem.at[buf]).wait()

    def wait_scatter(buf):
        pltpu.make_async_copy(obuf.at[lines(buf * blk, blk)], o_ref.at[pl.ds(0, blk)], ssem.at[buf]).wait()

    @pl.when(i == 0)
    def _():
        start_gather(tok0_ref, 0)

    @pl.when(i + 1 < nb)
    def _():
        start_gather(tokn_ref, nxt)

    wait_gather(cur)
    base = cur * blk
    xb = jnp.concatenate([xbuf[pl.ds(base + c, rb, stride=nk), :] for c in range(nk)], axis=1).astype(BF16)
    hid = jax.nn.silu(jnp.dot(xb, wg_ref[...].astype(BF16), preferred_element_type=F32))
    hid = hid * jnp.dot(xb, wu_ref[...].astype(BF16), preferred_element_type=F32)
    out = jnp.dot(hid.astype(BF16), wd_ref[...].astype(BF16), preferred_element_type=F32)

    @pl.when(i >= 2)
    def _():
        wait_scatter(cur)

    for c in range(nk):
        obuf[pl.ds(base + c, rb, stride=nk), :] = out[:, c * LANES:(c + 1) * LANES]

    for_rows(lambda r, pri: pltpu.make_async_copy(
        obuf.at[lines(base + r * nk, nk)], o_ref.at[lines(slot_ref[0, 0, r], nk)], ssem.at[cur]).start(priority=pri))

    @pl.when(i == nb - 1)
    def _():
        wait_scatter(cur)

        @pl.when(nb >= 2)
        def _():
            wait_scatter(nxt)


def _experts(h2, src_tok, out_slot, blk_e, w_gate, w_up, w_down, rb):
    d, de = w_gate.shape[1:]
    nk = d // LANES
    rows = src_tok.shape[0]
    nb = rows // rb
    idx3 = lambda a: (a * nk).reshape(nb, 1, rb)
    ispec = lambda f: pl.BlockSpec((1, 1, rb), f, memory_space=pltpu.SMEM)
    wspec = lambda r, c: pl.BlockSpec((None, r, c), lambda i, be: (be[i], 0, 0))
    return pl.pallas_call(
        _expert_kernel,
        grid_spec=pltpu.PrefetchScalarGridSpec(
            num_scalar_prefetch=1,
            grid=(nb,),
            in_specs=[ispec(lambda i, be: (0, 0, 0)), ispec(lambda i, be: (jnp.minimum(i + 1, nb - 1), 0, 0)),
                      ispec(lambda i, be: (i, 0, 0)), pl.BlockSpec(memory_space=pl.ANY),
                      wspec(d, de), wspec(d, de), wspec(de, d)],
            out_specs=pl.BlockSpec(memory_space=pl.ANY),
            scratch_shapes=[pltpu.VMEM((2 * rb * nk, LANES), F32), pltpu.VMEM((2 * rb * nk, LANES), F32),
                            pltpu.SemaphoreType.DMA((2,)), pltpu.SemaphoreType.DMA((2,))],
        ),
        out_shape=jax.ShapeDtypeStruct((rows * nk, LANES), F32),
        compiler_params=_cparams(("arbitrary",)),
        name="experts",
    )(blk_e, idx3(src_tok), idx3(src_tok), idx3(out_slot), h2, w_gate, w_up, w_down)


def _route_plan(expert, rb):
    n = expert.shape[0]
    a = n * TOP_E
    flat_e = expert.reshape(a)
    counts = jnp.sum((flat_e[:, None] == jnp.arange(N_EXPERTS, dtype=I32)[None, :]).astype(I32), axis=0)
    padded = (counts + rb - 1) // rb * rb
    ends = jnp.cumsum(padded)
    n_blocks = (a + N_EXPERTS * (rb - 1) + rb - 1) // rb
    rows = n_blocks * rb
    q = jnp.arange(rb - 1, dtype=I32)[None, :]
    e = jnp.arange(N_EXPERTS, dtype=I32)[:, None]
    pad_key = jnp.where(q < (padded - counts)[:, None], 2 * e + 1, 2 * N_EXPERTS).reshape(-1)
    spare = jnp.full((rows - a - N_EXPERTS * (rb - 1),), 2 * N_EXPERTS, I32)
    perm = jnp.argsort(jnp.concatenate([2 * flat_e, pad_key, spare]), stable=True).astype(I32)
    src_tok = jnp.where(perm < a, perm // TOP_E, 0)
    out_slot = perm
    blk_e = jnp.minimum(jnp.sum((ends[None, :] <= jnp.arange(n_blocks, dtype=I32)[:, None] * rb).astype(I32), axis=1),
                        N_EXPERTS - 1)
    return src_tok, out_slot, blk_e


def _combine_kernel(x1_ref, eo_ref, gt_ref, g_ref, y_ref):
    tm, d = x1_ref.shape
    nk = d // LANES
    gt = gt_ref[...]
    moe = _load_tile_major(eo_ref, tm, nk, 0, TOP_E * nk) * gt[:, 0:1]
    moe = moe + _load_tile_major(eo_ref, tm, nk, nk, TOP_E * nk) * gt[:, 1:2]
    x = x1_ref[...] + moe
    y_ref[...] = (x * lax.rsqrt(jnp.mean(jnp.square(x), -1, keepdims=True) + EPS)) * g_ref[...]


def _combine(x1, eo2, gate, g_final, row0):
    n, d = x1.shape
    tm = _row_tile(n, 512)
    assert row0 % tm == 0
    blk0 = row0 // tm
    nk = d // LANES
    rspec = lambda w: pl.BlockSpec((tm, w), lambda i: (i, 0))
    return pl.pallas_call(
        _combine_kernel,
        grid=(n // tm,),
        in_specs=[rspec(d), pl.BlockSpec((tm * TOP_E * nk, LANES), lambda i: (blk0 + i, 0)), rspec(TOP_E),
                  pl.BlockSpec((1, d), lambda i: (0, 0))],
        out_specs=rspec(d),
        out_shape=jax.ShapeDtypeStruct((n, d), F32),
        compiler_params=_cparams(("parallel",)),
        name="combine",
    )(x1, eo2, gate, g_final.reshape(1, d))


def _mixer(x, past_len, s_ret, past_k, past_v, past_ki, band, params, h2_all, row0, rows_all):
    (rel_bias, g_mix, w_in, g_ret, w_out, g_ffn, w_group, b_group, w_er, b_er, w_gate, w_up, w_down) = params
    b, t, d = x.shape
    n = b * t
    x2 = x.reshape(n, d)
    rq, rk, rv, rg, ak, av, akb, avb, ik, aqt, iqt, tailt = _inproj(x2, g_mix, w_in)
    o_ret, s_new = _retention(rq, rk, rv, rg, s_ret, g_ret, b, t, past_len)
    aw = ATT_HEADS * ATT_HEAD_DIM
    keys_k = jnp.concatenate([past_k.reshape(b, past_len, aw).astype(BF16), akb.reshape(b, t, aw)], 1)
    keys_v = jnp.concatenate([past_v.reshape(b, past_len, aw).astype(BF16), avb.reshape(b, t, aw)], 1)
    keys_i = jnp.concatenate([past_ki.astype(BF16), ik.reshape(b, t, IDX_DIM).astype(BF16)], 1)
    o_att = _attention(aqt, iqt, tailt, keys_k, keys_v, keys_i, band, b, t, past_len)
    x1, h2_all, expert, gate = _outproj_router(x2, o_ret, o_att, w_out, g_ffn, w_group, b_group, w_er, b_er,
                                               h2_all, row0, rows_all)
    caches = (ak.reshape(1, b, t, ATT_HEADS, ATT_HEAD_DIM), av.reshape(1, b, t, ATT_HEADS, ATT_HEAD_DIM),
              ik.reshape(1, b, t, IDX_DIM), s_new[None])
    return x1, h2_all, expert, gate, caches


def kernel(x_prompt, x_sample, cache_attn_k, cache_attn_v, cache_idx_k, state_ret, rel_bias, g_mix, w_in, g_ret, w_out, g_ffn, w_group, b_group, w_expert_router, b_expert_router, w_gate, w_up, w_down, g_final):
    assert g_mix.shape[0] == 1, "single-layer model"
    params = (rel_bias, g_mix[0], w_in[0], g_ret[0], w_out[0], g_ffn[0], w_group[0], b_group[0],
              w_expert_router[0], b_expert_router[0], w_gate[0], w_up[0], w_down[0])
    band = _bias_band(rel_bias)
    nb = x_prompt.shape[0]
    past_len = cache_attn_k.shape[2]
    dt = x_prompt.dtype
    empty_kv = jnp.zeros((nb, 0, ATT_HEADS, ATT_HEAD_DIM), dt)
    empty_ki = jnp.zeros((nb, 0, IDX_DIM), dt)
    s0 = jnp.zeros((nb, RET_HEADS, RET_DK, RET_DV), dt)
    n_p = x_prompt.shape[0] * x_prompt.shape[1]
    n_s = x_sample.shape[0] * x_sample.shape[1]
    x1p, h2_all, ep, gp, caches_p = _mixer(x_prompt, 0, s0, empty_kv, empty_kv, empty_ki, band, params,
                                           None, 0, n_p + n_s)
    x1s, h2_all, es, gs, caches_s = _mixer(x_sample, past_len, state_ret[0], cache_attn_k[0], cache_attn_v[0],
                                           cache_idx_k[0], band, params, h2_all, n_p, n_p + n_s)
    rb = 256 if n_p + n_s >= 8192 else 128
    src_tok, out_slot, blk_e = _route_plan(jnp.concatenate([ep, es], 0), rb)
    eo = _experts(h2_all, src_tok, out_slot, blk_e, w_gate[0], w_up[0], w_down[0], rb)
    yp = _combine(x1p, eo, gp, g_final, 0).reshape(x_prompt.shape)
    ys = _combine(x1s, eo, gs, g_final, n_p).reshape(x_sample.shape)
    return (yp, ys) + caches_p + caches_s
```

```python
import functools
import math

import jax
import jax.numpy as jnp
import numpy as np
from jax import lax
from jax.experimental import pallas as pl
from jax.experimental.pallas import tpu as pltpu

F32 = jnp.float32
BF16 = jnp.bfloat16
I32 = jnp.int32

CHUNK = 64
RET_HEADS = 4
RET_DK = 128
RET_DV = 128
ROPE_BASE = 10000.0
ATT_HEADS = 4
ATT_HEAD_DIM = 128
IDX_HEADS = 8
IDX_DIM = 64
TOPK_MAX = 256
NUM_BUCKETS = 32
MAX_DISTANCE = 128
N_GROUPS = 4
EXPERTS_PER_GROUP = 8
N_EXPERTS = N_GROUPS * EXPERTS_PER_GROUP
TOP_E = 2
EPS = 1e-6
NEG_INF = -1e30

LANES = 128
SUBLANES = 8
VMEM_LIMIT = 56 * 1024 * 1024
RET_CHUNK = 256
DMA_UNROLL = 8
GATHER_DEPTH = 3
INT_MIN = -(2 ** 31)
INT_MAX = 2 ** 31 - 1
BAND_TILES = 3
assert (BAND_TILES - 2) * LANES + 1 >= MAX_DISTANCE
assert math.log(IDX_DIM, 4).is_integer()


def _f32_key_const(v):
    b = int(np.array(v, np.float32).view(np.int32))
    return b ^ ((b >> 31) & 0x7FFFFFFF)


HALF_NEG_KEY = _f32_key_const(0.5 * NEG_INF)


def _cparams(sem):
    return pltpu.CompilerParams(dimension_semantics=sem, vmem_limit_bytes=VMEM_LIMIT)


def _row_tile(n, want=256):
    return want if n % want == 0 else n


def _inproj_kernel(x_ref, g_ref, wm_ref, wt_ref, wqt_ref, wiqt_ref, wtt_ref,
                   rq, rk, rv, rg, ak, av, akb, avb, ik, aqt, iqt, tailt):
    x = x_ref[...]
    h = (x * lax.rsqrt(jnp.mean(jnp.square(x), -1, keepdims=True) + EPS)) * g_ref[...]
    hb = h.astype(BF16)

    def proj(i):
        return jnp.dot(hb, wm_ref[:, i * 512:(i + 1) * 512], preferred_element_type=F32)

    def proj_t(wt_rows):
        return lax.dot_general(wt_rows, hb, (((1,), (1,)), ((), ())), preferred_element_type=F32)

    rq[...] = proj(0)
    rk[...] = proj(1)
    rv[...] = proj(2).astype(BF16)
    rg[...] = proj(3)
    aqt[...] = proj_t(wqt_ref[...]).astype(BF16)
    k = proj(5)
    _store_tile_major(ak, k)
    akb[...] = k.astype(BF16)
    v = proj(6)
    _store_tile_major(av, v)
    avb[...] = v.astype(BF16)
    iqt[...] = proj_t(wiqt_ref[...]).astype(BF16)
    ik[...] = jnp.dot(hb, wt_ref[...], preferred_element_type=F32)[:, :IDX_DIM]
    tailt[...] = proj_t(wtt_ref[...])


def _inproj(x2, g_mix, w_in):
    n, d = x2.shape
    tm = _row_tile(n, 512)
    wm = w_in[:, :4096].astype(BF16)
    wt = jnp.pad(w_in[:, 4096:], ((0, 0), (0, LANES - (w_in.shape[1] - 4096)))).astype(BF16)
    rspec = lambda w, lines=1: pl.BlockSpec((tm * lines, w), lambda i: (i, 0))
    outs = [(512, F32, 1), (512, F32, 1), (512, BF16, 1), (512, F32, 1),
            (LANES, F32, ATT_HEADS), (LANES, F32, ATT_HEADS),
            (512, BF16, 1), (512, BF16, 1), (IDX_DIM, F32, 1)]
    outs_t = [(512, BF16), (512, BF16), (LANES, F32)]
    whole = lambda a: pl.BlockSpec(a.shape, lambda i: (0, 0), pipeline_mode=pl.Buffered(1))
    wqt, wiqt, wtt = wm[:, 4 * 512:5 * 512].T, wm[:, 7 * 512:8 * 512].T, wt.T
    return pl.pallas_call(
        _inproj_kernel,
        grid=(n // tm,),
        in_specs=[rspec(d), pl.BlockSpec((1, d), lambda i: (0, 0)), whole(wm), whole(wt),
                  whole(wqt), whole(wiqt), whole(wtt)],
        out_specs=[rspec(w, ln) for w, _, ln in outs] + [pl.BlockSpec((w, tm), lambda i: (0, i)) for w, _ in outs_t],
        out_shape=[jax.ShapeDtypeStruct((n * ln, w), dt) for w, dt, ln in outs]
        + [jax.ShapeDtypeStruct((w, n), dt) for w, dt in outs_t],
        compiler_params=_cparams(("parallel",)),
        name="inproj",
    )(x2, g_mix.reshape(1, d), wm, wt, wqt, wiqt, wtt)


def _retention_kernel(cd_ref, rq_ref, rk_ref, rv_ref, rg_ref, cos_ref, sin_ref, dmat_ref, qd_ref, kd_ref,
                      gret_ref, s0_ref, o_ref, s_ref):
    c = pl.program_id(1)

    @pl.when(c == 0)
    def _():
        s_ref[...] = s0_ref[...]

    cosf = cos_ref[...]
    sinf = sin_ref[...]
    half = RET_DK // 2

    def rot(x):
        return x * cosf + pltpu.roll(x, half, 1) * sinf

    for h in range(RET_HEADS):
        sl = slice(h * RET_DK, (h + 1) * RET_DK)
        q = rot(rq_ref[:, sl])
        k = rot(rk_ref[:, sl]) * (RET_DK ** -0.5)
        v = rv_ref[:, sl]
        s = s_ref[0, h]
        sc = lax.dot_general(q.astype(BF16), k.astype(BF16), (((1,), (1,)), ((), ())),
                             preferred_element_type=F32) * dmat_ref[h]
        o = jnp.dot(sc.astype(BF16), v, preferred_element_type=F32)
        o = o + jnp.dot((q * qd_ref[:, sl]).astype(BF16), s.astype(BF16), preferred_element_type=F32)
        kdt = jnp.transpose(k * kd_ref[:, sl]).astype(BF16)
        s_ref[0, h] = cd_ref[h] * s + jnp.dot(kdt, v, preferred_element_type=F32)
        mu = jnp.mean(o, -1, keepdims=True)
        var = jnp.mean(jnp.square(o - mu), -1, keepdims=True)
        on = (o - mu) * lax.rsqrt(var + EPS) * gret_ref[:, sl]
        o_ref[:, sl] = (jax.nn.silu(rg_ref[:, sl]) * on).astype(o_ref.dtype)


def _retention(rq, rk, rv, rg, s0, g_ret, b, t, past_len):
    cl = min(RET_CHUNK, t)
    nc = t // cl
    half = RET_DK // 2
    pos = (past_len + jnp.arange(t)).astype(F32)
    inv = ROPE_BASE ** (-jnp.arange(half, dtype=F32) / half)
    ang = pos[:, None] * inv[None, :]
    cosf = jnp.concatenate([jnp.cos(ang), jnp.cos(ang)], -1)
    sinf = jnp.concatenate([-jnp.sin(ang), jnp.sin(ang)], -1)
    log_g = jnp.log1p(-jnp.exp2(-5.0 - jnp.arange(RET_HEADS, dtype=F32)))
    i = jnp.arange(cl, dtype=F32)
    diff = i[:, None] - i[None, :]
    dmat = jnp.where(diff[None] >= 0, jnp.exp(jnp.maximum(diff, 0.0)[None] * log_g[:, None, None]), 0.0)
    kd = jnp.repeat(jnp.exp((cl - 1.0 - i)[:, None] * log_g[None, :]), RET_DK, axis=1)
    qd = jnp.repeat(jnp.exp((i + 1.0)[:, None] * log_g[None, :]), RET_DK, axis=1)
    cd = jnp.exp(cl * log_g)
    w = RET_HEADS * RET_DK
    rspec = pl.BlockSpec((cl, w), lambda bi, ci: (bi * nc + ci, 0))
    cspec = lambda shape: pl.BlockSpec(shape, lambda bi, ci: (0,) * len(shape))
    sspec = pl.BlockSpec((1, RET_HEADS, RET_DK, RET_DV), lambda bi, ci: (bi, 0, 0, 0))
    return pl.pallas_call(
        _retention_kernel,
        grid=(b, nc),
        in_specs=[pl.BlockSpec(memory_space=pltpu.SMEM), rspec, rspec, rspec, rspec,
                  pl.BlockSpec((cl, RET_DK), lambda bi, ci: (ci, 0)),
                  pl.BlockSpec((cl, RET_DK), lambda bi, ci: (ci, 0)),
                  cspec((RET_HEADS, cl, cl)), cspec((cl, w)), cspec((cl, w)), cspec((1, w)), sspec],
        out_specs=[rspec, sspec],
        out_shape=[jax.ShapeDtypeStruct((b * t, w), BF16),
                   jax.ShapeDtypeStruct((b, RET_HEADS, RET_DK, RET_DV), F32)],
        compiler_params=_cparams(("parallel", "arbitrary")),
        name="retention",
    )(cd, rq, rk, rv, rg, cosf, sinf, dmat, qd, kd, g_ret.reshape(1, w), s0)


def _t5_bucket(rel):
    nb = NUM_BUCKETS // 2
    max_exact = nb // 2
    base = jnp.where(rel > 0, nb, 0)
    n = jnp.abs(rel)
    nf = jnp.maximum(n, 1).astype(F32)
    large = max_exact + (jnp.log(nf / max_exact) / math.log(MAX_DISTANCE / max_exact) * (nb - max_exact)).astype(I32)
    large = jnp.minimum(large, nb - 1)
    return base + jnp.where(n < max_exact, n, large)


def _band_kernel(rb_ref, bucket_ref, band_ref):
    bucket = bucket_ref[...]
    for h in range(ATT_HEADS):
        acc = jnp.zeros(bucket.shape, F32)
        for j in range(NUM_BUCKETS):
            acc = jnp.where(bucket == j, rb_ref[j, h], acc)
        band_ref[h] = acc


def _bias_band(rel_bias):
    c = jnp.arange(BAND_TILES * LANES, dtype=I32)[:, None]
    t = jnp.arange(LANES, dtype=I32)[None, :]
    bucket = _t5_bucket(c - (BAND_TILES - 1) * LANES - t)
    return pl.pallas_call(
        _band_kernel,
        in_specs=[pl.BlockSpec(memory_space=pltpu.SMEM), pl.BlockSpec(memory_space=pltpu.VMEM)],
        out_specs=pl.BlockSpec(memory_space=pltpu.VMEM),
        out_shape=jax.ShapeDtypeStruct((ATT_HEADS, BAND_TILES * LANES, LANES), F32),
        name="bias_band",
    )(rel_bias, bucket)


def _order_key(s):
    bits = lax.bitcast_convert_type(s, I32)
    return bits ^ ((bits >> 31) & 0x7FFFFFFF)


def _fold8(x, op=jnp.add):
    parts = [x[i * SUBLANES:(i + 1) * SUBLANES] for i in range(x.shape[0] // SUBLANES)]
    while len(parts) > 1:
        parts = [op(parts[i], parts[i + 1]) for i in range(0, len(parts), 2)]
    return parts[0]


GROUP = LANES // SUBLANES


def _sort_network(n):
    pairs = []
    p = 1
    while p < n:
        k = p
        while k >= 1:
            for j in range(k % p, n - k, 2 * k):
                for i in range(min(k, n - j - k)):
                    if (i + j) // (2 * p) == (i + j + k) // (2 * p):
                        pairs.append((i + j, i + j + k))
            k //= 2
        p *= 2
    return pairs


def _sort_desc(vals):
    vals = list(vals)
    for a, b in _sort_network(len(vals)):
        vals[a], vals[b] = jnp.maximum(vals[a], vals[b]), jnp.minimum(vals[a], vals[b])
    return vals


def _tile_loop(nt, trip, carry):
    def run(first, trips, width, carry):
        return lax.fori_loop(0, trips, lambda i, c: trip([first + i * width + u for u in range(width)], c), carry)
    carry = run(0, nt // 4, 4, carry)
    carry = run(nt // 4 * 4, (nt // 2) % 2, 2, carry)
    return run(nt // 2 * 2, nt % 2, 1, carry)


def _attention_kernel(q_ref, iq_ref, tail_ref, k_ref, vt_ref, ki_ref, band_ref, adm_ref, o_ref,
                      iqs_s, key_s, ks_s, mask_s, log_s, acc_s, *, jd0, n_sel, idx_bits):
    tq = q_ref.shape[1]
    qb = tq // LANES
    jd = jd0 + qb * pl.program_id(1) + qb - 1
    nt = jd + 1
    classes = band_ref.shape[1] // LANES
    krow = lax.broadcasted_iota(I32, (LANES, tq), 0)

    qt = q_ref[...]
    iqt = iq_ref[...]
    wt = tail_ref[IDX_DIM:IDX_DIM + IDX_HEADS, :] * (IDX_HEADS ** -0.5) * (IDX_DIM ** -0.5)
    zpad = jnp.zeros((LANES - IDX_DIM, tq), BF16)
    for h in range(IDX_HEADS):
        iqs_s[:, h * tq:(h + 1) * tq] = jnp.concatenate([iqt[h * IDX_DIM:(h + 1) * IDX_DIM], zpad], axis=0)

    def tile_rows(j):
        return pl.ds(pl.multiple_of(j * LANES, LANES), LANES)

    def class_rows(j):
        return tile_rows(jnp.clip(j - jd + classes - 1, 0, classes - 1))

    def score_tile(j):
        d = jnp.dot(ki_ref[tile_rows(j), :], iqs_s[...], preferred_element_type=F32)
        acc = jnp.zeros((LANES, tq), F32)
        for h in range(IDX_HEADS):
            acc = acc + wt[h:h + 1] * jnp.maximum(d[:, h * tq:(h + 1) * tq], 0.0)
        return acc

    def score_trip(js, carry):
        for j in js:
            key = jnp.where(adm_ref[class_rows(j), :] > 0.0, _order_key(score_tile(j)), _f32_key_const(NEG_INF))
            key_s[j] = key
            srt = _sort_desc([key[g * SUBLANES:(g + 1) * SUBLANES] for g in range(GROUP)])
            for g in range(GROUP):
                ks_s[j, g * SUBLANES:(g + 1) * SUBLANES] = srt[g]
        return carry

    _tile_loop(nt, score_trip, 0)
    nquad = (nt + 3) // 4

    def fill_body(j, carry):
        ks_s[j] = jnp.full((LANES, tq), INT_MIN, I32)
        return carry

    lax.fori_loop(nt, nquad * 4, fill_body, 0)

    def count(pred_tile):
        def trip(js, acc):
            for j in js:
                acc = acc + _fold8(pred_tile(j))
            return acc
        return jnp.sum(_tile_loop(nt, trip, jnp.zeros((SUBLANES, tq), F32)), axis=0, keepdims=True)

    def count_ge(cand):
        def one(m):
            return jnp.where(m, 1.0, 0.0)

        def trip(q, accs):
            a8, a4, a2, a1 = accs
            for u in range(4):
                v = [ks_s[4 * q + u, g * SUBLANES:(g + 1) * SUBLANES] for g in range(GROUP)]
                t1 = v[7] >= cand
                t2 = jnp.where(t1, v[11], v[3]) >= cand
                t3 = jnp.where(t1, jnp.where(t2, v[13], v[9]), jnp.where(t2, v[5], v[1])) >= cand
                t4 = jnp.where(t1, jnp.where(t2, jnp.where(t3, v[14], v[12]), jnp.where(t3, v[10], v[8])),
                               jnp.where(t2, jnp.where(t3, v[6], v[4]), jnp.where(t3, v[2], v[0]))) >= cand
                a8, a4, a2 = a8 + one(t1), a4 + one(t2), a2 + one(t3)
                a1 = a1 + one(t4) + one(v[15] >= cand)
            return a8, a4, a2, a1

        zero = jnp.zeros((SUBLANES, tq), F32)
        a8, a4, a2, a1 = lax.fori_loop(0, nquad, trip, (zero, zero, zero, zero))
        return jnp.sum(8.0 * a8 + 4.0 * a4 + 2.0 * a2 + a1, axis=0, keepdims=True)

    kf = float(n_sel)

    def bit_body(it, thr):
        cand = thr + lax.shift_left(jnp.int32(1), 31 - it)
        return jnp.where(count_ge(cand) >= kf, cand, thr)

    thr = lax.fori_loop(0, 32, bit_body, jnp.full((1, tq), INT_MIN, I32))

    need = kf - count(lambda j: jnp.where(key_s[j] > thr, 1.0, 0.0))
    n_tied = count(lambda j: jnp.where(key_s[j] == thr, 1.0, 0.0))
    surplus = jnp.where(thr > HALF_NEG_KEY, n_tied - need, 0.0)

    def tie_search():
        def tie_body(it, j0):
            cand = j0 + lax.shift_left(jnp.int32(1), idx_bits - 1 - it)
            cnt = count(lambda j: jnp.where(key_s[j] == thr, jnp.where(krow + j * LANES < cand, 1.0, 0.0), 0.0))
            return jnp.where(cnt < need, cand, j0)
        return lax.fori_loop(0, idx_bits, tie_body, jnp.zeros((1, tq), I32))

    j0 = lax.cond(jnp.max(surplus) > 0.0, tie_search, lambda: jnp.full((1, tq), INT_MAX, I32))

    def mask_trip(js, carry):
        for j in js:
            key = key_s[j]
            sel = jnp.where(key > thr, 1.0, jnp.where(key == thr, jnp.where(krow + j * LANES <= j0, 1.0, 0.0), 0.0))
            sel = jnp.where(key > HALF_NEG_KEY, sel, 0.0)
            mask_s[j] = jnp.where(sel > 0.0, 0.0, NEG_INF)
        return carry

    _tile_loop(nt, mask_trip, 0)

    scale = ATT_HEAD_DIM ** -0.5
    heads = range(ATT_HEADS)
    qh = [qt[h * ATT_HEAD_DIM:(h + 1) * ATT_HEAD_DIM] for h in heads]

    def log_trip(js, mx):
        mx = list(mx)
        for j in js:
            band_rows = class_rows(j)
            msk = mask_s[j]
            for h in heads:
                kh = k_ref[tile_rows(j), h * ATT_HEAD_DIM:(h + 1) * ATT_HEAD_DIM]
                lg = jnp.dot(kh, qh[h], preferred_element_type=F32) * scale + band_ref[h, band_rows, :] + msk
                log_s[h, j] = lg
                mx[h] = jnp.maximum(mx[h], _fold8(lg, jnp.maximum))
        return tuple(mx)

    mx = _tile_loop(nt, log_trip, tuple(jnp.full((SUBLANES, tq), NEG_INF, F32) for _ in heads))
    m = [jnp.max(mx[h], axis=0, keepdims=True) for h in heads]
    acc_s[...] = jnp.zeros(acc_s.shape, F32)

    def pv_trip(js, ls):
        ls = list(ls)
        for h in heads:
            acc = acc_s[h]
            for j in js:
                p = jnp.exp(log_s[h, j] - m[h])
                ls[h] = ls[h] + _fold8(p)
                vth = vt_ref[j, h * ATT_HEAD_DIM:(h + 1) * ATT_HEAD_DIM, :]
                acc = acc + jnp.dot(vth, p.astype(BF16), preferred_element_type=F32)
            acc_s[h] = acc
        return tuple(ls)

    ls = _tile_loop(nt, pv_trip, tuple(jnp.zeros((SUBLANES, tq), F32) for _ in heads))
    for h in heads:
        ot = acc_s[h] / jnp.sum(ls[h], axis=0, keepdims=True)
        for u in range(qb):
            o_ref[u * LANES:(u + 1) * LANES, h * ATT_HEAD_DIM:(h + 1) * ATT_HEAD_DIM] = jnp.transpose(
                ot[:, u * LANES:(u + 1) * LANES]).astype(o_ref.dtype)


def _attention(aqt, iqt, tailt, keys_k, keys_v, keys_i, band, b, t, past_len):
    l = past_len + t
    tq = LANES
    nqb = -(-t // tq)
    tp = nqb * tq
    assert past_len % LANES == 0 and (t % tq == 0 or nqb == 1)
    jd0 = past_len // LANES
    ntiles = jd0 + nqb
    ntp = ntiles
    lp = ntp * LANES
    n_sel = min(TOPK_MAX, l // 4)
    if tp != t:
        padq = lambda a: jnp.pad(a.reshape(-1, b, t), ((0, 0), (0, 0), (0, tp - t))).reshape(-1, b * tp)
        aqt, iqt, tailt = padq(aqt), padq(iqt), padq(tailt)
    w = ATT_HEADS * ATT_HEAD_DIM
    pad = ((0, 0), (0, lp - l), (0, 0))
    kk = jnp.pad(keys_k, pad)
    vt = jnp.pad(keys_v, pad).reshape(b, ntp, LANES, w).transpose(0, 1, 3, 2)
    ki2 = jnp.pad(keys_i, ((0, 0), (0, lp - l), (0, LANES - IDX_DIM)))
    cc = jnp.arange(LANES)[:, None]
    tt = jnp.arange(LANES)[None, :]
    adm = ((cc // CHUNK <= tt // CHUNK) & (cc < l - (ntiles - 1) * LANES)).astype(F32)
    qb = 2 if nqb % 2 == 0 else 1
    wq = qb * LANES
    classes = BAND_TILES + qb - 1
    band_w, adm_w = [], []
    for r in range(classes):
        rel = [r - (classes - 1) + (qb - 1 - u) for u in range(qb)]
        band_w.append(jnp.concatenate(
            [band[:, min(max(x + BAND_TILES - 1, 0), BAND_TILES - 1) * LANES:][:, :LANES] for x in rel], axis=2))
        adm_w.append(jnp.concatenate(
            [adm if x == 0 else jnp.full((LANES, LANES), 1.0 if x < 0 else 0.0, F32) for x in rel], axis=1))
    band_w = jnp.concatenate(band_w, axis=1)
    adm_w = jnp.concatenate(adm_w, axis=0)
    nsteps = nqb // qb
    qspec = lambda width: pl.BlockSpec((wq, width), lambda bi, qi: (bi * nsteps + qi, 0))
    qtspec = lambda rows: pl.BlockSpec((rows, wq), lambda bi, qi: (0, bi * nsteps + qi))
    kspec = lambda width: pl.BlockSpec((None, lp, width), lambda bi, qi: (bi, 0, 0))
    kern = functools.partial(_attention_kernel, jd0=jd0, n_sel=n_sel, idx_bits=max((lp - 1).bit_length(), 1))
    out = pl.pallas_call(
        kern,
        grid=(b, nsteps),
        in_specs=[qtspec(w), qtspec(IDX_HEADS * IDX_DIM), qtspec(LANES),
                  kspec(w), pl.BlockSpec((None, ntp, w, LANES), lambda bi, qi: (bi, 0, 0, 0)), kspec(LANES),
                  pl.BlockSpec((ATT_HEADS, classes * LANES, wq), lambda bi, qi: (0, 0, 0)),
                  pl.BlockSpec((classes * LANES, wq), lambda bi, qi: (0, 0))],
        out_specs=qspec(w),
        out_shape=jax.ShapeDtypeStruct((b * tp, w), BF16),
        scratch_shapes=[pltpu.VMEM((LANES, IDX_HEADS * wq), BF16),
                        pltpu.VMEM((ntp, LANES, wq), I32), pltpu.VMEM((-(-ntp // 4) * 4, LANES, wq), I32),
                        pltpu.VMEM((ntp, LANES, wq), F32),
                        pltpu.VMEM((ATT_HEADS, ntp, LANES, wq), F32), pltpu.VMEM((ATT_HEADS, ATT_HEAD_DIM, wq), F32)],
        compiler_params=_cparams(("parallel", "arbitrary")),
        name="attention",
    )(aqt, iqt, tailt, kk, vt, ki2, band_w, adm_w)
    if tp != t:
        out = out.reshape(b, tp, w)[:, :t].reshape(b * t, w)
    return out


def _store_tile_major(ref, x):
    rows, width = x.shape
    nk = width // LANES
    for c in range(nk):
        ref[pl.ds(c, rows, stride=nk), :] = x[:, c * LANES:(c + 1) * LANES]


def _load_tile_major(ref, rows, nk, first=0, stride=None):
    stride = nk if stride is None else stride
    return jnp.concatenate([ref[pl.ds(first + c, rows, stride=stride), :] for c in range(nk)], axis=1)


def _outproj_kernel(x_ref, oret_ref, oatt_ref, wo_ref, g_ref, wr_ref, br_ref, *rest, own_steps):
    outs = rest[-4:]
    step = pl.program_id(0)

    @pl.when(step < own_steps)
    def _():
        _outproj_body(x_ref, oret_ref, oatt_ref, wo_ref, g_ref, wr_ref, br_ref, *outs)

    @pl.when(step >= own_steps)
    def _():
        outs[1][...] = jnp.zeros(outs[1].shape, F32)


def _outproj_body(x_ref, oret_ref, oatt_ref, wo_ref, g_ref, wr_ref, br_ref, x1_ref, h2_ref, e_ref, gt_ref):
    mixed = jnp.concatenate([oret_ref[...], oatt_ref[...]], axis=1)
    x1 = x_ref[...] + jnp.dot(mixed, wo_ref[...], preferred_element_type=F32)
    x1_ref[...] = x1
    h2 = (x1 * lax.rsqrt(jnp.mean(jnp.square(x1), -1, keepdims=True) + EPS)) * g_ref[...]
    _store_tile_major(h2_ref, h2)
    lg = lax.dot_general(wr_ref[...], h2.astype(BF16), (((1,), (1,)), ((), ())),
                         preferred_element_type=F32) + br_ref[...]
    tm = lg.shape[1]
    ridx = lax.broadcasted_iota(I32, (SUBLANES, tm), 0).astype(F32)
    first = lambda hit: jnp.min(jnp.where(hit, ridx, float(SUBLANES)), 0, keepdims=True)
    gl = lg[:SUBLANES]
    gmax = jnp.max(gl, 0, keepdims=True)
    p_top = 1.0 / jnp.sum(jnp.exp(gl - gmax), 0, keepdims=True)
    g_top = first(gl == gmax)
    el = jnp.zeros((EXPERTS_PER_GROUP, tm), F32)
    for g in range(N_GROUPS):
        el = jnp.where(g_top == float(g), lg[(1 + g) * SUBLANES:(2 + g) * SUBLANES], el)
    v1 = jnp.max(el, 0, keepdims=True)
    i1 = first(el == v1)
    el2 = jnp.where(ridx == i1, -jnp.inf, el)
    v2 = jnp.max(el2, 0, keepdims=True)
    i2 = first(el2 == v2)
    e2 = jnp.exp(v2 - v1)
    den = 1.0 + e2
    e_ref[...] = jnp.concatenate([g_top * EXPERTS_PER_GROUP + i1, g_top * EXPERTS_PER_GROUP + i2], 0).astype(I32)
    gt_ref[...] = jnp.concatenate([1.0 / den, e2 / den], 0) * p_top


def _outproj_router(x2, o_ret, o_att, w_out, g_ffn, w_group, b_group, w_er, b_er, h2_all, row0, rows_all):
    n, d = x2.shape
    shared = () if h2_all is None else (h2_all,)
    tm = _row_tile(n, 512)
    other = 0 if shared else rows_all - n
    if other:
        assert row0 == 0
        tm = math.gcd(tm, other)
    assert row0 % tm == 0
    blk0 = row0 // tm
    own_steps = n // tm
    nk = d // LANES
    assert N_GROUPS <= SUBLANES and EXPERTS_PER_GROUP == SUBLANES
    gpad = SUBLANES - N_GROUPS
    nrow = SUBLANES + N_EXPERTS
    rpad = -nrow % 16
    wr = jnp.concatenate([w_group.T, jnp.zeros((gpad, d), F32), w_er.reshape(d, N_EXPERTS).T,
                          jnp.zeros((rpad, d), F32)], 0).astype(BF16)
    br = jnp.concatenate([b_group, jnp.full((gpad,), NEG_INF, F32), b_er.reshape(N_EXPERTS),
                          jnp.zeros((rpad,), F32)]).reshape(nrow + rpad, 1)
    last = own_steps - 1
    rspec = lambda w: pl.BlockSpec((tm, w), lambda i: (jnp.minimum(i, last), 0))
    tspec = pl.BlockSpec((TOP_E, tm), lambda i: (0, jnp.minimum(i, last)))
    cspec = lambda r, c: pl.BlockSpec((r, c), lambda i: (0, 0))
    mw = w_out.shape[0]
    x1, h2_all, expert_t, gate_t = pl.pallas_call(
        functools.partial(_outproj_kernel, own_steps=own_steps),
        grid=(own_steps + other // tm,),
        in_specs=[rspec(d), rspec(o_ret.shape[1]), rspec(o_att.shape[1]), cspec(mw, d), cspec(1, d),
                  cspec(nrow + rpad, d), cspec(nrow + rpad, 1)] + [pl.BlockSpec(memory_space=pl.ANY) for _ in shared],
        out_specs=[rspec(d), pl.BlockSpec((tm * nk, LANES), lambda i: (blk0 + i, 0)), tspec, tspec],
        out_shape=[jax.ShapeDtypeStruct((n, d), F32), jax.ShapeDtypeStruct((rows_all * nk, LANES), F32),
                   jax.ShapeDtypeStruct((TOP_E, n), I32), jax.ShapeDtypeStruct((TOP_E, n), F32)],
        input_output_aliases={7: 1} if shared else {},
        compiler_params=_cparams(("arbitrary",)),
        name="outproj_router",
    )(x2, o_ret, o_att, w_out.astype(BF16), g_ffn.reshape(1, d), wr, br, *shared)
    return x1, h2_all, expert_t.T, gate_t.T


def _expert_kernel(blk_e_ref, tok0_ref, tok1_ref, tokn_ref, slot_ref, h_ref, wg_ref, wu_ref, wd_ref, o_ref,
                   xbuf, obuf, gsem, ssem):
    i = pl.program_id(0)
    nb = pl.num_programs(0)
    nk = wg_ref.shape[0] // LANES
    rb = obuf.shape[0] // (2 * nk)
    blk = rb * nk
    cur = i % 2
    nxt = 1 - cur
    gcur = lax.rem(i, GATHER_DEPTH)

    def lines(first, count):
        return pl.ds(pl.multiple_of(first, nk), count)

    def for_rows(start_row):
        def body(g, carry):
            for u in range(DMA_UNROLL):
                start_row(g * DMA_UNROLL + u, u % 2)
            return carry
        lax.fori_loop(0, rb // DMA_UNROLL, body, 0)

    def start_gather(tok_ref, buf):
        for_rows(lambda r, pri: pltpu.make_async_copy(
            h_ref.at[lines(tok_ref[0, 0, r], nk)], xbuf.at[lines(buf * blk + r * nk, nk)],
            gsem.at[buf]).start(priority=pri))

    def wait_gather(buf):
        pltpu.make_async_copy(h_ref.at[pl.ds(0, blk)], xbuf.at[lines(buf * blk, blk)], gsem.at[buf]).wait()

    def wait_scatter(buf):
        pltpu.make_async_copy(obuf.at[lines(buf * blk, blk)], o_ref.at[pl.ds(0, blk)], ssem.at[buf]).wait()

    @pl.when(i == 0)
    def _():
        start_gather(tok0_ref, 0)

        @pl.when(nb >= 2)
        def _():
            start_gather(tok1_ref, 1)

    @pl.when(i + GATHER_DEPTH - 1 < nb)
    def _():
        start_gather(tokn_ref, lax.rem(i + GATHER_DEPTH - 1, GATHER_DEPTH))

    wait_gather(gcur)
    gbase = gcur * blk
    base = cur * blk
    xb = jnp.concatenate([xbuf[pl.ds(gbase + c, rb, stride=nk), :] for c in range(nk)], axis=1).astype(BF16)
    hid = jax.nn.silu(jnp.dot(xb, wg_ref[...].astype(BF16), preferred_element_type=F32))
    hid = hid * jnp.dot(xb, wu_ref[...].astype(BF16), preferred_element_type=F32)
    out = jnp.dot(hid.astype(BF16), wd_ref[...].astype(BF16), preferred_element_type=F32)

    @pl.when(i >= 2)
    def _():
        wait_scatter(cur)

    for c in range(nk):
        obuf[pl.ds(base + c, rb, stride=nk), :] = out[:, c * LANES:(c + 1) * LANES]

    for_rows(lambda r, pri: pltpu.make_async_copy(
        obuf.at[lines(base + r * nk, nk)], o_ref.at[lines(slot_ref[0, 0, r], nk)], ssem.at[cur]).start(priority=pri))

    @pl.when(i == nb - 1)
    def _():
        wait_scatter(cur)

        @pl.when(nb >= 2)
        def _():
            wait_scatter(nxt)


def _experts(h2, src_tok, out_slot, blk_e, w_gate, w_up, w_down, rb):
    d, de = w_gate.shape[1:]
    nk = d // LANES
    rows = src_tok.shape[0]
    nb = rows // rb
    idx3 = lambda a: (a * nk).reshape(nb, 1, rb)
    ispec = lambda f: pl.BlockSpec((1, 1, rb), f, memory_space=pltpu.SMEM)
    wspec = lambda r, c: pl.BlockSpec((None, r, c), lambda i, be: (be[i], 0, 0))
    return pl.pallas_call(
        _expert_kernel,
        grid_spec=pltpu.PrefetchScalarGridSpec(
            num_scalar_prefetch=1,
            grid=(nb,),
            in_specs=[ispec(lambda i, be: (0, 0, 0)), ispec(lambda i, be: (min(1, nb - 1), 0, 0)),
                      ispec(lambda i, be: (jnp.minimum(i + GATHER_DEPTH - 1, nb - 1), 0, 0)),
                      ispec(lambda i, be: (i, 0, 0)), pl.BlockSpec(memory_space=pl.ANY),
                      wspec(d, de), wspec(d, de), wspec(de, d)],
            out_specs=pl.BlockSpec(memory_space=pl.ANY),
            scratch_shapes=[pltpu.VMEM((GATHER_DEPTH * rb * nk, LANES), F32), pltpu.VMEM((2 * rb * nk, LANES), F32),
                            pltpu.SemaphoreType.DMA((GATHER_DEPTH,)), pltpu.SemaphoreType.DMA((2,))],
        ),
        out_shape=jax.ShapeDtypeStruct((rows * nk, LANES), F32),
        compiler_params=_cparams(("arbitrary",)),
        name="experts",
    )(blk_e, idx3(src_tok), idx3(src_tok), idx3(src_tok), idx3(out_slot), h2, w_gate, w_up, w_down)


def _route_plan(expert, rb):
    n = expert.shape[0]
    a = n * TOP_E
    flat_e = expert.reshape(a)
    counts = jnp.sum((flat_e[:, None] == jnp.arange(N_EXPERTS, dtype=I32)[None, :]).astype(I32), axis=0)
    padded = (counts + rb - 1) // rb * rb
    ends = jnp.cumsum(padded)
    n_blocks = (a + N_EXPERTS * (rb - 1) + rb - 1) // rb
    rows = n_blocks * rb
    q = jnp.arange(rb - 1, dtype=I32)[None, :]
    e = jnp.arange(N_EXPERTS, dtype=I32)[:, None]
    pad_key = jnp.where(q < (padded - counts)[:, None], 2 * e + 1, 2 * N_EXPERTS).reshape(-1)
    spare = jnp.full((rows - a - N_EXPERTS * (rb - 1),), 2 * N_EXPERTS, I32)
    perm = jnp.argsort(jnp.concatenate([2 * flat_e, pad_key, spare]), stable=True).astype(I32)
    src_tok = jnp.where(perm < a, perm // TOP_E, 0)
    out_slot = perm
    blk_e = jnp.minimum(jnp.sum((ends[None, :] <= jnp.arange(n_blocks, dtype=I32)[:, None] * rb).astype(I32), axis=1),
                        N_EXPERTS - 1)
    return src_tok, out_slot, blk_e


def _combine_kernel(x1_ref, eo_ref, gt_ref, g_ref, y_ref):
    tm, d = x1_ref.shape
    nk = d // LANES
    gt = gt_ref[...]
    moe = _load_tile_major(eo_ref, tm, nk, 0, TOP_E * nk) * gt[:, 0:1]
    moe = moe + _load_tile_major(eo_ref, tm, nk, nk, TOP_E * nk) * gt[:, 1:2]
    x = x1_ref[...] + moe
    y_ref[...] = (x * lax.rsqrt(jnp.mean(jnp.square(x), -1, keepdims=True) + EPS)) * g_ref[...]


def _combine(x1, eo2, gate, g_final, row0):
    n, d = x1.shape
    tm = _row_tile(n, 512)
    assert row0 % tm == 0
    blk0 = row0 // tm
    nk = d // LANES
    rspec = lambda w: pl.BlockSpec((tm, w), lambda i: (i, 0))
    return pl.pallas_call(
        _combine_kernel,
        grid=(n // tm,),
        in_specs=[rspec(d), pl.BlockSpec((tm * TOP_E * nk, LANES), lambda i: (blk0 + i, 0)), rspec(TOP_E),
                  pl.BlockSpec((1, d), lambda i: (0, 0))],
        out_specs=rspec(d),
        out_shape=jax.ShapeDtypeStruct((n, d), F32),
        compiler_params=_cparams(("parallel",)),
        name="combine",
    )(x1, eo2, gate, g_final.reshape(1, d))


def _mixer(x, past_len, s_ret, past_k, past_v, past_ki, band, params, h2_all, row0, rows_all):
    (rel_bias, g_mix, w_in, g_ret, w_out, g_ffn, w_group, b_group, w_er, b_er, w_gate, w_up, w_down) = params
    b, t, d = x.shape
    n = b * t
    x2 = x.reshape(n, d)
    rq, rk, rv, rg, ak, av, akb, avb, ik, aqt, iqt, tailt = _inproj(x2, g_mix, w_in)
    o_ret, s_new = _retention(rq, rk, rv, rg, s_ret, g_ret, b, t, past_len)
    aw = ATT_HEADS * ATT_HEAD_DIM
    keys_k = jnp.concatenate([past_k.reshape(b, past_len, aw).astype(BF16), akb.reshape(b, t, aw)], 1)
    keys_v = jnp.concatenate([past_v.reshape(b, past_len, aw).astype(BF16), avb.reshape(b, t, aw)], 1)
    keys_i = jnp.concatenate([past_ki.astype(BF16), ik.reshape(b, t, IDX_DIM).astype(BF16)], 1)
    o_att = _attention(aqt, iqt, tailt, keys_k, keys_v, keys_i, band, b, t, past_len)
    x1, h2_all, expert, gate = _outproj_router(x2, o_ret, o_att, w_out, g_ffn, w_group, b_group, w_er, b_er,
                                               h2_all, row0, rows_all)
    caches = (ak.reshape(1, b, t, ATT_HEADS, ATT_HEAD_DIM), av.reshape(1, b, t, ATT_HEADS, ATT_HEAD_DIM),
              ik.reshape(1, b, t, IDX_DIM), s_new[None])
    return x1, h2_all, expert, gate, caches


def kernel(x_prompt, x_sample, cache_attn_k, cache_attn_v, cache_idx_k, state_ret, rel_bias, g_mix, w_in, g_ret, w_out, g_ffn, w_group, b_group, w_expert_router, b_expert_router, w_gate, w_up, w_down, g_final):
    assert g_mix.shape[0] == 1, "single-layer model"
    params = (rel_bias, g_mix[0], w_in[0], g_ret[0], w_out[0], g_ffn[0], w_group[0], b_group[0],
              w_expert_router[0], b_expert_router[0], w_gate[0], w_up[0], w_down[0])
    band = _bias_band(rel_bias)
    nb = x_prompt.shape[0]
    past_len = cache_attn_k.shape[2]
    dt = x_prompt.dtype
    empty_kv = jnp.zeros((nb, 0, ATT_HEADS, ATT_HEAD_DIM), dt)
    empty_ki = jnp.zeros((nb, 0, IDX_DIM), dt)
    s0 = jnp.zeros((nb, RET_HEADS, RET_DK, RET_DV), dt)
    n_p = x_prompt.shape[0] * x_prompt.shape[1]
    n_s = x_sample.shape[0] * x_sample.shape[1]
    x1p, h2_all, ep, gp, caches_p = _mixer(x_prompt, 0, s0, empty_kv, empty_kv, empty_ki, band, params,
                                           None, 0, n_p + n_s)
    x1s, h2_all, es, gs, caches_s = _mixer(x_sample, past_len, state_ret[0], cache_attn_k[0], cache_attn_v[0],
                                           cache_idx_k[0], band, params, h2_all, n_p, n_p + n_s)
    rb = 256 if n_p + n_s >= 8192 else 128
    src_tok, out_slot, blk_e = _route_plan(jnp.concatenate([ep, es], 0), rb)
    eo = _experts(h2_all, src_tok, out_slot, blk_e, w_gate[0], w_up[0], w_down[0], rb)
    yp = _combine(x1p, eo, gp, g_final, 0).reshape(x_prompt.shape)
    ys = _combine(x1s, eo, gs, g_final, n_p).reshape(x_sample.shape)
    return (yp, ys) + caches_p + caches_s
```
